```python
import jax
import jax.numpy as jnp
from jax import lax
import numpy as np


D_MODEL = 4096
BATCH = 2
SEQ = 8192
DEPTH = 2

HEAD_DIM = 128
MIX_WIDTH = D_MODEL
A_HEADS = MIX_WIDTH // (2 * HEAD_DIM)
B_HEADS = MIX_WIDTH // (2 * HEAD_DIM)
A_WIDTH = A_HEADS * HEAD_DIM
B_WIDTH = B_HEADS * HEAD_DIM
HGRN_CHUNK = 64
SB_Q_BLOCK = 128
NSA_HEADS = MIX_WIDTH // HEAD_DIM
NSA_KV_HEADS = 4
NSA_GROUP = NSA_HEADS // NSA_KV_HEADS
KV_WIDTH = NSA_KV_HEADS * HEAD_DIM
CMP_LEN = 32
CMP_STRIDE = 16
SLC_LEN = 64
SLC_TOP = 16
WINDOW = 512
NSA_Q_BLOCK = 64
N_MEM = 256
XA_HEADS = 4
XA_WIDTH = XA_HEADS * HEAD_DIM
D_FF = ((8 * D_MODEL // 3 + 255) // 256) * 256
CONV_W = 3
ROPE_THETA = 10000.0
DN_ALPHA = (2 * DEPTH) ** 0.25
DN_BETA = (8 * DEPTH) ** -0.25
N_EVEN = (DEPTH + 1) // 2
N_ODD = DEPTH // 2
LN_EPS = 1e-5
RMS_EPS = 1e-6
NEG_INF = -1e30
FORCE_SCORE = 1e9

kernel_name = 'hybrid_hgrn2_stickbreak_nsa_deepnorm'


def layer_norm(x, g, b):
    xf = x.astype(jnp.float32)
    mu = jnp.mean(xf, axis=-1, keepdims=True)
    var = jnp.mean(jnp.square(xf - mu), axis=-1, keepdims=True)
    y = (xf - mu) * lax.rsqrt(var + LN_EPS) * g.astype(jnp.float32) + b.astype(jnp.float32)
    return y.astype(x.dtype)


def rope(t, pos):
    half = t.shape[-1] // 2
    inv_freq = ROPE_THETA ** (-jnp.arange(half, dtype=jnp.float32) / half)
    ang = pos.astype(jnp.float32)[:, None] * inv_freq[None, :]
    cos = jnp.cos(ang)[None, :, None, :]
    sin = jnp.sin(ang)[None, :, None, :]
    tf = t.astype(jnp.float32)
    t1, t2 = tf[..., :half], tf[..., half:]
    return jnp.concatenate([t1 * cos - t2 * sin, t2 * cos + t1 * sin], axis=-1).astype(t.dtype)


def masked_softmax(s, mask):
    s = jnp.where(mask, s, NEG_INF)
    return jnp.where(mask, jax.nn.softmax(s, axis=-1), 0.0)


def hgrn2_chunked(q, f_logit, i, lb):
    B, S, H, Dk = q.shape
    Dv = i.shape[-1]
    C = HGRN_CHUNK
    NC = S // C
    zf = f_logit.astype(jnp.float32)
    log_f = jnp.log(lb + (1.0 - lb) * jax.nn.sigmoid(zf))
    k = (1.0 - lb) * jax.nn.sigmoid(-zf)

    def chunks(t):
        return t.reshape(B, NC, C, H, t.shape[-1]).transpose(1, 0, 3, 2, 4)

    qc = chunks(q.astype(jnp.float32))
    kc = chunks(k)
    vc = chunks(i.astype(jnp.float32))
    bc = jnp.cumsum(chunks(log_f), axis=3)
    causal = jnp.tril(jnp.ones((C, C), dtype=bool))[:, :, None]

    def step(state, inp):
        q_, k_, v_, b_ = inp
        diff = b_[:, :, :, None, :] - b_[:, :, None, :, :]
        decay = jnp.exp(jnp.where(causal, diff, -jnp.inf))
        scores = jnp.einsum('bhtd,bhtsd->bhts', q_, decay * k_[:, :, None, :, :])
        o = jnp.einsum('bhts,bhsv->bhtv', scores, v_)
        o = o + jnp.einsum('bhtd,bhdv->bhtv', q_ * jnp.exp(b_), state)
        b_last = b_[:, :, -1, :]
        k_dec = k_ * jnp.exp(b_last[:, :, None, :] - b_)
        state = state * jnp.exp(b_last)[..., None] + jnp.einsum('bhsd,bhsv->bhdv', k_dec, v_)
        return state, o

    state0 = jnp.zeros((B, H, Dk, Dv), jnp.float32)
    _, o = lax.scan(step, state0, (qc, kc, vc, bc))
    return o.transpose(1, 0, 3, 2, 4).reshape(B, S, H, Dv)


def stick_breaking_attention(q, k, v):
    B, S, H, Dh = q.shape
    nb = S // SB_Q_BLOCK
    scale = Dh ** -0.5
    key_pos = jnp.arange(S)
    qb = q.reshape(B, nb, SB_Q_BLOCK, H, Dh).transpose(1, 0, 3, 2, 4)

    def block(args):
        q_blk, bi = args
        z = jnp.einsum('bhqd,bshd->bhqs', q_blk, k, preferred_element_type=jnp.float32) * scale
        t = bi * SB_Q_BLOCK + jnp.arange(SB_Q_BLOCK)
        mask = key_pos[None, :] < t[:, None]
        log_1m = jnp.where(mask, jax.nn.log_sigmoid(-z), 0.0)
        rev = lax.cumsum(log_1m, axis=3, reverse=True) - log_1m
        w = jnp.where(mask, jnp.exp(jax.nn.log_sigmoid(z) + rev), 0.0)
        return jnp.einsum('bhqs,bshd->bqhd', w.astype(v.dtype), v)

    out = lax.map(block, (qb, jnp.arange(nb)))
    return out.transpose(1, 0, 2, 3, 4).reshape(B, S, H, Dh)


def hgrn_stickbreak_mixer(h, w_in, lb, norm_w, w_out):
    B, S, _ = h.shape
    proj = h @ w_in
    cuts = [A_WIDTH, 2 * A_WIDTH, 3 * A_WIDTH, 4 * A_WIDTH, 4 * A_WIDTH + B_WIDTH, 4 * A_WIDTH + 2 * B_WIDTH]
    a_q, a_f, a_i, a_g, b_q, b_k, b_v = jnp.split(proj, cuts, axis=-1)

    def heads(t, n):
        return t.reshape(B, S, n, HEAD_DIM)

    o_a = hgrn2_chunked(heads(a_q, A_HEADS), heads(a_f, A_HEADS), heads(a_i, A_HEADS),
                        lb.reshape(A_HEADS, HEAD_DIM))
    gate = jax.nn.silu(heads(a_g, A_HEADS).astype(jnp.float32))
    o_a = o_a * lax.rsqrt(jnp.mean(jnp.square(o_a), axis=-1, keepdims=True) + RMS_EPS) \
        * norm_w.astype(jnp.float32) * gate
    o_b = stick_breaking_attention(heads(b_q, B_HEADS), heads(b_k, B_HEADS), heads(b_v, B_HEADS))
    o = jnp.concatenate([o_a.astype(h.dtype).reshape(B, S, A_WIDTH), o_b.reshape(B, S, B_WIDTH)], axis=-1)
    return o @ w_out


def compress_blocks(t, pos_emb, w1, w2, n_cmp):
    idx = jnp.arange(n_cmp)[:, None] * CMP_STRIDE + jnp.arange(CMP_LEN)[None, :]
    blocks = t[:, idx] + pos_emb[None, None, :, None, :]
    hid = jax.nn.gelu(jnp.einsum('bnlgd,lde->bnge', blocks, w1), approximate=False)
    return jnp.einsum('bnge,ef->bngf', hid, w2)


def nsa_mixer(h, w_in, cmp_pos, cmp_w1, cmp_w2, w_out, pos):
    B, S, _ = h.shape
    G, R, Dh = NSA_KV_HEADS, NSA_GROUP, HEAD_DIM
    scale = Dh ** -0.5
    proj = h @ w_in
    q_w = NSA_HEADS * Dh
    cuts = [q_w + j * KV_WIDTH for j in range(7)]
    q, kc, vc, ks, vs, kw, vw, gl = jnp.split(proj, cuts, axis=-1)
    q = q.reshape(B, S, NSA_HEADS, Dh)
    q_nope = q.reshape(B, S, G, R, Dh)
    q_rot = rope(q, pos).reshape(B, S, G, R, Dh)
    kc, vc, ks, vs, kw, vw = [t.reshape(B, S, G, Dh) for t in (kc, vc, ks, vs, kw, vw)]
    ks = rope(ks, pos)
    kw = rope(kw, pos)
    gates = jax.nn.sigmoid(gl.astype(jnp.float32)).reshape(B, S, G, R, 3)

    n_cmp = (S - CMP_LEN) // CMP_STRIDE + 1
    k_cmp = compress_blocks(kc, cmp_pos[0], cmp_w1[0], cmp_w2[0], n_cmp)
    v_cmp = compress_blocks(vc, cmp_pos[1], cmp_w1[1], cmp_w2[1], n_cmp)
    cmp_start = jnp.arange(n_cmp) * CMP_STRIDE
    cmp_end = cmp_start + CMP_LEN - 1
    n_slc = S // SLC_LEN
    n_top = min(SLC_TOP, n_slc)
    slc_start = jnp.arange(n_slc) * SLC_LEN
    overlap = ((cmp_start[:, None] < slc_start[None, :] + SLC_LEN)
               & (cmp_start[:, None] + CMP_LEN > slc_start[None, :])).astype(jnp.float32)
    k_blocks = ks.reshape(B, n_slc, SLC_LEN, G, Dh).transpose(0, 3, 1, 2, 4)
    v_blocks = vs.reshape(B, n_slc, SLC_LEN, G, Dh).transpose(0, 3, 1, 2, 4)
    b_ix = jnp.arange(B)[:, None, None, None]
    g_ix = jnp.arange(G)[None, :, None, None]
    kw_pad = jnp.pad(kw, ((0, 0), (WINDOW, 0), (0, 0), (0, 0)))
    vw_pad = jnp.pad(vw, ((0, 0), (WINDOW, 0), (0, 0), (0, 0)))
    QB = NSA_Q_BLOCK
    nb = S // QB

    def block(bi):
        q0 = bi * QB
        t = q0 + jnp.arange(QB)
        qr = lax.dynamic_slice_in_dim(q_rot, q0, QB, axis=1)
        qn = lax.dynamic_slice_in_dim(q_nope, q0, QB, axis=1)
        gt = lax.dynamic_slice_in_dim(gates, q0, QB, axis=1)
        s_c = jnp.einsum('bqgrd,bngd->bgrqn', qn, k_cmp, preferred_element_type=jnp.float32) * scale
        p_c = masked_softmax(s_c, cmp_end[None, :] <= t[:, None])
        o_c = jnp.einsum('bgrqn,bngd->bqgrd', p_c.astype(v_cmp.dtype), v_cmp)
        imp = jnp.einsum('bgrqn,nj->bgqj', p_c, overlap)
        cur = t // SLC_LEN
        j = jnp.arange(n_slc)
        forced = (j[None, :] == 0) | (j[None, :] == cur[:, None]) | (j[None, :] == cur[:, None] - 1)
        allowed = j[None, :] * SLC_LEN <= t[:, None]
        score = jnp.where(forced, FORCE_SCORE, jnp.where(allowed, imp, -1.0))
        _, idx = lax.top_k(score, n_top)
        k_sel = k_blocks[b_ix, g_ix, idx]
        v_sel = v_blocks[b_ix, g_ix, idx]
        tok = idx[..., None] * SLC_LEN + jnp.arange(SLC_LEN)
        smask = (tok <= t[None, None, :, None, None]).reshape(B, G, 1, QB, n_top * SLC_LEN)
        s_s = jnp.einsum('bqgrd,bgqkld->bgrqkl', qr, k_sel, preferred_element_type=jnp.float32) * scale
        p_s = masked_softmax(s_s.reshape(B, G, R, QB, n_top * SLC_LEN), smask)
        o_s = jnp.einsum('bgrqkl,bgqkld->bqgrd',
                         p_s.reshape(B, G, R, QB, n_top, SLC_LEN).astype(v_sel.dtype), v_sel)
        kwb = lax.dynamic_slice_in_dim(kw_pad, q0, QB + WINDOW, axis=1)
        vwb = lax.dynamic_slice_in_dim(vw_pad, q0, QB + WINDOW, axis=1)
        kp = q0 - WINDOW + jnp.arange(QB + WINDOW)
        wmask = (kp[None, :] >= 0) & (kp[None, :] <= t[:, None]) & (kp[None, :] > t[:, None] - WINDOW)
        s_w = jnp.einsum('bqgrd,bkgd->bgrqk', qr, kwb, preferred_element_type=jnp.float32) * scale
        p_w = masked_softmax(s_w, wmask)
        o_w = jnp.einsum('bgrqk,bkgd->bqgrd', p_w.astype(vwb.dtype), vwb)
        out = gt[..., 0:1] * o_c + gt[..., 1:2] * o_s + gt[..., 2:3] * o_w
        return out.astype(h.dtype)

    o = lax.map(block, jnp.arange(nb))
    o = o.transpose(1, 0, 2, 3, 4, 5).reshape(B, S, NSA_HEADS * Dh)
    return o @ w_out


def memory_cross_attention(h, mem, w_q, w_kv, w_o):
    B, S, _ = h.shape
    q = (h @ w_q).reshape(B, S, XA_HEADS, HEAD_DIM)
    kv = (mem @ w_kv).reshape(mem.shape[0], mem.shape[1], 2, XA_HEADS, HEAD_DIM)
    k, v = kv[:, :, 0], kv[:, :, 1]
    s = jnp.einsum('bshd,bmhd->bhsm', q, k, preferred_element_type=jnp.float32) * HEAD_DIM ** -0.5
    p = jax.nn.softmax(s, axis=-1)
    o = jnp.einsum('bhsm,bmhd->bshd', p.astype(v.dtype), v).reshape(B, S, XA_WIDTH)
    return o @ w_o


def conv_glu_ffn(h, w_up, conv_w, w_down):
    S = h.shape[1]
    a, u = jnp.split(h @ w_up, 2, axis=-1)
    a_pad = jnp.pad(a, ((0, 0), (CONV_W - 1, 0), (0, 0)))
    c = conv_w[CONV_W - 1] * a
    for tap in range(CONV_W - 1):
        c = c + conv_w[tap] * a_pad[:, tap:tap + S]
    return (jax.nn.gelu(c, approximate=False) * u) @ w_down


def setup_inputs(seed: int = 0) -> dict:
    key = jax.random.key(seed)
    ks = jax.random.split(key, 20)
    D = D_MODEL
    ab_in = 4 * A_WIDTH + 3 * B_WIDTH
    nsa_in = NSA_HEADS * HEAD_DIM + 6 * KV_WIDTH + 3 * NSA_HEADS

    def nrm(k, shape, s):
        return jax.random.normal(k, shape, jnp.float32) * s

    return {
        'x': nrm(ks[0], (BATCH, SEQ, D), 1.0),
        'mem': nrm(ks[1], (BATCH, N_MEM, D), 1.0),
        'ab_w_in': nrm(ks[2], (N_EVEN, D, ab_in), D ** -0.5),
        'hgrn_lb': nrm(ks[3], (N_EVEN + 1, A_WIDTH), 0.1),
        'hgrn_norm_w': 1.0 + nrm(ks[4], (N_EVEN, HEAD_DIM), 0.02),
        'ab_w_out': nrm(ks[5], (N_EVEN, A_WIDTH + B_WIDTH, D), DN_BETA * (A_WIDTH + B_WIDTH) ** -0.5),
        'nsa_w_in': nrm(ks[6], (N_ODD, D, nsa_in), D ** -0.5),
        'nsa_cmp_pos': nrm(ks[7], (N_ODD, 2, CMP_LEN, HEAD_DIM), 0.1),
        'nsa_cmp_w1': nrm(ks[8], (N_ODD, 2, CMP_LEN, HEAD_DIM, HEAD_DIM), (CMP_LEN * HEAD_DIM) ** -0.5),
        'nsa_cmp_w2': nrm(ks[9], (N_ODD, 2, HEAD_DIM, HEAD_DIM), HEAD_DIM ** -0.5),
        'nsa_w_out': nrm(ks[10], (N_ODD, NSA_HEADS * HEAD_DIM, D), DN_BETA * (NSA_HEADS * HEAD_DIM) ** -0.5),
        'xa_w_q': nrm(ks[11], (DEPTH, D, XA_WIDTH), D ** -0.5),
        'xa_w_kv': nrm(ks[12], (DEPTH, D, 2 * XA_WIDTH), D ** -0.5),
        'xa_w_o': nrm(ks[13], (DEPTH, XA_WIDTH, D), DN_BETA * XA_WIDTH ** -0.5),
        'ffn_w_up': nrm(ks[14], (DEPTH, D, 2 * D_FF), D ** -0.5),
        'ffn_conv': nrm(ks[15], (DEPTH, CONV_W, D_FF), CONV_W ** -0.5),
        'ffn_w_down': nrm(ks[16], (DEPTH, D_FF, D), DN_BETA * D_FF ** -0.5),
        'ln_g': 1.0 + nrm(ks[17], (DEPTH, 3, D), 0.02),
        'ln_b': nrm(ks[18], (DEPTH, 3, D), 0.02),
    }


def reference(x, mem, ab_w_in, hgrn_lb, hgrn_norm_w, ab_w_out, nsa_w_in, nsa_cmp_pos, nsa_cmp_w1,
              nsa_cmp_w2, nsa_w_out, xa_w_q, xa_w_kv, xa_w_o, ffn_w_up, ffn_conv, ffn_w_down, ln_g, ln_b):
    S = x.shape[1]
    pos = jnp.arange(S)
    lb_all = jnp.cumsum(jax.nn.softmax(hgrn_lb.astype(jnp.float32), axis=0), axis=0)
    h = x
    for layer in range(DEPTH):
        if layer % 2 == 0:
            e = layer // 2
            mix = hgrn_stickbreak_mixer(h, ab_w_in[e], lb_all[e], hgrn_norm_w[e], ab_w_out[e])
        else:
            o = layer // 2
            mix = nsa_mixer(h, nsa_w_in[o], nsa_cmp_pos[o], nsa_cmp_w1[o], nsa_cmp_w2[o], nsa_w_out[o], pos)
        h = layer_norm(DN_ALPHA * h + mix, ln_g[layer, 0], ln_b[layer, 0])
        h = layer_norm(DN_ALPHA * h + memory_cross_attention(h, mem, xa_w_q[layer], xa_w_kv[layer], xa_w_o[layer]),
                       ln_g[layer, 1], ln_b[layer, 1])
        h = layer_norm(DN_ALPHA * h + conv_glu_ffn(h, ffn_w_up[layer], ffn_conv[layer], ffn_w_down[layer]),
                       ln_g[layer, 2], ln_b[layer, 2])
    return h
```

```python
import functools

import jax
import jax.numpy as jnp
from jax import lax
from jax.experimental import pallas as pl
from jax.experimental.pallas import tpu as pltpu

F32 = jnp.float32
BF16 = jnp.bfloat16

HEAD_DIM = 128
LANES = 128
HGRN_SUB = 16
NSA_KV_HEADS = 4
CMP_LEN = 32
CMP_STRIDE = 16
SLC_LEN = 64
SLC_TOP = 16
WINDOW = 512
XA_HEADS = 4
CONV_W = 3
ROPE_THETA = 10000.0
LN_EPS = 1e-5
RMS_EPS = 1e-6
NEG_INF = -1e30
FORCE_SCORE = 1e9
EXP_ZERO_BELOW = -104.0
VMEM_LIMIT = 52 * 1024 * 1024

_NT = (((1,), (1,)), ((), ()))
_TN = (((0,), (0,)), ((), ()))


def _cparams(*sem):
    return pltpu.CompilerParams(dimension_semantics=sem, vmem_limit_bytes=VMEM_LIMIT)


def _split_dot(a, b01):
    hi = a.astype(BF16)
    lo = (a - hi.astype(F32)).astype(BF16)
    return (jnp.dot(hi, b01, preferred_element_type=F32)
            + jnp.dot(lo, b01, preferred_element_type=F32))


def _mm_body(x_ref, w_ref, o_ref, *scratch, nk):
    prod = jnp.dot(x_ref[...], w_ref[...], preferred_element_type=F32)
    if nk == 1:
        o_ref[...] = prod.astype(o_ref.dtype)
        return
    acc_ref, = scratch
    k = pl.program_id(2)

    @pl.when(k == 0)
    def _():
        acc_ref[...] = prod

    @pl.when(k > 0)
    def _():
        acc_ref[...] += prod

    @pl.when(k == nk - 1)
    def _():
        o_ref[...] = acc_ref[...].astype(o_ref.dtype)


def _pick(n, pref):
    if n <= pref:
        return n
    t = pref
    while t >= LANES:
        if n % t == 0:
            return t
        t -= LANES
    return n


def matmul(x, w, out_dtype=BF16, tm=1024, tn=512, tk=4096):
    M, K = x.shape
    N = w.shape[1]
    tm, tn, tk = _pick(M, tm), _pick(N, tn), _pick(K, tk)
    nk = K // tk
    scratch = [] if nk == 1 else [pltpu.VMEM((tm, tn), F32)]
    return pl.pallas_call(
        functools.partial(_mm_body, nk=nk),
        grid=(M // tm, N // tn, nk),
        in_specs=[pl.BlockSpec((tm, tk), lambda i, j, k: (i, k)),
                  pl.BlockSpec((tk, tn), lambda i, j, k: (k, j))],
        out_specs=pl.BlockSpec((tm, tn), lambda i, j, k: (i, j)),
        out_shape=jax.ShapeDtypeStruct((M, N), out_dtype),
        scratch_shapes=scratch,
        compiler_params=_cparams("parallel", "parallel", "arbitrary"),
        name="matmul",
    )(x, w)


def _add_ln_body(h_ref, m_ref, g_ref, b_ref, o32_ref, o16_ref, *, alpha):
    y = alpha * h_ref[...] + m_ref[...].astype(F32)
    mu = jnp.mean(y, axis=-1, keepdims=True)
    d = y - mu
    var = jnp.mean(d * d, axis=-1, keepdims=True)
    out = d * lax.rsqrt(var + LN_EPS) * g_ref[...] + b_ref[...]
    o32_ref[...] = out
    o16_ref[...] = out.astype(BF16)


def add_layer_norm(h, mix, g, b, alpha, tm=256):
    M, D = h.shape
    tm = _pick(M, tm)
    row = pl.BlockSpec((tm, D), lambda i: (i, 0))
    vec = pl.BlockSpec((1, D), lambda i: (0, 0))
    return pl.pallas_call(
        functools.partial(_add_ln_body, alpha=alpha),
        grid=(M // tm,),
        in_specs=[row, row, vec, vec],
        out_specs=[row, row],
        out_shape=[jax.ShapeDtypeStruct((M, D), F32), jax.ShapeDtypeStruct((M, D), BF16)],
        compiler_params=_cparams("parallel"),
        name="add_layer_norm",
    )(h, mix, g.reshape(1, D).astype(F32), b.reshape(1, D).astype(F32))


def _hgrn_body(q_ref, f_ref, i_ref, g_ref, lb_ref, nw_ref, o_ref, st_ref, *, ts, layer_idx):
    C = HGRN_SUB
    nsub = ts // C

    @pl.when(pl.program_id(2) == 0)
    def _():
        st_ref[...] = jnp.zeros_like(st_ref)

    lbr = lb_ref[...]
    ex = jnp.exp(lbr - jnp.max(lbr, axis=0, keepdims=True))
    sm = ex / jnp.sum(ex, axis=0, keepdims=True)
    lb = jnp.sum(sm[:layer_idx + 1], axis=0, keepdims=True)

    q = q_ref[0].astype(F32)
    z = f_ref[0].astype(F32)
    v = i_ref[0]
    e = jnp.exp(-jnp.abs(z))
    r = 1.0 / (1.0 + e)
    pos = z >= 0
    sig = jnp.where(pos, r, e * r)
    nsig = jnp.where(pos, e * r, r)
    logf = jnp.log(lb + (1.0 - lb) * sig)
    k = (1.0 - lb) * nsig

    row = lax.broadcasted_iota(jnp.int32, (ts, ts), 0)
    col = lax.broadcasted_iota(jnp.int32, (ts, ts), 1)
    same = (row // C) == (col // C)
    cum01 = jnp.where(same & (row >= col), 1.0, 0.0).astype(BF16)
    tot01 = jnp.where(same, 1.0, 0.0).astype(BF16)
    hi = logf.astype(BF16)
    lo = (logf - hi.astype(F32)).astype(BF16)
    b = jnp.dot(cum01, hi, preferred_element_type=F32) + jnp.dot(cum01, lo, preferred_element_type=F32)
    bl = jnp.dot(tot01, hi, preferred_element_type=F32) + jnp.dot(tot01, lo, preferred_element_type=F32)

    qd = (q * jnp.exp(b)).astype(BF16)
    kd = (k * jnp.exp(bl - b)).astype(BF16)
    dec = jnp.exp(bl)

    tri = (lax.broadcasted_iota(jnp.int32, (C, C, HEAD_DIM), 0)
           >= lax.broadcasted_iota(jnp.int32, (C, C, HEAD_DIM), 1))
    st = st_ref[...]
    outs = []
    for n in range(nsub):
        sl = slice(n * C, (n + 1) * C)
        bn, qn, kn = b[sl], q[sl], k[sl]
        diff = bn[:, None, :] - bn[None, :, :]
        ee = jnp.exp(jnp.where(tri, diff, NEG_INF))
        dmat = jnp.sum(qn[:, None, :] * (kn[None, :, :] * ee), axis=-1)
        o_n = jnp.dot(dmat.astype(BF16), v[sl], preferred_element_type=F32)
        o_n = o_n + lax.dot_general(qd[sl], st.astype(BF16), _NT, preferred_element_type=F32)
        upd = lax.dot_general(v[sl], kd[sl], _TN, preferred_element_type=F32)
        st = st * dec[n * C:n * C + 1] + upd
        outs.append(o_n)
    st_ref[...] = st
    o = jnp.concatenate(outs, axis=0)

    gt = g_ref[0].astype(F32)
    gate = gt / (1.0 + jnp.exp(-gt))
    o = o * lax.rsqrt(jnp.mean(o * o, axis=-1, keepdims=True) + RMS_EPS) * nw_ref[...] * gate
    o_ref[0] = o.astype(o_ref.dtype)


def hgrn2(proj, lb_raw, norm_w, n_heads, layer_idx, ts=128):
    B, S, _ = proj.shape
    ts = _pick(S, ts)
    H = n_heads
    L = lb_raw.shape[0]

    def col(off):
        return pl.BlockSpec((1, ts, HEAD_DIM), lambda b, h, s: (b, s, off * H + h))

    return pl.pallas_call(
        functools.partial(_hgrn_body, ts=ts, layer_idx=layer_idx),
        grid=(B, H, S // ts),
        in_specs=[col(0), col(1), col(2), col(3),
                  pl.BlockSpec((L, HEAD_DIM), lambda b, h, s: (0, h)),
                  pl.BlockSpec((1, HEAD_DIM), lambda b, h, s: (0, 0))],
        out_specs=pl.BlockSpec((1, ts, HEAD_DIM), lambda b, h, s: (b, s, h)),
        out_shape=jax.ShapeDtypeStruct((B, S, H * HEAD_DIM), BF16),
        scratch_shapes=[pltpu.VMEM((HEAD_DIM, HEAD_DIM), F32)],
        compiler_params=_cparams("parallel", "parallel", "arbitrary"),
        name="hgrn2",
    )(proj, proj, proj, proj, lb_raw.astype(F32), norm_w.reshape(1, HEAD_DIM).astype(F32))


def _sb_block(q, k, v, carry, after01, scale, mask):
    z = lax.dot_general(q, k, _NT, preferred_element_type=F32) * scale
    sp = jnp.maximum(z, 0.0) + jnp.log(1.0 + jnp.exp(-jnp.abs(z)))
    lm = -sp
    if mask is not None:
        lm = jnp.where(mask, lm, 0.0)
    rev = _split_dot(lm, after01)
    w = jnp.exp(z - sp + rev + carry)
    if mask is not None:
        w = jnp.where(mask, w, 0.0)
    contrib = jnp.dot(w.astype(BF16), v, preferred_element_type=F32)
    return contrib, carry + rev[:, 0:1] + lm[:, 0:1]


def _sb_body(q_ref, k_ref, v_ref, o_ref, *, tq, scale):
    qi = pl.program_id(2)
    q = q_ref[0]
    row = lax.broadcasted_iota(jnp.int32, (tq, tq), 0)
    col = lax.broadcasted_iota(jnp.int32, (tq, tq), 1)
    after01 = jnp.where(row > col, 1.0, 0.0).astype(BF16)
    q0 = pl.multiple_of(qi * tq, tq)
    acc, carry = _sb_block(q, k_ref[0, pl.ds(q0, tq), :], v_ref[0, pl.ds(q0, tq), :],
                           jnp.zeros((tq, 1), F32), after01, scale, col < row)

    def cond(c):
        j, _, carry = c
        return jnp.logical_and(j >= 0, jnp.max(carry) > EXP_ZERO_BELOW)

    def body(c):
        j, acc, carry = c
        k0 = pl.multiple_of(j * tq, tq)
        contrib, carry = _sb_block(q, k_ref[0, pl.ds(k0, tq), :], v_ref[0, pl.ds(k0, tq), :],
                                   carry, after01, scale, None)
        return j - 1, acc + contrib, carry

    _, acc, _ = lax.while_loop(cond, body, (qi - 1, acc, carry))
    o_ref[0] = acc.astype(o_ref.dtype)


def stick_breaking(proj, col_q, col_k, col_v, n_heads, tq=256):
    B, S, _ = proj.shape
    tq = _pick(S, tq)
    full = lambda off: pl.BlockSpec((1, S, HEAD_DIM), lambda b, h, i: (b, 0, off + h))
    return pl.pallas_call(
        functools.partial(_sb_body, tq=tq, scale=HEAD_DIM ** -0.5),
        grid=(B, n_heads, S // tq),
        in_specs=[pl.BlockSpec((1, tq, HEAD_DIM), lambda b, h, i: (b, i, col_q + h)),
                  full(col_k), full(col_v)],
        out_specs=pl.BlockSpec((1, tq, HEAD_DIM), lambda b, h, i: (b, i, h)),
        out_shape=jax.ShapeDtypeStruct((B, S, n_heads * HEAD_DIM), BF16),
        compiler_params=_cparams("parallel", "parallel", "arbitrary"),
        name="stick_breaking",
    )(proj, proj, proj)


def _rope_body(x_ref, cos_ref, sin_ref, o_ref, *, n_heads):
    cos = cos_ref[...]
    sin = sin_ref[...]
    for h in range(n_heads):
        sl = slice(h * HEAD_DIM, (h + 1) * HEAD_DIM)
        t = x_ref[0, :, sl].astype(F32)
        o_ref[0, :, sl] = (t * cos + pltpu.roll(t, HEAD_DIM // 2, 1) * sin).astype(o_ref.dtype)


def rope(x, col0, n_heads, cos, sin, ts=256):
    B, S, _ = x.shape
    ts = _pick(S, ts)
    w = n_heads * HEAD_DIM
    cb = col0 // w
    tab = pl.BlockSpec((ts, HEAD_DIM), lambda b, s: (s, 0))
    return pl.pallas_call(
        functools.partial(_rope_body, n_heads=n_heads),
        grid=(B, S // ts),
        in_specs=[pl.BlockSpec((1, ts, w), lambda b, s: (b, s, cb)), tab, tab],
        out_specs=pl.BlockSpec((1, ts, w), lambda b, s: (b, s, 0)),
        out_shape=jax.ShapeDtypeStruct((B, S, w), BF16),
        compiler_params=_cparams("parallel", "parallel"),
        name="rope",
    )(x, cos, sin)


def _gelu(x):
    return 0.5 * x * (1.0 + lax.erf(x * (2.0 ** -0.5)))


def _compress_body(x_ref, pe_ref, w1_ref, w2_ref, o_ref):
    x = x_ref[0, 0]
    half = x.shape[1]
    n16 = x.shape[0]
    y1 = jnp.dot(x, w1_ref[0, :half], preferred_element_type=F32)
    y2 = jnp.dot(x, w1_ref[0, half:], preferred_element_type=F32)
    bias = jnp.dot(pe_ref[0], w1_ref[0], preferred_element_type=F32)[0:1]
    hid = _gelu(y1 + pltpu.roll(y2, n16 - 1, 0) + bias)
    o_ref[0, 0] = jnp.dot(hid.astype(BF16), w2_ref[0], preferred_element_type=F32).astype(o_ref.dtype)


def compress(x16, pe, w1, w2):
    two, B, G, n16, wide = x16.shape
    x16 = x16.reshape(two * B, G, n16, wide)
    out = pl.pallas_call(
        _compress_body,
        grid=(two * B, G),
        in_specs=[pl.BlockSpec((1, 1, n16, wide), lambda i, g: (i, g, 0, 0)),
                  pl.BlockSpec((1, 8, 2 * wide), lambda i, g: (i // B, 0, 0)),
                  pl.BlockSpec((1, 2 * wide, HEAD_DIM), lambda i, g: (i // B, 0, 0)),
                  pl.BlockSpec((1, HEAD_DIM, HEAD_DIM), lambda i, g: (i // B, 0, 0))],
        out_specs=pl.BlockSpec((1, 1, n16, HEAD_DIM), lambda i, g: (i, g, 0, 0)),
        out_shape=jax.ShapeDtypeStruct((two * B, G, n16, HEAD_DIM), BF16),
        compiler_params=_cparams("parallel", "parallel"),
        name="nsa_compress",
    )(x16, pe, w1, w2)
    return out.reshape(two, B, G, n16, HEAD_DIM)


def _cmp_body(q_ref, kc_ref, vc_ref, o_ref, sel_ref, *, tq, rep, n_slc, scale):
    q0 = pl.program_id(2) * tq
    kc = kc_ref[0, 0]
    vc = vc_ref[0, 0]
    n16 = kc.shape[0]
    t = q0 + lax.broadcasted_iota(jnp.int32, (tq, n16), 0)
    n = lax.broadcasted_iota(jnp.int32, (tq, n16), 1)
    valid = n * CMP_STRIDE + (CMP_LEN - 1) <= t
    psum = jnp.zeros((tq, n16), F32)
    for r in range(rep):
        sl = slice(r * HEAD_DIM, (r + 1) * HEAD_DIM)
        s = lax.dot_general(q_ref[0, :, sl], kc, _NT, preferred_element_type=F32) * scale
        s = jnp.where(valid, s, NEG_INF)
        p = jnp.where(valid, jnp.exp(s - jnp.max(s, axis=-1, keepdims=True)), 0.0)
        den = jnp.sum(p, axis=-1, keepdims=True)
        p = p / jnp.where(den > 0.0, den, 1.0)
        o_ref[0, :, sl] = jnp.dot(p.astype(BF16), vc, preferred_element_type=F32).astype(o_ref.dtype)
        psum = psum + p

    cn = lax.broadcasted_iota(jnp.int32, (n16, LANES), 0) * CMP_STRIDE
    cj = lax.broadcasted_iota(jnp.int32, (n16, LANES), 1) * SLC_LEN
    ov01 = jnp.where((cn < cj + SLC_LEN) & (cn + CMP_LEN > cj), 1.0, 0.0).astype(BF16)
    imp = _split_dot(psum, ov01)

    tt = q0 + lax.broadcasted_iota(jnp.int32, (tq, LANES), 0)
    j = lax.broadcasted_iota(jnp.int32, (tq, LANES), 1)
    cur = tt // SLC_LEN
    forced = (j == 0) | (j == cur) | (j == cur - 1)
    allowed = j * SLC_LEN <= tt
    score = jnp.where(forced, FORCE_SCORE, jnp.where(allowed, imp, -1.0))
    score = jnp.where(j < n_slc, score, -jnp.inf)
    jf = j.astype(F32)
    sel = jnp.zeros((tq, LANES), F32)
    for _ in range(min(SLC_TOP, n_slc)):
        m = jnp.max(score, axis=-1, keepdims=True)
        first = jnp.min(jnp.where(score == m, jf, float(LANES)), axis=-1, keepdims=True)
        pick = jf == first
        sel = jnp.where(pick, 1.0, sel)
        score = jnp.where(pick, -jnp.inf, score)
    sel_ref[0, 0] = sel.astype(sel_ref.dtype)


def cmp_attention(q, kc, vc, rep, tq=128):
    B, S, _ = q.shape
    G, n16 = kc.shape[1], kc.shape[2]
    n_slc = S // SLC_LEN
    assert n_slc <= LANES
    tq = _pick(S, tq)
    w = rep * HEAD_DIM
    kv = pl.BlockSpec((1, 1, n16, HEAD_DIM), lambda b, g, i: (b, g, 0, 0))
    return pl.pallas_call(
        functools.partial(_cmp_body, tq=tq, rep=rep, n_slc=n_slc, scale=HEAD_DIM ** -0.5),
        grid=(B, G, S // tq),
        in_specs=[pl.BlockSpec((1, tq, w), lambda b, g, i: (b, i, g)), kv, kv],
        out_specs=[pl.BlockSpec((1, tq, w), lambda b, g, i: (b, i, g)),
                   pl.BlockSpec((1, 1, tq, LANES), lambda b, g, i: (b, g, i, 0))],
        out_shape=[jax.ShapeDtypeStruct((B, S, G * w), BF16),
                   jax.ShapeDtypeStruct((B, G, S, LANES), BF16)],
        compiler_params=_cparams("parallel", "parallel", "parallel"),
        name="nsa_cmp_topk",
    )(q, kc, vc)


def _gqa_body(*refs, tq, tk, rep, scale, mode):
    if mode == "sel":
        q_ref, k_ref, v_ref, sel_ref, exp_ref, o_ref, mask_ref = refs
    else:
        q_ref, k_ref, v_ref, o_ref = refs
    q0 = pl.program_id(2) * tq
    rows = rep * tq
    q2 = jnp.concatenate([q_ref[0, :, r * HEAD_DIM:(r + 1) * HEAD_DIM] for r in range(rep)], axis=0)

    if mode == "sel":
        mask_ref[...] = jnp.dot(sel_ref[0, 0], exp_ref[...], preferred_element_type=F32)
        lo = 0
    else:
        lo = jnp.maximum(q0 - (WINDOW - 1), 0) // tk
    hi = (q0 + tq - 1) // tk + 1

    t = q0 + lax.broadcasted_iota(jnp.int32, (tq, tk), 0)
    c = lax.broadcasted_iota(jnp.int32, (tq, tk), 1)

    def body(kj, carry):
        m, l, acc = carry
        k0 = pl.multiple_of(kj * tk, tk)
        kp = k0 + c
        ok = kp <= t
        if mode == "sel":
            ok = ok & (mask_ref[:, pl.ds(k0, tk)] > 0.5)
        else:
            ok = ok & (kp > t - WINDOW)
        s = lax.dot_general(q2, k_ref[0, pl.ds(k0, tk), :], _NT, preferred_element_type=F32) * scale
        ok3 = ok[None]
        s3 = jnp.where(ok3, s.reshape(rep, tq, tk), NEG_INF)
        m_new = jnp.maximum(m, jnp.max(s3, axis=-1, keepdims=True))
        p3 = jnp.where(ok3, jnp.exp(s3 - m_new), 0.0)
        a = jnp.exp(m - m_new)
        l = a * l + jnp.sum(p3, axis=-1, keepdims=True)
        pv = jnp.dot(p3.reshape(rows, tk).astype(BF16), v_ref[0, pl.ds(k0, tk), :],
                     preferred_element_type=F32)
        acc = a * acc + pv.reshape(rep, tq, HEAD_DIM)
        return m_new, l, acc

    init = (jnp.full((rep, tq, 1), NEG_INF, F32), jnp.zeros((rep, tq, 1), F32),
            jnp.zeros((rep, tq, HEAD_DIM), F32))
    _, l, acc = lax.fori_loop(lo, hi, body, init)
    out = acc / l
    for r in range(rep):
        o_ref[0, :, r * HEAD_DIM:(r + 1) * HEAD_DIM] = out[r].astype(o_ref.dtype)


def gqa_attention(q, k, v, rep, mode, sel=None, tq=128, tk=256):
    B, S, _ = q.shape
    G = k.shape[2] // HEAD_DIM
    tq, tk = _pick(S, tq), _pick(S, tk)
    w = rep * HEAD_DIM
    kv = pl.BlockSpec((1, S, HEAD_DIM), lambda b, g, i: (b, 0, g))
    in_specs = [pl.BlockSpec((1, tq, w), lambda b, g, i: (b, i, g)), kv, kv]
    args = [q, k, v]
    scratch = []
    if mode == "sel":
        blk = lax.broadcasted_iota(jnp.int32, (LANES, S), 0)
        key = lax.broadcasted_iota(jnp.int32, (LANES, S), 1)
        expand01 = (key // SLC_LEN == blk).astype(BF16)
        in_specs += [pl.BlockSpec((1, 1, tq, LANES), lambda b, g, i: (b, g, i, 0)),
                     pl.BlockSpec((LANES, S), lambda b, g, i: (0, 0))]
        args += [sel, expand01]
        scratch = [pltpu.VMEM((tq, S), F32)]
    return pl.pallas_call(
        functools.partial(_gqa_body, tq=tq, tk=tk, rep=rep, scale=HEAD_DIM ** -0.5, mode=mode),
        grid=(B, G, S // tq),
        in_specs=in_specs,
        out_specs=pl.BlockSpec((1, tq, w), lambda b, g, i: (b, i, g)),
        out_shape=jax.ShapeDtypeStruct((B, S, G * w), BF16),
        scratch_shapes=scratch,
        compiler_params=_cparams("parallel", "parallel", "arbitrary"),
        name="nsa_" + mode,
    )(*args)


def _nsa_gate_body(gl_ref, oc_ref, os_ref, ow_ref, o_ref, *, n_heads):
    gl = gl_ref[...].astype(F32)
    gate = 1.0 / (1.0 + jnp.exp(-gl))
    ng = 3 * n_heads
    src = lax.broadcasted_iota(jnp.int32, (ng, n_heads * HEAD_DIM), 0)
    head = lax.broadcasted_iota(jnp.int32, (ng, n_heads * HEAD_DIM), 1) // HEAD_DIM
    out = None
    for c, ref in enumerate((oc_ref, os_ref, ow_ref)):
        spread01 = jnp.where(src == head * 3 + c, 1.0, 0.0).astype(BF16)
        term = _split_dot(gate, spread01) * ref[...].astype(F32)
        out = term if out is None else out + term
    o_ref[...] = out.astype(o_ref.dtype)


def nsa_gate(gl, oc, os_, ow, n_heads, tm=256):
    M, W = oc.shape
    tm = _pick(M, tm)
    row = pl.BlockSpec((tm, W), lambda i: (i, 0))
    return pl.pallas_call(
        functools.partial(_nsa_gate_body, n_heads=n_heads),
        grid=(M // tm,),
        in_specs=[pl.BlockSpec((tm, gl.shape[1]), lambda i: (i, 0)), row, row, row],
        out_specs=row,
        out_shape=jax.ShapeDtypeStruct((M, W), BF16),
        compiler_params=_cparams("parallel"),
        name="nsa_gate",
    )(gl, oc, os_, ow)


def _xattn_body(q_ref, kv_ref, o_ref, *, n_heads, scale):
    w = n_heads * HEAD_DIM
    for h in range(n_heads):
        sl = slice(h * HEAD_DIM, (h + 1) * HEAD_DIM)
        k = kv_ref[0, :, sl]
        v = kv_ref[0, :, w + h * HEAD_DIM:w + (h + 1) * HEAD_DIM]
        s = lax.dot_general(q_ref[0, :, sl], k, _NT, preferred_element_type=F32) * scale
        p = jnp.exp(s - jnp.max(s, axis=-1, keepdims=True))
        p = p / jnp.sum(p, axis=-1, keepdims=True)
        o_ref[0, :, sl] = jnp.dot(p.astype(BF16), v, preferred_element_type=F32).astype(o_ref.dtype)


def mem_attention(q, kv, n_heads, tq=512):
    B, S, w = q.shape
    n_mem = kv.shape[1]
    tq = _pick(S, tq)
    return pl.pallas_call(
        functools.partial(_xattn_body, n_heads=n_heads, scale=HEAD_DIM ** -0.5),
        grid=(B, S // tq),
        in_specs=[pl.BlockSpec((1, tq, w), lambda b, i: (b, i, 0)),
                  pl.BlockSpec((1, n_mem, 2 * w), lambda b, i: (b, 0, 0))],
        out_specs=pl.BlockSpec((1, tq, w), lambda b, i: (b, i, 0)),
        out_shape=jax.ShapeDtypeStruct((B, S, w), BF16),
        compiler_params=_cparams("parallel", "parallel"),
        name="mem_attention",
    )(q, kv)


def _convglu_body(a_ref, prev_ref, u_ref, cw_ref, o_ref, *, tiles_per_seq):
    a = a_ref[...].astype(F32)
    tm = a.shape[0]
    first = (pl.program_id(0) % tiles_per_seq) == 0
    prev = jnp.where(first, 0.0, prev_ref[...].astype(F32))
    rowi = lax.broadcasted_iota(jnp.int32, a.shape, 0)
    a1 = jnp.where(rowi >= 1, pltpu.roll(a, 1, 0), prev[7:8])
    a2 = jnp.where(rowi >= 2, pltpu.roll(a, 2, 0), jnp.where(rowi == 1, prev[7:8], prev[6:7]))
    cw = cw_ref[...]
    c = cw[2:3] * a + cw[1:2] * a1 + cw[0:1] * a2
    o_ref[...] = (_gelu(c) * u_ref[...].astype(F32)).astype(o_ref.dtype)


def conv_glu(au, conv_w, seq_len, tm=512, tn=512):
    M, F2 = au.shape
    Fd = F2 // 2
    tm, tn = _pick(seq_len, tm), _pick(Fd, tn)
    nj = Fd // tn
    r8 = tm // 8
    return pl.pallas_call(
        functools.partial(_convglu_body, tiles_per_seq=seq_len // tm),
        grid=(M // tm, nj),
        in_specs=[pl.BlockSpec((tm, tn), lambda i, j: (i, j)),
                  pl.BlockSpec((8, tn), lambda i, j: (jnp.maximum(i * r8 - 1, 0), j)),
                  pl.BlockSpec((tm, tn), lambda i, j: (i, j + nj)),
                  pl.BlockSpec((CONV_W, tn), lambda i, j: (0, j))],
        out_specs=pl.BlockSpec((tm, tn), lambda i, j: (i, j)),
        out_shape=jax.ShapeDtypeStruct((M, Fd), BF16),
        compiler_params=_cparams("parallel", "parallel"),
        name="conv_glu",
    )(au, au, au, conv_w.astype(F32))


def _rope_tables(S):
    half = HEAD_DIM // 2
    inv_freq = ROPE_THETA ** (-jnp.arange(half, dtype=F32) / half)
    ang = jnp.arange(S, dtype=F32)[:, None] * inv_freq[None, :]
    cos, sin = jnp.cos(ang), jnp.sin(ang)
    return jnp.concatenate([cos, cos], axis=-1), jnp.concatenate([-sin, sin], axis=-1)


def _hgrn_sb_mixer(hb, B, S, w_in, lb_raw, norm_w, w_out, e):
    width = w_out.shape[0]
    a_heads = width // (2 * HEAD_DIM)
    b_heads = a_heads
    proj = matmul(hb, w_in.astype(BF16)).reshape(B, S, -1)
    o_a = hgrn2(proj, lb_raw, norm_w, a_heads, e)
    o_b = stick_breaking(proj, 4 * a_heads, 4 * a_heads + b_heads, 4 * a_heads + 2 * b_heads, b_heads)
    o = jnp.concatenate([o_a, o_b], axis=-1).reshape(B * S, width)
    return matmul(o, w_out.astype(BF16))


def _nsa_mixer(hb, B, S, w_in, cmp_pos, cmp_w1, cmp_w2, w_out, cos, sin):
    G = NSA_KV_HEADS
    q_w = w_out.shape[0]
    n_heads = q_w // HEAD_DIM
    rep = n_heads // G
    kv_w = G * HEAD_DIM
    main_w = q_w + 6 * kv_w
    w_in = w_in.astype(BF16)
    proj = matmul(hb, w_in[:, :main_w]).reshape(B, S, main_w)
    gl = matmul(hb, w_in[:, main_w:])

    q_rot = rope(proj, 0, n_heads, cos, sin)
    kvs = proj[:, :, q_w:].reshape(B, S, 6, G, HEAD_DIM)
    ks_rot = rope(proj[:, :, q_w + 2 * kv_w:q_w + 3 * kv_w], 0, G, cos, sin)
    kw_rot = rope(proj[:, :, q_w + 4 * kv_w:q_w + 5 * kv_w], 0, G, cos, sin)
    vs = proj[:, :, q_w + 3 * kv_w:q_w + 4 * kv_w]
    vw = proj[:, :, q_w + 5 * kv_w:q_w + 6 * kv_w]

    n16 = S // CMP_STRIDE
    x16 = kvs[:, :, 0:2].transpose(2, 0, 3, 1, 4).reshape(2, B, G, n16, CMP_STRIDE * HEAD_DIM)
    pe = jnp.broadcast_to(cmp_pos.reshape(2, 1, CMP_LEN * HEAD_DIM), (2, 8, CMP_LEN * HEAD_DIM)).astype(BF16)
    w1 = cmp_w1.reshape(2, CMP_LEN * HEAD_DIM, HEAD_DIM).astype(BF16)
    kvc = compress(x16, pe, w1, cmp_w2.astype(BF16))

    o_c, sel = cmp_attention(proj, kvc[0], kvc[1], rep)
    o_s = gqa_attention(q_rot, ks_rot, vs, rep, "sel", sel=sel)
    o_w = gqa_attention(q_rot, kw_rot, vw, rep, "win")
    o = nsa_gate(gl, o_c.reshape(B * S, q_w), o_s.reshape(B * S, q_w), o_w.reshape(B * S, q_w), n_heads)
    return matmul(o, w_out.astype(BF16))


def kernel(x, mem, ab_w_in, hgrn_lb, hgrn_norm_w, ab_w_out, nsa_w_in, nsa_cmp_pos, nsa_cmp_w1,
           nsa_cmp_w2, nsa_w_out, xa_w_q, xa_w_kv, xa_w_o, ffn_w_up, ffn_conv, ffn_w_down, ln_g, ln_b):
    B, S, D = x.shape
    depth = ln_g.shape[0]
    alpha = (2 * depth) ** 0.25
    n_mem = mem.shape[1]
    cos, sin = _rope_tables(S)
    h = x.reshape(B * S, D).astype(F32)
    hb = h.astype(BF16)
    memb = mem.reshape(B * n_mem, D).astype(BF16)
    for layer in range(depth):
        if layer % 2 == 0:
            e = layer // 2
            mix = _hgrn_sb_mixer(hb, B, S, ab_w_in[e], hgrn_lb, hgrn_norm_w[e], ab_w_out[e], e)
        else:
            o = layer // 2
            mix = _nsa_mixer(hb, B, S, nsa_w_in[o], nsa_cmp_pos[o], nsa_cmp_w1[o], nsa_cmp_w2[o],
                             nsa_w_out[o], cos, sin)
        h, hb = add_layer_norm(h, mix, ln_g[layer, 0], ln_b[layer, 0], alpha)

        xq = matmul(hb, xa_w_q[layer].astype(BF16)).reshape(B, S, -1)
        xkv = matmul(memb, xa_w_kv[layer].astype(BF16)).reshape(B, n_mem, -1)
        xo = mem_attention(xq, xkv, XA_HEADS).reshape(B * S, -1)
        h, hb = add_layer_norm(h, matmul(xo, xa_w_o[layer].astype(BF16)), ln_g[layer, 1], ln_b[layer, 1], alpha)

        au = matmul(hb, ffn_w_up[layer].astype(BF16))
        gated = conv_glu(au, ffn_conv[layer], S)
        h, hb = add_layer_norm(h, matmul(gated, ffn_w_down[layer].astype(BF16), tk=5504),
                               ln_g[layer, 2], ln_b[layer, 2], alpha)
    return h.reshape(B, S, D).astype(x.dtype)
```

```python
import functools

import jax
import jax.numpy as jnp
from jax import lax
from jax.experimental import pallas as pl
from jax.experimental.pallas import tpu as pltpu

F32 = jnp.float32
BF16 = jnp.bfloat16

HEAD_DIM = 128
LANES = 128
HGRN_SUB = 16
HGRN_SAFE_LOG_DECAY = -60.0
NSA_KV_HEADS = 4
CMP_LEN = 32
CMP_STRIDE = 16
SLC_LEN = 64
SLC_TOP = 16
WINDOW = 512
XA_HEADS = 4
CONV_W = 3
ROPE_THETA = 10000.0
LN_EPS = 1e-5
RMS_EPS = 1e-6
NEG_INF = -1e30
FORCE_SCORE = 1e9
EXP_ZERO_BELOW = -104.0
LOG2E = 1.4426950408889634
VMEM_LIMIT = 52 * 1024 * 1024

_NT = (((1,), (1,)), ((), ()))
_TN = (((0,), (0,)), ((), ()))


def _cparams(*sem):
    return pltpu.CompilerParams(dimension_semantics=sem, vmem_limit_bytes=VMEM_LIMIT)


def _split_dot(a, b01):
    hi = a.astype(BF16)
    lo = (a - hi.astype(F32)).astype(BF16)
    return (jnp.dot(hi, b01, preferred_element_type=F32)
            + jnp.dot(lo, b01, preferred_element_type=F32))


def _mm_body(x_ref, w_ref, o_ref, *scratch, nk):
    prod = jnp.dot(x_ref[...], w_ref[...], preferred_element_type=F32)
    if nk == 1:
        o_ref[...] = prod.astype(o_ref.dtype)
        return
    acc_ref, = scratch
    k = pl.program_id(2)

    @pl.when(k == 0)
    def _():
        acc_ref[...] = prod

    @pl.when(k > 0)
    def _():
        acc_ref[...] += prod

    @pl.when(k == nk - 1)
    def _():
        o_ref[...] = acc_ref[...].astype(o_ref.dtype)


def _pick(n, pref):
    if n <= pref:
        return n
    t = pref
    while t >= LANES:
        if n % t == 0:
            return t
        t -= LANES
    return n


def matmul(x, w, out_dtype=BF16, tm=1024, tn=512, tk=4096):
    M, K = x.shape
    N = w.shape[1]
    tm, tn, tk = _pick(M, tm), _pick(N, tn), _pick(K, tk)
    nk = K // tk
    scratch = [] if nk == 1 else [pltpu.VMEM((tm, tn), F32)]
    return pl.pallas_call(
        functools.partial(_mm_body, nk=nk),
        grid=(M // tm, N // tn, nk),
        in_specs=[pl.BlockSpec((tm, tk), lambda i, j, k: (i, k)),
                  pl.BlockSpec((tk, tn), lambda i, j, k: (k, j))],
        out_specs=pl.BlockSpec((tm, tn), lambda i, j, k: (i, j)),
        out_shape=jax.ShapeDtypeStruct((M, N), out_dtype),
        scratch_shapes=scratch,
        compiler_params=_cparams("parallel", "parallel", "arbitrary"),
        name="matmul",
    )(x, w)


def _add_ln_body(h_ref, m_ref, g_ref, b_ref, o32_ref, o16_ref, *, alpha):
    y = alpha * h_ref[...] + m_ref[...].astype(F32)
    mu = jnp.mean(y, axis=-1, keepdims=True)
    d = y - mu
    var = jnp.mean(d * d, axis=-1, keepdims=True)
    out = d * lax.rsqrt(var + LN_EPS) * g_ref[...] + b_ref[...]
    o32_ref[...] = out
    o16_ref[...] = out.astype(BF16)


def add_layer_norm(h, mix, g, b, alpha, tm=256):
    M, D = h.shape
    tm = _pick(M, tm)
    row = pl.BlockSpec((tm, D), lambda i: (i, 0))
    vec = pl.BlockSpec((1, D), lambda i: (0, 0))
    return pl.pallas_call(
        functools.partial(_add_ln_body, alpha=alpha),
        grid=(M // tm,),
        in_specs=[row, row, vec, vec],
        out_specs=[row, row],
        out_shape=[jax.ShapeDtypeStruct((M, D), F32), jax.ShapeDtypeStruct((M, D), BF16)],
        compiler_params=_cparams("parallel"),
        name="add_layer_norm",
    )(h, mix, g.reshape(1, D).astype(F32), b.reshape(1, D).astype(F32))


def _hgrn_body(q_ref, f_ref, i_ref, g_ref, lb_ref, nw_ref, o_ref, st_ref, *, ts, layer_idx, nh):
    C = HGRN_SUB
    nsub = ts // C
    heads = [slice(h * HEAD_DIM, (h + 1) * HEAD_DIM) for h in range(nh)]

    @pl.when(pl.program_id(2) == 0)
    def _():
        st_ref[...] = jnp.zeros_like(st_ref)

    lbr = lb_ref[...]
    ex = jnp.exp(lbr - jnp.max(lbr, axis=0, keepdims=True))
    sm = ex / jnp.sum(ex, axis=0, keepdims=True)
    lb = jnp.sum(sm[:layer_idx + 1], axis=0, keepdims=True)

    q = q_ref[0].astype(F32)
    z = f_ref[0].astype(F32)
    v = i_ref[0]
    e = jnp.exp(-jnp.abs(z))
    r = 1.0 / (1.0 + e)
    pos = z >= 0
    sig = jnp.where(pos, r, e * r)
    nsig = jnp.where(pos, e * r, r)
    logf = jnp.log(lb + (1.0 - lb) * sig)
    k = (1.0 - lb) * nsig

    hi = logf.astype(BF16)
    lo = (logf - hi.astype(F32)).astype(BF16)
    row = lax.broadcasted_iota(jnp.int32, (ts, ts), 0)
    col = lax.broadcasted_iota(jnp.int32, (ts, ts), 1)

    def cumdot(m01):
        return jnp.dot(m01, hi, preferred_element_type=F32) + jnp.dot(m01, lo, preferred_element_type=F32)

    def whole_tile(sts):
        b = cumdot(jnp.where(row >= col, 1.0, 0.0).astype(BF16))
        bl = b[ts - 1:ts]
        qd = (q * jnp.exp(b)).astype(BF16)
        kinv = (k * jnp.exp(-b)).astype(BF16)
        kd = (k * jnp.exp(bl - b)).astype(BF16)
        dec = jnp.exp(bl)
        outs, new = [], []
        for h, sl in enumerate(heads):
            dmat = lax.dot_general(qd[:, sl], kinv[:, sl], _NT, preferred_element_type=F32)
            dmat = jnp.where(row >= col, dmat, 0.0)
            o = jnp.dot(dmat.astype(BF16), v[:, sl], preferred_element_type=F32)
            outs.append(o + lax.dot_general(qd[:, sl], sts[h].astype(BF16), _NT, preferred_element_type=F32))
            new.append(sts[h] * dec[:, sl] + lax.dot_general(v[:, sl], kd[:, sl], _TN, preferred_element_type=F32))
        return tuple(outs), tuple(new)

    def sub_chunks(sts):
        same = (row // C) == (col // C)
        b = cumdot(jnp.where(same & (row >= col), 1.0, 0.0).astype(BF16))
        bl = cumdot(jnp.where(same, 1.0, 0.0).astype(BF16))
        qd = (q * jnp.exp(b)).astype(BF16)
        kd = (k * jnp.exp(bl - b)).astype(BF16)
        dec = jnp.exp(bl)
        tri = (lax.broadcasted_iota(jnp.int32, (C, C, HEAD_DIM), 0)
               >= lax.broadcasted_iota(jnp.int32, (C, C, HEAD_DIM), 1))
        outs, new = [], []
        for h, hs in enumerate(heads):
            st = sts[h]
            parts = []
            for n in range(nsub):
                sl = slice(n * C, (n + 1) * C)
                bn, qn, kn = b[sl, hs], q[sl, hs], k[sl, hs]
                diff = bn[:, None, :] - bn[None, :, :]
                ee = jnp.exp(jnp.where(tri, diff, NEG_INF))
                dmat = jnp.sum(qn[:, None, :] * (kn[None, :, :] * ee), axis=-1)
                o_n = jnp.dot(dmat.astype(BF16), v[sl, hs], preferred_element_type=F32)
                o_n = o_n + lax.dot_general(qd[sl, hs], st.astype(BF16), _NT, preferred_element_type=F32)
                upd = lax.dot_general(v[sl, hs], kd[sl, hs], _TN, preferred_element_type=F32)
                st = st * dec[n * C:n * C + 1, hs] + upd
                parts.append(o_n)
            outs.append(jnp.concatenate(parts, axis=0))
            new.append(st)
        return tuple(outs), tuple(new)

    tile_decay = jnp.min(jnp.sum(logf, axis=0, keepdims=True))
    outs, sts = lax.cond(tile_decay > HGRN_SAFE_LOG_DECAY, whole_tile, sub_chunks,
                         tuple(st_ref[h] for h in range(nh)))
    gt = g_ref[0].astype(F32)
    gate = gt / (1.0 + jnp.exp(-gt))
    for h, sl in enumerate(heads):
        st_ref[h] = sts[h]
        o = outs[h]
        o = o * lax.rsqrt(jnp.mean(o * o, axis=-1, keepdims=True) + RMS_EPS) * nw_ref[...] * gate[:, sl]
        o_ref[0, :, sl] = o.astype(o_ref.dtype)


def hgrn2(proj, lb_raw, norm_w, n_heads, layer_idx, ts=128, nh=2):
    B, S, _ = proj.shape
    ts = _pick(S, ts)
    assert n_heads % nh == 0
    H = n_heads // nh
    L = lb_raw.shape[0]
    w = nh * HEAD_DIM

    def col(off):
        return pl.BlockSpec((1, ts, w), lambda b, h, s: (b, s, off * H + h))

    return pl.pallas_call(
        functools.partial(_hgrn_body, ts=ts, layer_idx=layer_idx, nh=nh),
        grid=(B, H, S // ts),
        in_specs=[col(0), col(1), col(2), col(3),
                  pl.BlockSpec((L, w), lambda b, h, s: (0, h)),
                  pl.BlockSpec((1, HEAD_DIM), lambda b, h, s: (0, 0))],
        out_specs=pl.BlockSpec((1, ts, w), lambda b, h, s: (b, s, h)),
        out_shape=jax.ShapeDtypeStruct((B, S, n_heads * HEAD_DIM), BF16),
        scratch_shapes=[pltpu.VMEM((nh, HEAD_DIM, HEAD_DIM), F32)],
        compiler_params=_cparams("parallel", "parallel", "arbitrary"),
        name="hgrn2",
    )(proj, proj, proj, proj, lb_raw.astype(F32), norm_w.reshape(1, HEAD_DIM).astype(F32))


def _sb_block(q, k, v, carry, after01, scale, mask):
    z = lax.dot_general(q, k, _NT, preferred_element_type=F32) * scale
    sp = jnp.maximum(z, 0.0) + jnp.log(1.0 + jnp.exp(-jnp.abs(z)))
    lm = -sp
    if mask is not None:
        lm = jnp.where(mask, lm, 0.0)
    rev = _split_dot(lm, after01)
    w = jnp.exp(z - sp + rev + carry)
    if mask is not None:
        w = jnp.where(mask, w, 0.0)
    contrib = jnp.dot(w.astype(BF16), v, preferred_element_type=F32)
    return contrib, carry + rev[:, 0:1] + lm[:, 0:1]


def _sb_body(q_ref, k_ref, v_ref, o_ref, *, tq, scale, nh):
    qi = pl.program_id(2)
    row = lax.broadcasted_iota(jnp.int32, (tq, tq), 0)
    col = lax.broadcasted_iota(jnp.int32, (tq, tq), 1)
    after01 = jnp.where(row > col, 1.0, 0.0).astype(BF16)
    heads = [slice(h * HEAD_DIM, (h + 1) * HEAD_DIM) for h in range(nh)]
    qs = [q_ref[0, :, sl] for sl in heads]

    def blocks(k0, carries, mask):
        res = [_sb_block(qs[h], k_ref[0, pl.ds(k0, tq), heads[h]], v_ref[0, pl.ds(k0, tq), heads[h]],
                         carries[h], after01, scale, mask) for h in range(nh)]
        return tuple(r[0] for r in res), tuple(r[1] for r in res)

    accs, carries = blocks(pl.multiple_of(qi * tq, tq), (jnp.zeros((tq, 1), F32),) * nh, col < row)

    def cond(c):
        j, _, carries = c
        top = functools.reduce(jnp.maximum, [jnp.max(cr) for cr in carries])
        return jnp.logical_and(j >= 0, top > EXP_ZERO_BELOW)

    def body(c):
        j, accs, carries = c
        contribs, carries = blocks(pl.multiple_of(j * tq, tq), carries, None)
        return j - 1, tuple(a + cb for a, cb in zip(accs, contribs)), carries

    _, accs, _ = lax.while_loop(cond, body, (qi - 1, accs, carries))
    for h in range(nh):
        o_ref[0, :, heads[h]] = accs[h].astype(o_ref.dtype)


def stick_breaking(proj, col_q, col_k, col_v, n_heads, tq=256, nh=2):
    B, S, _ = proj.shape
    tq = _pick(S, tq)
    assert n_heads % nh == 0 and col_q % nh == 0 and col_k % nh == 0 and col_v % nh == 0
    w = nh * HEAD_DIM
    full = lambda off: pl.BlockSpec((1, S, w), lambda b, h, i: (b, 0, off // nh + h))
    return pl.pallas_call(
        functools.partial(_sb_body, tq=tq, scale=HEAD_DIM ** -0.5, nh=nh),
        grid=(B, n_heads // nh, S // tq),
        in_specs=[pl.BlockSpec((1, tq, w), lambda b, h, i: (b, i, col_q // nh + h)),
                  full(col_k), full(col_v)],
        out_specs=pl.BlockSpec((1, tq, w), lambda b, h, i: (b, i, h)),
        out_shape=jax.ShapeDtypeStruct((B, S, n_heads * HEAD_DIM), BF16),
        compiler_params=_cparams("parallel", "parallel", "arbitrary"),
        name="stick_breaking",
    )(proj, proj, proj)


def _rope_body(x_ref, cos_ref, sin_ref, o_ref, *, n_heads, mult):
    cos = cos_ref[...] * mult
    sin = sin_ref[...] * mult
    for h in range(n_heads):
        sl = slice(h * HEAD_DIM, (h + 1) * HEAD_DIM)
        t = x_ref[0, :, sl].astype(F32)
        o_ref[0, :, sl] = (t * cos + pltpu.roll(t, HEAD_DIM // 2, 1) * sin).astype(o_ref.dtype)


def rope(x, col0, n_heads, cos, sin, mult=1.0, ts=256):
    B, S, _ = x.shape
    ts = _pick(S, ts)
    w = n_heads * HEAD_DIM
    assert col0 % w == 0
    cb = col0 // w
    tab = pl.BlockSpec((ts, HEAD_DIM), lambda b, s: (s, 0))
    return pl.pallas_call(
        functools.partial(_rope_body, n_heads=n_heads, mult=mult),
        grid=(B, S // ts),
        in_specs=[pl.BlockSpec((1, ts, w), lambda b, s: (b, s, cb)), tab, tab],
        out_specs=pl.BlockSpec((1, ts, w), lambda b, s: (b, s, 0)),
        out_shape=jax.ShapeDtypeStruct((B, S, w), BF16),
        compiler_params=_cparams("parallel", "parallel"),
        name="rope",
    )(x, cos, sin)


def _gelu(x):
    return 0.5 * x * (1.0 + lax.erf(x * (2.0 ** -0.5)))


def _compress_body(x_ref, pe_ref, w1_ref, w2_ref, o_ref, *, batch, k_mult):
    x = x_ref[0, 0]
    half = x.shape[1]
    n16 = x.shape[0]
    y1 = jnp.dot(x, w1_ref[0, :half], preferred_element_type=F32)
    y2 = jnp.dot(x, w1_ref[0, half:], preferred_element_type=F32)
    bias = jnp.dot(pe_ref[0], w1_ref[0], preferred_element_type=F32)[0:1]
    hid = _gelu(y1 + pltpu.roll(y2, n16 - 1, 0) + bias)
    out = jnp.dot(hid.astype(BF16), w2_ref[0], preferred_element_type=F32)
    mult = jnp.where(pl.program_id(0) < batch, k_mult, 1.0)
    o_ref[0, 0] = (out * mult).astype(o_ref.dtype)


def compress(x16, pe, w1, w2, k_mult):
    two, B, G, n16, wide = x16.shape
    x16 = x16.reshape(two * B, G, n16, wide)
    out = pl.pallas_call(
        functools.partial(_compress_body, batch=B, k_mult=k_mult),
        grid=(two * B, G),
        in_specs=[pl.BlockSpec((1, 1, n16, wide), lambda i, g: (i, g, 0, 0)),
                  pl.BlockSpec((1, 8, 2 * wide), lambda i, g: (i // B, 0, 0)),
                  pl.BlockSpec((1, 2 * wide, HEAD_DIM), lambda i, g: (i // B, 0, 0)),
                  pl.BlockSpec((1, HEAD_DIM, HEAD_DIM), lambda i, g: (i // B, 0, 0))],
        out_specs=pl.BlockSpec((1, 1, n16, HEAD_DIM), lambda i, g: (i, g, 0, 0)),
        out_shape=jax.ShapeDtypeStruct((two * B, G, n16, HEAD_DIM), BF16),
        compiler_params=_cparams("parallel", "parallel"),
        name="nsa_compress",
    )(x16, pe, w1, w2)
    return out.reshape(two, B, G, n16, HEAD_DIM)


def _stack_heads(q_ref, rep):
    return jnp.concatenate([q_ref[0, :, r * HEAD_DIM:(r + 1) * HEAD_DIM] for r in range(rep)], axis=0)


def _cmp_body(q_ref, kc_ref, vc_ref, o_ref, imp_ref, *, tq, rep):
    q0 = pl.program_id(2) * tq
    kc = kc_ref[0, 0]
    n16 = kc.shape[0]
    rows = rep * tq
    t = q0 + lax.broadcasted_iota(jnp.int32, (tq, n16), 0)
    n = lax.broadcasted_iota(jnp.int32, (tq, n16), 1)
    bias = jnp.where(n * CMP_STRIDE + (CMP_LEN - 1) <= t, 0.0, NEG_INF)
    s = lax.dot_general(_stack_heads(q_ref, rep), kc, _NT, preferred_element_type=F32)
    s3 = s.reshape(rep, tq, n16) + bias[None]
    m = jnp.maximum(jnp.max(s3, axis=-1, keepdims=True), 0.1 * NEG_INF)
    p = jnp.exp2(s3 - m)
    den = jnp.sum(p, axis=-1, keepdims=True)
    pn = p * (1.0 / jnp.where(den > 0.0, den, 1.0))
    o = jnp.dot(pn.reshape(rows, n16).astype(BF16), vc_ref[0, 0], preferred_element_type=F32)
    for r in range(rep):
        o_ref[0, :, r * HEAD_DIM:(r + 1) * HEAD_DIM] = o[r * tq:(r + 1) * tq].astype(o_ref.dtype)

    cn = lax.broadcasted_iota(jnp.int32, (n16, LANES), 0) * CMP_STRIDE
    cj = lax.broadcasted_iota(jnp.int32, (n16, LANES), 1) * SLC_LEN
    ov01 = jnp.where((cn < cj + SLC_LEN) & (cn + CMP_LEN > cj), 1.0, 0.0).astype(BF16)
    imp_ref[0, 0] = _split_dot(jnp.sum(pn, axis=0), ov01)


def cmp_attention(q, kc, vc, rep, tq=128):
    B, S, _ = q.shape
    G, n16 = kc.shape[1], kc.shape[2]
    tq = _pick(S, tq)
    w = rep * HEAD_DIM
    kv = pl.BlockSpec((1, 1, n16, HEAD_DIM), lambda b, g, i: (b, g, 0, 0))
    return pl.pallas_call(
        functools.partial(_cmp_body, tq=tq, rep=rep),
        grid=(B, G, S // tq),
        in_specs=[pl.BlockSpec((1, tq, w), lambda b, g, i: (b, i, g)), kv, kv],
        out_specs=[pl.BlockSpec((1, tq, w), lambda b, g, i: (b, i, g)),
                   pl.BlockSpec((1, 1, tq, LANES), lambda b, g, i: (b, g, i, 0))],
        out_shape=[jax.ShapeDtypeStruct((B, S, G * w), BF16),
                   jax.ShapeDtypeStruct((B, G, S, LANES), F32)],
        compiler_params=_cparams("parallel", "parallel", "parallel"),
        name="nsa_cmp",
    )(q, kc, vc)


def _topk_body(imp_ref, sel_ref, *, tq, n_slc):
    q0 = pl.program_id(2) * tq
    imp = imp_ref[0, 0]
    tt = q0 + lax.broadcasted_iota(jnp.int32, (tq, LANES), 0)
    j = lax.broadcasted_iota(jnp.int32, (tq, LANES), 1)
    cur = tt // SLC_LEN
    forced = (j == 0) | (j == cur) | (j == cur - 1)
    allowed = j * SLC_LEN <= tt
    score = jnp.where(forced, FORCE_SCORE, jnp.where(allowed, imp, -1.0))
    score = jnp.where(j < n_slc, score, -jnp.inf)
    jf = j.astype(F32)
    sel = jnp.zeros((tq, LANES), F32)
    for _ in range(min(SLC_TOP, n_slc)):
        m = jnp.max(score, axis=-1, keepdims=True)
        first = jnp.min(jnp.where(score == m, jf, float(LANES)), axis=-1, keepdims=True)
        pick = jf == first
        sel = jnp.where(pick, 1.0, sel)
        score = jnp.where(pick, -jnp.inf, score)
    sel_ref[0, 0] = sel.astype(sel_ref.dtype)


def select_blocks(imp, tq=1024):
    B, G, S, _ = imp.shape
    n_slc = S // SLC_LEN
    assert n_slc <= LANES
    tq = _pick(S, tq)
    spec = pl.BlockSpec((1, 1, tq, LANES), lambda b, g, i: (b, g, i, 0))
    return pl.pallas_call(
        functools.partial(_topk_body, tq=tq, n_slc=n_slc),
        grid=(B, G, S // tq),
        in_specs=[spec],
        out_specs=spec,
        out_shape=jax.ShapeDtypeStruct((B, G, S, LANES), BF16),
        compiler_params=_cparams("parallel", "parallel", "parallel"),
        name="nsa_topk",
    )(imp)


def _gqa_body(*refs, tq, tk, rep, mode):
    if mode == "sel":
        q_ref, k_ref, v_ref, sel_ref, exp_ref, o_ref, bias_ref = refs
        picked = jnp.dot(sel_ref[0, 0], exp_ref[...], preferred_element_type=F32)
        bias_ref[...] = (picked - 1.0) * (-NEG_INF)
    else:
        q_ref, k_ref, v_ref, o_ref = refs
    q0 = pl.program_id(2) * tq
    rows = rep * tq
    q2 = _stack_heads(q_ref, rep)
    t = q0 + lax.broadcasted_iota(jnp.int32, (tq, tk), 0)
    c = lax.broadcasted_iota(jnp.int32, (tq, tk), 1)
    ones = jnp.ones((tk, HEAD_DIM), BF16)

    def tile(kj, m, acc, diagonal):
        k0 = pl.multiple_of(kj * tk, tk)
        kp = k0 + c
        if mode == "sel":
            bias = bias_ref[:, pl.ds(k0, tk)]
            if diagonal:
                bias = jnp.where(kp <= t, bias, NEG_INF)
        else:
            ok = kp > t - WINDOW
            if diagonal:
                ok = ok & (kp <= t)
            bias = jnp.where(ok, 0.0, NEG_INF)
        s = lax.dot_general(q2, k_ref[0, pl.ds(k0, tk), :], _NT, preferred_element_type=F32)
        s3 = s.reshape(rep, tq, tk) + bias[None]
        m_new = jnp.maximum(m, jnp.max(s3, axis=-1, keepdims=True))
        p = jnp.exp2(s3 - m_new)
        vx = jnp.concatenate([v_ref[0, pl.ds(k0, tk), :], ones], axis=1)
        pv = jnp.dot(p.reshape(rows, tk).astype(BF16), vx, preferred_element_type=F32)
        acc = jnp.exp2(m - m_new) * acc + pv.reshape(rep, tq, 2 * HEAD_DIM)
        return m_new, acc

    kd = q0 // tk
    m, acc = tile(kd, jnp.full((rep, tq, 1), NEG_INF, F32), jnp.zeros((rep, tq, 2 * HEAD_DIM), F32), True)
    lo = 0 if mode == "sel" else jnp.maximum(q0 - (WINDOW - 1), 0) // tk
    m, acc = lax.fori_loop(lo, kd, lambda kj, c_: tile(kj, c_[0], c_[1], False), (m, acc))
    out = acc[:, :, :HEAD_DIM] * (1.0 / acc[:, :, HEAD_DIM:HEAD_DIM + 1])
    for r in range(rep):
        o_ref[0, :, r * HEAD_DIM:(r + 1) * HEAD_DIM] = out[r].astype(o_ref.dtype)


def gqa_attention(q, k, v, v_col0, rep, mode, sel=None, tq=128, tk=512):
    B, S, _ = q.shape
    G = k.shape[2] // HEAD_DIM
    tq, tk = _pick(S, tq), _pick(S, tk)
    assert tk % tq == 0
    w = rep * HEAD_DIM
    in_specs = [pl.BlockSpec((1, tq, w), lambda b, g, i: (b, i, g)),
                pl.BlockSpec((1, S, HEAD_DIM), lambda b, g, i: (b, 0, g)),
                pl.BlockSpec((1, S, HEAD_DIM), lambda b, g, i: (b, 0, v_col0 + g))]
    args = [q, k, v]
    scratch = []
    if mode == "sel":
        blk = lax.broadcasted_iota(jnp.int32, (LANES, S), 0)
        key = lax.broadcasted_iota(jnp.int32, (LANES, S), 1)
        expand01 = (key // SLC_LEN == blk).astype(BF16)
        in_specs += [pl.BlockSpec((1, 1, tq, LANES), lambda b, g, i: (b, g, i, 0)),
                     pl.BlockSpec((LANES, S), lambda b, g, i: (0, 0))]
        args += [sel, expand01]
        scratch = [pltpu.VMEM((tq, S), F32)]
    return pl.pallas_call(
        functools.partial(_gqa_body, tq=tq, tk=tk, rep=rep, mode=mode),
        grid=(B, G, S // tq),
        in_specs=in_specs,
        out_specs=pl.BlockSpec((1, tq, w), lambda b, g, i: (b, i, g)),
        out_shape=jax.ShapeDtypeStruct((B, S, G * w), BF16),
        scratch_shapes=scratch,
        compiler_params=_cparams("parallel", "parallel", "arbitrary"),
        name="nsa_" + mode,
    )(*args)


def _nsa_gate_body(gl_ref, oc_ref, os_ref, ow_ref, o_ref, *, n_heads):
    gl = gl_ref[...].astype(F32)
    gate = 1.0 / (1.0 + jnp.exp(-gl))
    ng = 3 * n_heads
    src = lax.broadcasted_iota(jnp.int32, (ng, n_heads * HEAD_DIM), 0)
    head = lax.broadcasted_iota(jnp.int32, (ng, n_heads * HEAD_DIM), 1) // HEAD_DIM
    out = None
    for c, ref in enumerate((oc_ref, os_ref, ow_ref)):
        spread01 = jnp.where(src == head * 3 + c, 1.0, 0.0).astype(BF16)
        term = _split_dot(gate, spread01) * ref[...].astype(F32)
        out = term if out is None else out + term
    o_ref[...] = out.astype(o_ref.dtype)


def nsa_gate(gl, oc, os_, ow, n_heads, tm=256):
    M, W = oc.shape
    tm = _pick(M, tm)
    row = pl.BlockSpec((tm, W), lambda i: (i, 0))
    return pl.pallas_call(
        functools.partial(_nsa_gate_body, n_heads=n_heads),
        grid=(M // tm,),
        in_specs=[pl.BlockSpec((tm, gl.shape[1]), lambda i: (i, 0)), row, row, row],
        out_specs=row,
        out_shape=jax.ShapeDtypeStruct((M, W), BF16),
        compiler_params=_cparams("parallel"),
        name="nsa_gate",
    )(gl, oc, os_, ow)


def _xattn_body(q_ref, kv_ref, o_ref, *, n_heads, scale):
    w = n_heads * HEAD_DIM
    for h in range(n_heads):
        sl = slice(h * HEAD_DIM, (h + 1) * HEAD_DIM)
        k = kv_ref[0, :, sl]
        v = kv_ref[0, :, w + h * HEAD_DIM:w + (h + 1) * HEAD_DIM]
        s = lax.dot_general(q_ref[0, :, sl], k, _NT, preferred_element_type=F32) * scale
        p = jnp.exp(s - jnp.max(s, axis=-1, keepdims=True))
        p = p / jnp.sum(p, axis=-1, keepdims=True)
        o_ref[0, :, sl] = jnp.dot(p.astype(BF16), v, preferred_element_type=F32).astype(o_ref.dtype)


def mem_attention(q, kv, n_heads, tq=512):
    B, S, w = q.shape
    n_mem = kv.shape[1]
    tq = _pick(S, tq)
    return pl.pallas_call(
        functools.partial(_xattn_body, n_heads=n_heads, scale=HEAD_DIM ** -0.5),
        grid=(B, S // tq),
        in_specs=[pl.BlockSpec((1, tq, w), lambda b, i: (b, i, 0)),
                  pl.BlockSpec((1, n_mem, 2 * w), lambda b, i: (b, 0, 0))],
        out_specs=pl.BlockSpec((1, tq, w), lambda b, i: (b, i, 0)),
        out_shape=jax.ShapeDtypeStruct((B, S, w), BF16),
        compiler_params=_cparams("parallel", "parallel"),
        name="mem_attention",
    )(q, kv)


def _ffn_up_body(x_ref, wa_ref, wu_ref, cw_ref, o_ref, tail_ref, *, tiles_per_seq):
    x = x_ref[...]
    a = jnp.dot(x, wa_ref[...], preferred_element_type=F32)
    tm = a.shape[0]
    first = (pl.program_id(1) % tiles_per_seq) == 0
    prev = jnp.where(first, 0.0, tail_ref[...])
    tail_ref[...] = a[tm - 8:]
    rowi = lax.broadcasted_iota(jnp.int32, a.shape, 0)
    a1 = jnp.where(rowi >= 1, pltpu.roll(a, 1, 0), prev[7:8])
    a2 = jnp.where(rowi >= 2, pltpu.roll(a, 2, 0), jnp.where(rowi == 1, prev[7:8], prev[6:7]))
    cw = cw_ref[...]
    gate = _gelu(cw[2:3] * a + cw[1:2] * a1 + cw[0:1] * a2)
    u = jnp.dot(x, wu_ref[...], preferred_element_type=F32)
    o_ref[...] = (gate * u).astype(o_ref.dtype)


def ffn_up_glu(x, w_up, conv_w, seq_len, tm=1024, tn=256):
    M, K = x.shape
    Fd = w_up.shape[1] // 2
    tm, tn = _pick(seq_len, tm), _pick(Fd, tn)
    nj = Fd // tn
    return pl.pallas_call(
        functools.partial(_ffn_up_body, tiles_per_seq=seq_len // tm),
        grid=(nj, M // tm),
        in_specs=[pl.BlockSpec((tm, K), lambda j, i: (i, 0)),
                  pl.BlockSpec((K, tn), lambda j, i: (0, j)),
                  pl.BlockSpec((K, tn), lambda j, i: (0, j + nj)),
                  pl.BlockSpec((CONV_W, tn), lambda j, i: (0, j))],
        out_specs=pl.BlockSpec((tm, tn), lambda j, i: (i, j)),
        out_shape=jax.ShapeDtypeStruct((M, Fd), BF16),
        scratch_shapes=[pltpu.VMEM((8, tn), F32)],
        compiler_params=_cparams("parallel", "arbitrary"),
        name="ffn_up_glu",
    )(x, w_up, w_up, conv_w.astype(F32))


def _rope_tables(S):
    half = HEAD_DIM // 2
    inv_freq = ROPE_THETA ** (-jnp.arange(half, dtype=F32) / half)
    ang = jnp.arange(S, dtype=F32)[:, None] * inv_freq[None, :]
    cos, sin = jnp.cos(ang), jnp.sin(ang)
    return jnp.concatenate([cos, cos], axis=-1), jnp.concatenate([-sin, sin], axis=-1)


def _hgrn_sb_mixer(hb, B, S, w_in, lb_raw, norm_w, w_out, e):
    width = w_out.shape[0]
    a_heads = width // (2 * HEAD_DIM)
    b_heads = a_heads
    proj = matmul(hb, w_in.astype(BF16)).reshape(B, S, -1)
    o_a = hgrn2(proj, lb_raw, norm_w, a_heads, e)
    o_b = stick_breaking(proj, 4 * a_heads, 4 * a_heads + b_heads, 4 * a_heads + 2 * b_heads, b_heads)
    o = jnp.concatenate([o_a, o_b], axis=-1).reshape(B * S, width)
    return matmul(o, w_out.astype(BF16))


def _nsa_mixer(hb, B, S, w_in, cmp_pos, cmp_w1, cmp_w2, w_out, cos, sin):
    G = NSA_KV_HEADS
    q_w = w_out.shape[0]
    n_heads = q_w // HEAD_DIM
    rep = n_heads // G
    kv_w = G * HEAD_DIM
    main_w = q_w + 6 * kv_w
    w_in = w_in.astype(BF16)
    proj = matmul(hb, w_in[:, :main_w]).reshape(B, S, main_w)
    gl = matmul(hb, w_in[:, main_w:])

    log2_scale = HEAD_DIM ** -0.5 * LOG2E
    q_rot = rope(proj, 0, n_heads, cos, sin, mult=log2_scale)
    ks_rot = rope(proj, q_w + 2 * kv_w, G, cos, sin)
    kw_rot = rope(proj, q_w + 4 * kv_w, G, cos, sin)

    n16 = S // CMP_STRIDE
    kvc_in = proj[:, :, q_w:q_w + 2 * kv_w].reshape(B, S, 2, G, HEAD_DIM)
    x16 = kvc_in.transpose(2, 0, 3, 1, 4).reshape(2, B, G, n16, CMP_STRIDE * HEAD_DIM)
    pe = jnp.broadcast_to(cmp_pos.reshape(2, 1, CMP_LEN * HEAD_DIM), (2, 8, CMP_LEN * HEAD_DIM)).astype(BF16)
    w1 = cmp_w1.reshape(2, CMP_LEN * HEAD_DIM, HEAD_DIM).astype(BF16)
    kvc = compress(x16, pe, w1, cmp_w2.astype(BF16), log2_scale)

    o_c, imp = cmp_attention(proj, kvc[0], kvc[1], rep)
    sel = select_blocks(imp)
    col = lambda off: (q_w + off * kv_w) // HEAD_DIM
    o_s = gqa_attention(q_rot, ks_rot, proj, col(3), rep, "sel", sel=sel, tq=128, tk=512)
    o_w = gqa_attention(q_rot, kw_rot, proj, col(5), rep, "win", tq=256, tk=256)
    o = nsa_gate(gl, o_c.reshape(B * S, q_w), o_s.reshape(B * S, q_w), o_w.reshape(B * S, q_w), n_heads)
    return matmul(o, w_out.astype(BF16))


def kernel(x, mem, ab_w_in, hgrn_lb, hgrn_norm_w, ab_w_out, nsa_w_in, nsa_cmp_pos, nsa_cmp_w1,
           nsa_cmp_w2, nsa_w_out, xa_w_q, xa_w_kv, xa_w_o, ffn_w_up, ffn_conv, ffn_w_down, ln_g, ln_b):
    B, S, D = x.shape
    depth = ln_g.shape[0]
    alpha = (2 * depth) ** 0.25
    n_mem = mem.shape[1]
    cos, sin = _rope_tables(S)
    h = x.reshape(B * S, D).astype(F32)
    hb = h.astype(BF16)
    memb = mem.reshape(B * n_mem, D).astype(BF16)
    for layer in range(depth):
        if layer % 2 == 0:
            e = layer // 2
            mix = _hgrn_sb_mixer(hb, B, S, ab_w_in[e], hgrn_lb, hgrn_norm_w[e], ab_w_out[e], e)
        else:
            o = layer // 2
            mix = _nsa_mixer(hb, B, S, nsa_w_in[o], nsa_cmp_pos[o], nsa_cmp_w1[o], nsa_cmp_w2[o],
                             nsa_w_out[o], cos, sin)
        h, hb = add_layer_norm(h, mix, ln_g[layer, 0], ln_b[layer, 0], alpha)

        xq = matmul(hb, xa_w_q[layer].astype(BF16)).reshape(B, S, -1)
        xkv = matmul(memb, xa_w_kv[layer].astype(BF16)).reshape(B, n_mem, -1)
        xo = mem_attention(xq, xkv, XA_HEADS).reshape(B * S, -1)
        h, hb = add_layer_norm(h, matmul(xo, xa_w_o[layer].astype(BF16)), ln_g[layer, 1], ln_b[layer, 1], alpha)

        gated = ffn_up_glu(hb, ffn_w_up[layer].astype(BF16), ffn_conv[layer], S)
        h, hb = add_layer_norm(h, matmul(gated, ffn_w_down[layer].astype(BF16), tk=5504),
                               ln_g[layer, 2], ln_b[layer, 2], alpha)
    return h.reshape(B, S, D).astype(x.dtype)
```

```python
import functools

import jax
import jax.numpy as jnp
from jax import lax
from jax.experimental import pallas as pl
from jax.experimental.pallas import tpu as pltpu

F32 = jnp.float32
BF16 = jnp.bfloat16

HEAD_DIM = 128
LANES = 128
HGRN_SUB = 16
HGRN_SAFE_LOG_DECAY = -60.0
NSA_KV_HEADS = 4
CMP_LEN = 32
CMP_STRIDE = 16
SLC_LEN = 64
SLC_TOP = 16
WINDOW = 512
XA_HEADS = 4
CONV_W = 3
ROPE_THETA = 10000.0
LN_EPS = 1e-5
RMS_EPS = 1e-6
NEG_INF = -1e30
FORCE_SCORE = 1e9
EXP_ZERO_BELOW = -104.0
LOG2E = 1.4426950408889634
VMEM_LIMIT = 52 * 1024 * 1024

_NT = (((1,), (1,)), ((), ()))
_TN = (((0,), (0,)), ((), ()))


def _cparams(*sem):
    return pltpu.CompilerParams(dimension_semantics=sem, vmem_limit_bytes=VMEM_LIMIT)


def _split_dot(a, b01):
    hi = a.astype(BF16)
    lo = (a - hi.astype(F32)).astype(BF16)
    return (jnp.dot(hi, b01, preferred_element_type=F32)
            + jnp.dot(lo, b01, preferred_element_type=F32))


def _mm_body(x_ref, w_ref, o_ref, *scratch, nk):
    prod = jnp.dot(x_ref[...], w_ref[...], preferred_element_type=F32)
    if nk == 1:
        o_ref[...] = prod.astype(o_ref.dtype)
        return
    acc_ref, = scratch
    k = pl.program_id(2)

    @pl.when(k == 0)
    def _():
        acc_ref[...] = prod

    @pl.when(k > 0)
    def _():
        acc_ref[...] += prod

    @pl.when(k == nk - 1)
    def _():
        o_ref[...] = acc_ref[...].astype(o_ref.dtype)


def _pick(n, pref):
    if n <= pref:
        return n
    t = pref
    while t >= LANES:
        if n % t == 0:
            return t
        t -= LANES
    return n


def matmul(x, w, out_dtype=BF16, tm=1024, tn=512, tk=4096):
    M, K = x.shape
    N = w.shape[1]
    tm, tn, tk = _pick(M, tm), _pick(N, tn), _pick(K, tk)
    nk = K // tk
    scratch = [] if nk == 1 else [pltpu.VMEM((tm, tn), F32)]
    return pl.pallas_call(
        functools.partial(_mm_body, nk=nk),
        grid=(M // tm, N // tn, nk),
        in_specs=[pl.BlockSpec((tm, tk), lambda i, j, k: (i, k)),
                  pl.BlockSpec((tk, tn), lambda i, j, k: (k, j))],
        out_specs=pl.BlockSpec((tm, tn), lambda i, j, k: (i, j)),
        out_shape=jax.ShapeDtypeStruct((M, N), out_dtype),
        scratch_shapes=scratch,
        compiler_params=_cparams("parallel", "parallel", "arbitrary"),
        name="matmul",
    )(x, w)


def _add_ln_body(h_ref, m_ref, g_ref, b_ref, o32_ref, o16_ref, *, alpha):
    y = alpha * h_ref[...] + m_ref[...].astype(F32)
    mu = jnp.mean(y, axis=-1, keepdims=True)
    d = y - mu
    var = jnp.mean(d * d, axis=-1, keepdims=True)
    out = d * lax.rsqrt(var + LN_EPS) * g_ref[...] + b_ref[...]
    o32_ref[...] = out
    o16_ref[...] = out.astype(BF16)


def add_layer_norm(h, mix, g, b, alpha, tm=256):
    M, D = h.shape
    tm = _pick(M, tm)
    row = pl.BlockSpec((tm, D), lambda i: (i, 0))
    vec = pl.BlockSpec((1, D), lambda i: (0, 0))
    return pl.pallas_call(
        functools.partial(_add_ln_body, alpha=alpha),
        grid=(M // tm,),
        in_specs=[row, row, vec, vec],
        out_specs=[row, row],
        out_shape=[jax.ShapeDtypeStruct((M, D), F32), jax.ShapeDtypeStruct((M, D), BF16)],
        compiler_params=_cparams("parallel"),
        name="add_layer_norm",
    )(h, mix, g.reshape(1, D).astype(F32), b.reshape(1, D).astype(F32))


def _hgrn_body(q_ref, f_ref, i_ref, g_ref, lb_ref, nw_ref, o_ref, st_ref, *, ts, layer_idx, nh):
    C = HGRN_SUB
    nsub = ts // C
    heads = [slice(h * HEAD_DIM, (h + 1) * HEAD_DIM) for h in range(nh)]

    @pl.when(pl.program_id(2) == 0)
    def _():
        st_ref[...] = jnp.zeros_like(st_ref)

    lbr = lb_ref[...]
    ex = jnp.exp(lbr - jnp.max(lbr, axis=0, keepdims=True))
    sm = ex / jnp.sum(ex, axis=0, keepdims=True)
    lb = jnp.sum(sm[:layer_idx + 1], axis=0, keepdims=True)

    q = q_ref[0].astype(F32)
    z = f_ref[0].astype(F32)
    v = i_ref[0]
    e = jnp.exp(-jnp.abs(z))
    r = 1.0 / (1.0 + e)
    pos = z >= 0
    sig = jnp.where(pos, r, e * r)
    nsig = jnp.where(pos, e * r, r)
    logf = jnp.log(lb + (1.0 - lb) * sig)
    k = (1.0 - lb) * nsig

    hi = logf.astype(BF16)
    lo = (logf - hi.astype(F32)).astype(BF16)
    row = lax.broadcasted_iota(jnp.int32, (ts, ts), 0)
    col = lax.broadcasted_iota(jnp.int32, (ts, ts), 1)

    def cumdot(m01):
        return jnp.dot(m01, hi, preferred_element_type=F32) + jnp.dot(m01, lo, preferred_element_type=F32)

    def whole_tile(sts):
        b = cumdot(jnp.where(row >= col, 1.0, 0.0).astype(BF16))
        bl = b[ts - 1:ts]
        qd = (q * jnp.exp(b)).astype(BF16)
        kinv = (k * jnp.exp(-b)).astype(BF16)
        kd = (k * jnp.exp(bl - b)).astype(BF16)
        dec = jnp.exp(bl)
        outs, new = [], []
        for h, sl in enumerate(heads):
            dmat = lax.dot_general(qd[:, sl], kinv[:, sl], _NT, preferred_element_type=F32)
            dmat = jnp.where(row >= col, dmat, 0.0)
            o = jnp.dot(dmat.astype(BF16), v[:, sl], preferred_element_type=F32)
            outs.append(o + lax.dot_general(qd[:, sl], sts[h].astype(BF16), _NT, preferred_element_type=F32))
            new.append(sts[h] * dec[:, sl] + lax.dot_general(v[:, sl], kd[:, sl], _TN, preferred_element_type=F32))
        return tuple(outs), tuple(new)

    def sub_chunks(sts):
        same = (row // C) == (col // C)
        b = cumdot(jnp.where(same & (row >= col), 1.0, 0.0).astype(BF16))
        bl = cumdot(jnp.where(same, 1.0, 0.0).astype(BF16))
        qd = (q * jnp.exp(b)).astype(BF16)
        kd = (k * jnp.exp(bl - b)).astype(BF16)
        dec = jnp.exp(bl)
        tri = (lax.broadcasted_iota(jnp.int32, (C, C, HEAD_DIM), 0)
               >= lax.broadcasted_iota(jnp.int32, (C, C, HEAD_DIM), 1))
        outs, new = [], []
        for h, hs in enumerate(heads):
            st = sts[h]
            parts = []
            for n in range(nsub):
                sl = slice(n * C, (n + 1) * C)
                bn, qn, kn = b[sl, hs], q[sl, hs], k[sl, hs]
                diff = bn[:, None, :] - bn[None, :, :]
                ee = jnp.exp(jnp.where(tri, diff, NEG_INF))
                dmat = jnp.sum(qn[:, None, :] * (kn[None, :, :] * ee), axis=-1)
                o_n = jnp.dot(dmat.astype(BF16), v[sl, hs], preferred_element_type=F32)
                o_n = o_n + lax.dot_general(qd[sl, hs], st.astype(BF16), _NT, preferred_element_type=F32)
                upd = lax.dot_general(v[sl, hs], kd[sl, hs], _TN, preferred_element_type=F32)
                st = st * dec[n * C:n * C + 1, hs] + upd
                parts.append(o_n)
            outs.append(jnp.concatenate(parts, axis=0))
            new.append(st)
        return tuple(outs), tuple(new)

    tile_decay = jnp.min(jnp.sum(logf, axis=0, keepdims=True))
    outs, sts = lax.cond(tile_decay > HGRN_SAFE_LOG_DECAY, whole_tile, sub_chunks,
                         tuple(st_ref[h] for h in range(nh)))
    gt = g_ref[0].astype(F32)
    gate = gt / (1.0 + jnp.exp(-gt))
    for h, sl in enumerate(heads):
        st_ref[h] = sts[h]
        o = outs[h]
        o = o * lax.rsqrt(jnp.mean(o * o, axis=-1, keepdims=True) + RMS_EPS) * nw_ref[...] * gate[:, sl]
        o_ref[0, :, sl] = o.astype(o_ref.dtype)


def hgrn2(proj, lb_raw, norm_w, n_heads, layer_idx, ts=128, nh=2):
    B, S, _ = proj.shape
    ts = _pick(S, ts)
    assert n_heads % nh == 0
    H = n_heads // nh
    L = lb_raw.shape[0]
    w = nh * HEAD_DIM

    def col(off):
        return pl.BlockSpec((1, ts, w), lambda b, h, s: (b, s, off * H + h))

    return pl.pallas_call(
        functools.partial(_hgrn_body, ts=ts, layer_idx=layer_idx, nh=nh),
        grid=(B, H, S // ts),
        in_specs=[col(0), col(1), col(2), col(3),
                  pl.BlockSpec((L, w), lambda b, h, s: (0, h)),
                  pl.BlockSpec((1, HEAD_DIM), lambda b, h, s: (0, 0))],
        out_specs=pl.BlockSpec((1, ts, w), lambda b, h, s: (b, s, h)),
        out_shape=jax.ShapeDtypeStruct((B, S, n_heads * HEAD_DIM), BF16),
        scratch_shapes=[pltpu.VMEM((nh, HEAD_DIM, HEAD_DIM), F32)],
        compiler_params=_cparams("parallel", "parallel", "arbitrary"),
        name="hgrn2",
    )(proj, proj, proj, proj, lb_raw.astype(F32), norm_w.reshape(1, HEAD_DIM).astype(F32))


def _sb_block(q, k, v, carry, after01, scale, mask):
    z = lax.dot_general(q, k, _NT, preferred_element_type=F32) * scale
    sp = jnp.maximum(z, 0.0) + jnp.log(1.0 + jnp.exp(-jnp.abs(z)))
    lm = -sp
    if mask is not None:
        lm = jnp.where(mask, lm, 0.0)
    rev = _split_dot(lm, after01)
    w = jnp.exp(z - sp + rev + carry)
    if mask is not None:
        w = jnp.where(mask, w, 0.0)
    contrib = jnp.dot(w.astype(BF16), v, preferred_element_type=F32)
    return contrib, carry + rev[:, 0:1] + lm[:, 0:1]


def _sb_body(q_ref, k_ref, v_ref, o_ref, *, tq, scale, nh):
    qi = pl.program_id(2)
    row = lax.broadcasted_iota(jnp.int32, (tq, tq), 0)
    col = lax.broadcasted_iota(jnp.int32, (tq, tq), 1)
    after01 = jnp.where(row > col, 1.0, 0.0).astype(BF16)
    heads = [slice(h * HEAD_DIM, (h + 1) * HEAD_DIM) for h in range(nh)]
    qs = [q_ref[0, :, sl] for sl in heads]

    def blocks(k0, carries, mask):
        res = [_sb_block(qs[h], k_ref[0, pl.ds(k0, tq), heads[h]], v_ref[0, pl.ds(k0, tq), heads[h]],
                         carries[h], after01, scale, mask) for h in range(nh)]
        return tuple(r[0] for r in res), tuple(r[1] for r in res)

    accs, carries = blocks(pl.multiple_of(qi * tq, tq), (jnp.zeros((tq, 1), F32),) * nh, col < row)

    def cond(c):
        j, _, carries = c
        top = functools.reduce(jnp.maximum, [jnp.max(cr) for cr in carries])
        return jnp.logical_and(j >= 0, top > EXP_ZERO_BELOW)

    def body(c):
        j, accs, carries = c
        contribs, carries = blocks(pl.multiple_of(j * tq, tq), carries, None)
        return j - 1, tuple(a + cb for a, cb in zip(accs, contribs)), carries

    _, accs, _ = lax.while_loop(cond, body, (qi - 1, accs, carries))
    for h in range(nh):
        o_ref[0, :, heads[h]] = accs[h].astype(o_ref.dtype)


def stick_breaking(proj, col_q, col_k, col_v, n_heads, tq=256, nh=2):
    B, S, _ = proj.shape
    tq = _pick(S, tq)
    assert n_heads % nh == 0 and col_q % nh == 0 and col_k % nh == 0 and col_v % nh == 0
    w = nh * HEAD_DIM
    full = lambda off: pl.BlockSpec((1, S, w), lambda b, h, i: (b, 0, off // nh + h))
    return pl.pallas_call(
        functools.partial(_sb_body, tq=tq, scale=HEAD_DIM ** -0.5, nh=nh),
        grid=(B, n_heads // nh, S // tq),
        in_specs=[pl.BlockSpec((1, tq, w), lambda b, h, i: (b, i, col_q // nh + h)),
                  full(col_k), full(col_v)],
        out_specs=pl.BlockSpec((1, tq, w), lambda b, h, i: (b, i, h)),
        out_shape=jax.ShapeDtypeStruct((B, S, n_heads * HEAD_DIM), BF16),
        compiler_params=_cparams("parallel", "parallel", "arbitrary"),
        name="stick_breaking",
    )(proj, proj, proj)


def _rope_body(x_ref, cos_ref, sin_ref, o_ref, *, n_heads, mult):
    cos = cos_ref[...] * mult
    sin = sin_ref[...] * mult
    for h in range(n_heads):
        sl = slice(h * HEAD_DIM, (h + 1) * HEAD_DIM)
        t = x_ref[0, :, sl].astype(F32)
        o_ref[0, :, sl] = (t * cos + pltpu.roll(t, HEAD_DIM // 2, 1) * sin).astype(o_ref.dtype)


def rope(x, col0, n_heads, cos, sin, mult=1.0, ts=256):
    B, S, _ = x.shape
    ts = _pick(S, ts)
    w = n_heads * HEAD_DIM
    assert col0 % w == 0
    cb = col0 // w
    tab = pl.BlockSpec((ts, HEAD_DIM), lambda b, s: (s, 0))
    return pl.pallas_call(
        functools.partial(_rope_body, n_heads=n_heads, mult=mult),
        grid=(B, S // ts),
        in_specs=[pl.BlockSpec((1, ts, w), lambda b, s: (b, s, cb)), tab, tab],
        out_specs=pl.BlockSpec((1, ts, w), lambda b, s: (b, s, 0)),
        out_shape=jax.ShapeDtypeStruct((B, S, w), BF16),
        compiler_params=_cparams("parallel", "parallel"),
        name="rope",
    )(x, cos, sin)


def _gelu(x):
    return 0.5 * x * (1.0 + lax.erf(x * (2.0 ** -0.5)))


def _compress_body(x_ref, pe_ref, w1_ref, w2_ref, o_ref, *, batch, k_mult):
    x = x_ref[0, 0]
    half = x.shape[1]
    n16 = x.shape[0]
    y1 = jnp.dot(x, w1_ref[0, :half], preferred_element_type=F32)
    y2 = jnp.dot(x, w1_ref[0, half:], preferred_element_type=F32)
    bias = jnp.dot(pe_ref[0], w1_ref[0], preferred_element_type=F32)[0:1]
    hid = _gelu(y1 + pltpu.roll(y2, n16 - 1, 0) + bias)
    out = jnp.dot(hid.astype(BF16), w2_ref[0], preferred_element_type=F32)
    mult = jnp.where(pl.program_id(0) < batch, k_mult, 1.0)
    o_ref[0, 0] = (out * mult).astype(o_ref.dtype)


def compress(x16, pe, w1, w2, k_mult):
    two, B, G, n16, wide = x16.shape
    x16 = x16.reshape(two * B, G, n16, wide)
    out = pl.pallas_call(
        functools.partial(_compress_body, batch=B, k_mult=k_mult),
        grid=(two * B, G),
        in_specs=[pl.BlockSpec((1, 1, n16, wide), lambda i, g: (i, g, 0, 0)),
                  pl.BlockSpec((1, 8, 2 * wide), lambda i, g: (i // B, 0, 0)),
                  pl.BlockSpec((1, 2 * wide, HEAD_DIM), lambda i, g: (i // B, 0, 0)),
                  pl.BlockSpec((1, HEAD_DIM, HEAD_DIM), lambda i, g: (i // B, 0, 0))],
        out_specs=pl.BlockSpec((1, 1, n16, HEAD_DIM), lambda i, g: (i, g, 0, 0)),
        out_shape=jax.ShapeDtypeStruct((two * B, G, n16, HEAD_DIM), BF16),
        compiler_params=_cparams("parallel", "parallel"),
        name="nsa_compress",
    )(x16, pe, w1, w2)
    return out.reshape(two, B, G, n16, HEAD_DIM)


def _stack_heads(q_ref, rep):
    return jnp.concatenate([q_ref[0, :, r * HEAD_DIM:(r + 1) * HEAD_DIM] for r in range(rep)], axis=0)


def _cmp_body(q_ref, kc_ref, vc_ref, o_ref, imp_ref, *, tq, rep):
    q0 = pl.program_id(2) * tq
    kc = kc_ref[0, 0]
    n16 = kc.shape[0]
    rows = rep * tq
    t = q0 + lax.broadcasted_iota(jnp.int32, (tq, n16), 0)
    n = lax.broadcasted_iota(jnp.int32, (tq, n16), 1)
    bias = jnp.where(n * CMP_STRIDE + (CMP_LEN - 1) <= t, 0.0, NEG_INF)
    s = lax.dot_general(_stack_heads(q_ref, rep), kc, _NT, preferred_element_type=F32)
    s3 = s.reshape(rep, tq, n16) + bias[None]
    m = jnp.maximum(jnp.max(s3, axis=-1, keepdims=True), 0.1 * NEG_INF)
    p = jnp.exp2(s3 - m)
    den = jnp.sum(p, axis=-1, keepdims=True)
    pn = p * (1.0 / jnp.where(den > 0.0, den, 1.0))
    o = jnp.dot(pn.reshape(rows, n16).astype(BF16), vc_ref[0, 0], preferred_element_type=F32)
    for r in range(rep):
        o_ref[0, :, r * HEAD_DIM:(r + 1) * HEAD_DIM] = o[r * tq:(r + 1) * tq].astype(o_ref.dtype)

    cn = lax.broadcasted_iota(jnp.int32, (n16, LANES), 0) * CMP_STRIDE
    cj = lax.broadcasted_iota(jnp.int32, (n16, LANES), 1) * SLC_LEN
    ov01 = jnp.where((cn < cj + SLC_LEN) & (cn + CMP_LEN > cj), 1.0, 0.0).astype(BF16)
    imp_ref[0, 0] = _split_dot(jnp.sum(pn, axis=0), ov01)


def cmp_attention(q, kc, vc, rep, tq=128):
    B, S, _ = q.shape
    G, n16 = kc.shape[1], kc.shape[2]
    tq = _pick(S, tq)
    w = rep * HEAD_DIM
    kv = pl.BlockSpec((1, 1, n16, HEAD_DIM), lambda b, g, i: (b, g, 0, 0))
    return pl.pallas_call(
        functools.partial(_cmp_body, tq=tq, rep=rep),
        grid=(B, G, S // tq),
        in_specs=[pl.BlockSpec((1, tq, w), lambda b, g, i: (b, i, g)), kv, kv],
        out_specs=[pl.BlockSpec((1, tq, w), lambda b, g, i: (b, i, g)),
                   pl.BlockSpec((1, 1, tq, LANES), lambda b, g, i: (b, g, i, 0))],
        out_shape=[jax.ShapeDtypeStruct((B, S, G * w), BF16),
                   jax.ShapeDtypeStruct((B, G, S, LANES), F32)],
        compiler_params=_cparams("parallel", "parallel", "parallel"),
        name="nsa_cmp",
    )(q, kc, vc)


def _topk_body(imp_ref, sel_ref, *, tq, n_slc):
    q0 = pl.program_id(2) * tq
    imp = imp_ref[0, 0]
    tt = q0 + lax.broadcasted_iota(jnp.int32, (tq, LANES), 0)
    j = lax.broadcasted_iota(jnp.int32, (tq, LANES), 1)
    cur = tt // SLC_LEN
    forced = (j == 0) | (j == cur) | (j == cur - 1)
    allowed = j * SLC_LEN <= tt
    score = jnp.where(forced, FORCE_SCORE, jnp.where(allowed, imp, -1.0))
    score = jnp.where(j < n_slc, score, -jnp.inf)
    jf = j.astype(F32)
    sel = jnp.zeros((tq, LANES), F32)
    for _ in range(min(SLC_TOP, n_slc)):
        m = jnp.max(score, axis=-1, keepdims=True)
        first = jnp.min(jnp.where(score == m, jf, float(LANES)), axis=-1, keepdims=True)
        pick = jf == first
        sel = jnp.where(pick, 1.0, sel)
        score = jnp.where(pick, -jnp.inf, score)
    sel_ref[0, 0] = sel.astype(sel_ref.dtype)


def select_blocks(imp, tq=1024):
    B, G, S, _ = imp.shape
    n_slc = S // SLC_LEN
    assert n_slc <= LANES
    tq = _pick(S, tq)
    spec = pl.BlockSpec((1, 1, tq, LANES), lambda b, g, i: (b, g, i, 0))
    return pl.pallas_call(
        functools.partial(_topk_body, tq=tq, n_slc=n_slc),
        grid=(B, G, S // tq),
        in_specs=[spec],
        out_specs=spec,
        out_shape=jax.ShapeDtypeStruct((B, G, S, LANES), BF16),
        compiler_params=_cparams("parallel", "parallel", "parallel"),
        name="nsa_topk",
    )(imp)


def _gqa_body(*refs, tq, tk, rep, mode):
    if mode == "sel":
        q_ref, k_ref, v_ref, sel_ref, exp_ref, o_ref, s_ref, acc_ref, bias_ref = refs
        picked = lax.dot_general(exp_ref[...], sel_ref[0, 0], _NT, preferred_element_type=F32)
        bias_ref[...] = (picked - 1.0) * (-NEG_INF)
    else:
        q_ref, k_ref, v_ref, o_ref, s_ref, acc_ref = refs
    q0 = pl.program_id(2) * tq
    q2 = _stack_heads(q_ref, rep)
    kpos = lax.broadcasted_iota(jnp.int32, (tk, tq), 0)
    t = q0 + lax.broadcasted_iota(jnp.int32, (tk, tq), 1)
    cols = [slice(r * tq, (r + 1) * tq) for r in range(rep)]

    def put_scores(slot, kj):
        k0 = pl.multiple_of(kj * tk, tk)
        s_ref[slot] = lax.dot_general(k_ref[0, pl.ds(k0, tk), :], q2, _NT, preferred_element_type=F32)

    def tile(slot, kj, m, l, diagonal):
        k0 = pl.multiple_of(kj * tk, tk)
        kp = k0 + kpos
        if mode == "sel":
            bias = bias_ref[pl.ds(k0, tk), :]
            if diagonal:
                bias = jnp.where(kp <= t, bias, NEG_INF)
        else:
            ok = kp > t - WINDOW
            if diagonal:
                ok = ok & (kp <= t)
            bias = jnp.where(ok, 0.0, NEG_INF)
        ps, m_new, l_new, scale = [], [], [], []
        for r in range(rep):
            s = s_ref[slot, :, cols[r]]
            mr = jnp.maximum(m[r], jnp.max(s + bias, axis=0, keepdims=True))
            p = jnp.exp2((s - mr) + bias)
            a = jnp.exp2(m[r] - mr)
            ps.append(p.astype(BF16))
            m_new.append(mr)
            l_new.append(a * l[r] + jnp.sum(p, axis=0, keepdims=True))
            scale.append(a)
        pt = jnp.concatenate(ps, axis=1)
        pv = lax.dot_general(v_ref[0, pl.ds(k0, tk), :], pt, _TN, preferred_element_type=F32)
        acc_ref[...] = jnp.concatenate(scale, axis=1) * acc_ref[...] + pv
        return tuple(m_new), tuple(l_new)

    kd = q0 // tk
    lo = 0 if mode == "sel" else jnp.maximum(q0 - (WINDOW - 1), 0) // tk
    put_scores(0, kd)
    put_scores(1, lo)
    acc_ref[...] = jnp.zeros_like(acc_ref)
    stats = tile(0, kd, (jnp.full((1, tq), NEG_INF, F32),) * rep, (jnp.zeros((1, tq), F32),) * rep, True)

    def pair(i, stats):
        kj = lo + 2 * i
        put_scores(0, kj + 1)
        stats = tile(1, kj, *stats, False)
        put_scores(1, kj + 2)
        return tile(0, kj + 1, *stats, False)

    n_off = kd - lo
    stats = lax.fori_loop(0, n_off // 2, pair, stats)
    m, l = lax.cond(n_off % 2 == 1, lambda st: tile(1, kd - 1, *st, False), lambda st: st, stats)
    for r in range(rep):
        out = acc_ref[:, cols[r]] * (1.0 / l[r])
        o_ref[0, :, r * HEAD_DIM:(r + 1) * HEAD_DIM] = out.T.astype(o_ref.dtype)


def gqa_attention(q, k, v, v_col0, rep, mode, sel=None, tq=128, tk=512):
    B, S, _ = q.shape
    G = k.shape[2] // HEAD_DIM
    tq, tk = _pick(S, tq), _pick(S, tk)
    assert tk % tq == 0
    w = rep * HEAD_DIM
    in_specs = [pl.BlockSpec((1, tq, w), lambda b, g, i: (b, i, g)),
                pl.BlockSpec((1, S, HEAD_DIM), lambda b, g, i: (b, 0, g)),
                pl.BlockSpec((1, S, HEAD_DIM), lambda b, g, i: (b, 0, v_col0 + g))]
    args = [q, k, v]
    scratch = [pltpu.VMEM((2, tk, rep * tq), F32), pltpu.VMEM((HEAD_DIM, rep * tq), F32)]
    if mode == "sel":
        key = lax.broadcasted_iota(jnp.int32, (S, LANES), 0)
        blk = lax.broadcasted_iota(jnp.int32, (S, LANES), 1)
        expand01 = (key // SLC_LEN == blk).astype(BF16)
        in_specs += [pl.BlockSpec((1, 1, tq, LANES), lambda b, g, i: (b, g, i, 0)),
                     pl.BlockSpec((S, LANES), lambda b, g, i: (0, 0))]
        args += [sel, expand01]
        scratch += [pltpu.VMEM((S, tq), F32)]
    return pl.pallas_call(
        functools.partial(_gqa_body, tq=tq, tk=tk, rep=rep, mode=mode),
        grid=(B, G, S // tq),
        in_specs=in_specs,
        out_specs=pl.BlockSpec((1, tq, w), lambda b, g, i: (b, i, g)),
        out_shape=jax.ShapeDtypeStruct((B, S, G * w), BF16),
        scratch_shapes=scratch,
        compiler_params=_cparams("parallel", "parallel", "arbitrary"),
        name="nsa_" + mode,
    )(*args)


def _nsa_gate_body(gl_ref, oc_ref, os_ref, ow_ref, o_ref, *, n_heads):
    gl = gl_ref[...].astype(F32)
    gate = 1.0 / (1.0 + jnp.exp(-gl))
    ng = 3 * n_heads
    src = lax.broadcasted_iota(jnp.int32, (ng, n_heads * HEAD_DIM), 0)
    head = lax.broadcasted_iota(jnp.int32, (ng, n_heads * HEAD_DIM), 1) // HEAD_DIM
    out = None
    for c, ref in enumerate((oc_ref, os_ref, ow_ref)):
        spread01 = jnp.where(src == head * 3 + c, 1.0, 0.0).astype(BF16)
        term = _split_dot(gate, spread01) * ref[...].astype(F32)
        out = term if out is None else out + term
    o_ref[...] = out.astype(o_ref.dtype)


def nsa_gate(gl, oc, os_, ow, n_heads, tm=256):
    M, W = oc.shape
    tm = _pick(M, tm)
    row = pl.BlockSpec((tm, W), lambda i: (i, 0))
    return pl.pallas_call(
        functools.partial(_nsa_gate_body, n_heads=n_heads),
        grid=(M // tm,),
        in_specs=[pl.BlockSpec((tm, gl.shape[1]), lambda i: (i, 0)), row, row, row],
        out_specs=row,
        out_shape=jax.ShapeDtypeStruct((M, W), BF16),
        compiler_params=_cparams("parallel"),
        name="nsa_gate",
    )(gl, oc, os_, ow)


def _xattn_body(q_ref, kv_ref, o_ref, *, n_heads, scale):
    w = n_heads * HEAD_DIM
    for h in range(n_heads):
        sl = slice(h * HEAD_DIM, (h + 1) * HEAD_DIM)
        k = kv_ref[0, :, sl]
        v = kv_ref[0, :, w + h * HEAD_DIM:w + (h + 1) * HEAD_DIM]
        s = lax.dot_general(q_ref[0, :, sl], k, _NT, preferred_element_type=F32) * scale
        p = jnp.exp(s - jnp.max(s, axis=-1, keepdims=True))
        p = p / jnp.sum(p, axis=-1, keepdims=True)
        o_ref[0, :, sl] = jnp.dot(p.astype(BF16), v, preferred_element_type=F32).astype(o_ref.dtype)


def mem_attention(q, kv, n_heads, tq=512):
    B, S, w = q.shape
    n_mem = kv.shape[1]
    tq = _pick(S, tq)
    return pl.pallas_call(
        functools.partial(_xattn_body, n_heads=n_heads, scale=HEAD_DIM ** -0.5),
        grid=(B, S // tq),
        in_specs=[pl.BlockSpec((1, tq, w), lambda b, i: (b, i, 0)),
                  pl.BlockSpec((1, n_mem, 2 * w), lambda b, i: (b, 0, 0))],
        out_specs=pl.BlockSpec((1, tq, w), lambda b, i: (b, i, 0)),
        out_shape=jax.ShapeDtypeStruct((B, S, w), BF16),
        compiler_params=_cparams("parallel", "parallel"),
        name="mem_attention",
    )(q, kv)


def _ffn_up_body(x_ref, wa_ref, wu_ref, cw_ref, o_ref, tail_ref, *, tiles_per_seq):
    x = x_ref[...]
    a = jnp.dot(x, wa_ref[...], preferred_element_type=F32)
    tm = a.shape[0]
    j = pl.program_id(1)
    first = (pl.program_id(0) % tiles_per_seq) == 0
    prev = jnp.where(first, 0.0, tail_ref[j])
    tail_ref[j] = a[tm - 8:]
    rowi = lax.broadcasted_iota(jnp.int32, a.shape, 0)
    a1 = jnp.where(rowi >= 1, pltpu.roll(a, 1, 0), prev[7:8])
    a2 = jnp.where(rowi >= 2, pltpu.roll(a, 2, 0), jnp.where(rowi == 1, prev[7:8], prev[6:7]))
    cw = cw_ref[...]
    gate = _gelu(cw[2:3] * a + cw[1:2] * a1 + cw[0:1] * a2)
    u = jnp.dot(x, wu_ref[...], preferred_element_type=F32)
    o_ref[...] = (gate * u).astype(o_ref.dtype)


def ffn_up_glu(x, w_up, conv_w, seq_len, tm=1024, tn=256):
    M, K = x.shape
    Fd = w_up.shape[1] // 2
    tm, tn = _pick(seq_len, tm), _pick(Fd, tn)
    nj = Fd // tn
    return pl.pallas_call(
        functools.partial(_ffn_up_body, tiles_per_seq=seq_len // tm),
        grid=(M // tm, nj),
        in_specs=[pl.BlockSpec((tm, K), lambda i, j: (i, 0)),
                  pl.BlockSpec((K, tn), lambda i, j: (0, j)),
                  pl.BlockSpec((K, tn), lambda i, j: (0, j + nj)),
                  pl.BlockSpec((CONV_W, tn), lambda i, j: (0, j))],
        out_specs=pl.BlockSpec((tm, tn), lambda i, j: (i, j)),
        out_shape=jax.ShapeDtypeStruct((M, Fd), BF16),
        scratch_shapes=[pltpu.VMEM((nj, 8, tn), F32)],
        compiler_params=_cparams("arbitrary", "arbitrary"),
        name="ffn_up_glu",
    )(x, w_up, w_up, conv_w.astype(F32))


def _rope_tables(S):
    half = HEAD_DIM // 2
    inv_freq = ROPE_THETA ** (-jnp.arange(half, dtype=F32) / half)
    ang = jnp.arange(S, dtype=F32)[:, None] * inv_freq[None, :]
    cos, sin = jnp.cos(ang), jnp.sin(ang)
    return jnp.concatenate([cos, cos], axis=-1), jnp.concatenate([-sin, sin], axis=-1)


def _hgrn_sb_mixer(hb, B, S, w_in, lb_raw, norm_w, w_out, e):
    width = w_out.shape[0]
    a_heads = width // (2 * HEAD_DIM)
    b_heads = a_heads
    proj = matmul(hb, w_in.astype(BF16)).reshape(B, S, -1)
    o_a = hgrn2(proj, lb_raw, norm_w, a_heads, e)
    o_b = stick_breaking(proj, 4 * a_heads, 4 * a_heads + b_heads, 4 * a_heads + 2 * b_heads, b_heads)
    o = jnp.concatenate([o_a, o_b], axis=-1).reshape(B * S, width)
    return matmul(o, w_out.astype(BF16))


def _nsa_mixer(hb, B, S, w_in, cmp_pos, cmp_w1, cmp_w2, w_out, cos, sin):
    G = NSA_KV_HEADS
    q_w = w_out.shape[0]
    n_heads = q_w // HEAD_DIM
    rep = n_heads // G
    kv_w = G * HEAD_DIM
    main_w = q_w + 6 * kv_w
    w_in = w_in.astype(BF16)
    proj = matmul(hb, w_in[:, :main_w]).reshape(B, S, main_w)
    gl = matmul(hb, w_in[:, main_w:])

    log2_scale = HEAD_DIM ** -0.5 * LOG2E
    q_rot = rope(proj, 0, n_heads, cos, sin, mult=log2_scale)
    ks_rot = rope(proj, q_w + 2 * kv_w, G, cos, sin)
    kw_rot = rope(proj, q_w + 4 * kv_w, G, cos, sin)

    n16 = S // CMP_STRIDE
    kvc_in = proj[:, :, q_w:q_w + 2 * kv_w].reshape(B, S, 2, G, HEAD_DIM)
    x16 = kvc_in.transpose(2, 0, 3, 1, 4).reshape(2, B, G, n16, CMP_STRIDE * HEAD_DIM)
    pe = jnp.broadcast_to(cmp_pos.reshape(2, 1, CMP_LEN * HEAD_DIM), (2, 8, CMP_LEN * HEAD_DIM)).astype(BF16)
    w1 = cmp_w1.reshape(2, CMP_LEN * HEAD_DIM, HEAD_DIM).astype(BF16)
    kvc = compress(x16, pe, w1, cmp_w2.astype(BF16), log2_scale)

    o_c, imp = cmp_attention(proj, kvc[0], kvc[1], rep)
    sel = select_blocks(imp)
    col = lambda off: (q_w + off * kv_w) // HEAD_DIM
    o_s = gqa_attention(q_rot, ks_rot, proj, col(3), rep, "sel", sel=sel, tq=128, tk=512)
    o_w = gqa_attention(q_rot, kw_rot, proj, col(5), rep, "win", tq=256, tk=256)
    o = nsa_gate(gl, o_c.reshape(B * S, q_w), o_s.reshape(B * S, q_w), o_w.reshape(B * S, q_w), n_heads)
    return matmul(o, w_out.astype(BF16))


def kernel(x, mem, ab_w_in, hgrn_lb, hgrn_norm_w, ab_w_out, nsa_w_in, nsa_cmp_pos, nsa_cmp_w1,
           nsa_cmp_w2, nsa_w_out, xa_w_q, xa_w_kv, xa_w_o, ffn_w_up, ffn_conv, ffn_w_down, ln_g, ln_b):
    B, S, D = x.shape
    depth = ln_g.shape[0]
    alpha = (2 * depth) ** 0.25
    n_mem = mem.shape[1]
    cos, sin = _rope_tables(S)
    h = x.reshape(B * S, D).astype(F32)
    hb = h.astype(BF16)
    memb = mem.reshape(B * n_mem, D).astype(BF16)
    for layer in range(depth):
        if layer % 2 == 0:
            e = layer // 2
            mix = _hgrn_sb_mixer(hb, B, S, ab_w_in[e], hgrn_lb, hgrn_norm_w[e], ab_w_out[e], e)
        else:
            o = layer // 2
            mix = _nsa_mixer(hb, B, S, nsa_w_in[o], nsa_cmp_pos[o], nsa_cmp_w1[o], nsa_cmp_w2[o],
                             nsa_w_out[o], cos, sin)
        h, hb = add_layer_norm(h, mix, ln_g[layer, 0], ln_b[layer, 0], alpha)

        xq = matmul(hb, xa_w_q[layer].astype(BF16)).reshape(B, S, -1)
        xkv = matmul(memb, xa_w_kv[layer].astype(BF16)).reshape(B, n_mem, -1)
        xo = mem_attention(xq, xkv, XA_HEADS).reshape(B * S, -1)
        h, hb = add_layer_norm(h, matmul(xo, xa_w_o[layer].astype(BF16)), ln_g[layer, 1], ln_b[layer, 1], alpha)

        gated = ffn_up_glu(hb, ffn_w_up[layer].astype(BF16), ffn_conv[layer], S)
        h, hb = add_layer_norm(h, matmul(gated, ffn_w_down[layer].astype(BF16), tk=5504),
                               ln_g[layer, 2], ln_b[layer, 2], alpha)
    return h.reshape(B, S, D).astype(x.dtype)
```

```python
import functools

import jax
import jax.numpy as jnp
from jax import lax
from jax.experimental import pallas as pl
from jax.experimental.pallas import tpu as pltpu

F32 = jnp.float32
BF16 = jnp.bfloat16

HEAD_DIM = 128
LANES = 128
HGRN_SUB = 16
HGRN_SAFE_LOG_DECAY = -60.0
NSA_KV_HEADS = 4
CMP_LEN = 32
CMP_STRIDE = 16
SLC_LEN = 64
SLC_TOP = 16
WINDOW = 512
XA_HEADS = 4
CONV_W = 3
ROPE_THETA = 10000.0
LN_EPS = 1e-5
RMS_EPS = 1e-6
NEG_INF = -1e30
FORCE_SCORE = 1e9
EXP_ZERO_BELOW = -104.0
LOG2E = 1.4426950408889634
VMEM_LIMIT = 52 * 1024 * 1024

_NT = (((1,), (1,)), ((), ()))
_TN = (((0,), (0,)), ((), ()))


def _cparams(*sem):
    return pltpu.CompilerParams(dimension_semantics=sem, vmem_limit_bytes=VMEM_LIMIT)


def _split_dot(a, b01):
    hi = a.astype(BF16)
    lo = (a - hi.astype(F32)).astype(BF16)
    return (jnp.dot(hi, b01, preferred_element_type=F32)
            + jnp.dot(lo, b01, preferred_element_type=F32))


def _mm_body(x_ref, w_ref, o_ref, *scratch, nk):
    prod = jnp.dot(x_ref[...], w_ref[...], preferred_element_type=F32)
    if nk == 1:
        o_ref[...] = prod.astype(o_ref.dtype)
        return
    acc_ref, = scratch
    k = pl.program_id(2)

    @pl.when(k == 0)
    def _():
        acc_ref[...] = prod

    @pl.when(k > 0)
    def _():
        acc_ref[...] += prod

    @pl.when(k == nk - 1)
    def _():
        o_ref[...] = acc_ref[...].astype(o_ref.dtype)


def _pick(n, pref):
    if n <= pref:
        return n
    t = pref
    while t >= LANES:
        if n % t == 0:
            return t
        t -= LANES
    return n


def matmul(x, w, out_dtype=BF16, tm=1024, tn=512, tk=4096, col0=0, n_cols=None):
    M, K = x.shape
    N = w.shape[1] if n_cols is None else n_cols
    tm, tn, tk = _pick(M, tm), _pick(N, tn), _pick(K, tk)
    assert col0 % tn == 0
    cb = col0 // tn
    nk = K // tk
    scratch = [] if nk == 1 else [pltpu.VMEM((tm, tn), F32)]
    return pl.pallas_call(
        functools.partial(_mm_body, nk=nk),
        grid=(M // tm, N // tn, nk),
        in_specs=[pl.BlockSpec((tm, tk), lambda i, j, k: (i, k)),
                  pl.BlockSpec((tk, tn), lambda i, j, k: (k, j + cb))],
        out_specs=pl.BlockSpec((tm, tn), lambda i, j, k: (i, j)),
        out_shape=jax.ShapeDtypeStruct((M, N), out_dtype),
        scratch_shapes=scratch,
        compiler_params=_cparams("parallel", "parallel", "arbitrary"),
        name="matmul",
    )(x, w)


def _add_ln_body(h_ref, m_ref, g_ref, b_ref, o32_ref, o16_ref, *, alpha):
    y = alpha * h_ref[...] + m_ref[...].astype(F32)
    mu = jnp.mean(y, axis=-1, keepdims=True)
    d = y - mu
    var = jnp.mean(d * d, axis=-1, keepdims=True)
    out = d * lax.rsqrt(var + LN_EPS) * g_ref[...] + b_ref[...]
    o32_ref[...] = out
    o16_ref[...] = out.astype(BF16)


def add_layer_norm(h, mix, g, b, alpha, tm=256):
    M, D = h.shape
    tm = _pick(M, tm)
    row = pl.BlockSpec((tm, D), lambda i: (i, 0))
    vec = pl.BlockSpec((1, D), lambda i: (0, 0))
    return pl.pallas_call(
        functools.partial(_add_ln_body, alpha=alpha),
        grid=(M // tm,),
        in_specs=[row, row, vec, vec],
        out_specs=[row, row],
        out_shape=[jax.ShapeDtypeStruct((M, D), F32), jax.ShapeDtypeStruct((M, D), BF16)],
        compiler_params=_cparams("parallel"),
        name="add_layer_norm",
    )(h, mix, g.reshape(1, D).astype(F32), b.reshape(1, D).astype(F32))


def _hgrn_body(q_ref, f_ref, i_ref, g_ref, lb_ref, nw_ref, o_ref, st_ref, *, ts, layer_idx, nh):
    C = HGRN_SUB
    nsub = ts // C
    heads = [slice(h * HEAD_DIM, (h + 1) * HEAD_DIM) for h in range(nh)]

    @pl.when(pl.program_id(2) == 0)
    def _():
        st_ref[...] = jnp.zeros_like(st_ref)

    lbr = lb_ref[...]
    ex = jnp.exp(lbr - jnp.max(lbr, axis=0, keepdims=True))
    sm = ex / jnp.sum(ex, axis=0, keepdims=True)
    lb = jnp.sum(sm[:layer_idx + 1], axis=0, keepdims=True)

    q = q_ref[0].astype(F32)
    z = f_ref[0].astype(F32)
    v = i_ref[0]
    e = jnp.exp(-jnp.abs(z))
    r = 1.0 / (1.0 + e)
    pos = z >= 0
    sig = jnp.where(pos, r, e * r)
    nsig = jnp.where(pos, e * r, r)
    logf = jnp.log(lb + (1.0 - lb) * sig)
    k = (1.0 - lb) * nsig

    hi = logf.astype(BF16)
    lo = (logf - hi.astype(F32)).astype(BF16)
    row = lax.broadcasted_iota(jnp.int32, (ts, ts), 0)
    col = lax.broadcasted_iota(jnp.int32, (ts, ts), 1)

    def cumdot(m01):
        return jnp.dot(m01, hi, preferred_element_type=F32) + jnp.dot(m01, lo, preferred_element_type=F32)

    def whole_tile(sts):
        b = cumdot(jnp.where(row >= col, 1.0, 0.0).astype(BF16))
        bl = b[ts - 1:ts]
        qd = (q * jnp.exp(b)).astype(BF16)
        kinv = (k * jnp.exp(-b)).astype(BF16)
        kd = (k * jnp.exp(bl - b)).astype(BF16)
        dec = jnp.exp(bl)
        outs, new = [], []
        for h, sl in enumerate(heads):
            dmat = lax.dot_general(qd[:, sl], kinv[:, sl], _NT, preferred_element_type=F32)
            dmat = jnp.where(row >= col, dmat, 0.0)
            o = jnp.dot(dmat.astype(BF16), v[:, sl], preferred_element_type=F32)
            outs.append(o + lax.dot_general(qd[:, sl], sts[h].astype(BF16), _NT, preferred_element_type=F32))
            new.append(sts[h] * dec[:, sl] + lax.dot_general(v[:, sl], kd[:, sl], _TN, preferred_element_type=F32))
        return tuple(outs), tuple(new)

    def sub_chunks(sts):
        same = (row // C) == (col // C)
        b = cumdot(jnp.where(same & (row >= col), 1.0, 0.0).astype(BF16))
        bl = cumdot(jnp.where(same, 1.0, 0.0).astype(BF16))
        qd = (q * jnp.exp(b)).astype(BF16)
        kd = (k * jnp.exp(bl - b)).astype(BF16)
        dec = jnp.exp(bl)
        tri = (lax.broadcasted_iota(jnp.int32, (C, C, HEAD_DIM), 0)
               >= lax.broadcasted_iota(jnp.int32, (C, C, HEAD_DIM), 1))
        outs, new = [], []
        for h, hs in enumerate(heads):
            st = sts[h]
            parts = []
            for n in range(nsub):
                sl = slice(n * C, (n + 1) * C)
                bn, qn, kn = b[sl, hs], q[sl, hs], k[sl, hs]
                diff = bn[:, None, :] - bn[None, :, :]
                ee = jnp.exp(jnp.where(tri, diff, NEG_INF))
                dmat = jnp.sum(qn[:, None, :] * (kn[None, :, :] * ee), axis=-1)
                o_n = jnp.dot(dmat.astype(BF16), v[sl, hs], preferred_element_type=F32)
                o_n = o_n + lax.dot_general(qd[sl, hs], st.astype(BF16), _NT, preferred_element_type=F32)
                upd = lax.dot_general(v[sl, hs], kd[sl, hs], _TN, preferred_element_type=F32)
                st = st * dec[n * C:n * C + 1, hs] + upd
                parts.append(o_n)
            outs.append(jnp.concatenate(parts, axis=0))
            new.append(st)
        return tuple(outs), tuple(new)

    tile_decay = jnp.min(jnp.sum(logf, axis=0, keepdims=True))
    outs, sts = lax.cond(tile_decay > HGRN_SAFE_LOG_DECAY, whole_tile, sub_chunks,
                         tuple(st_ref[h] for h in range(nh)))
    gt = g_ref[0].astype(F32)
    gate = gt / (1.0 + jnp.exp(-gt))
    for h, sl in enumerate(heads):
        st_ref[h] = sts[h]
        o = outs[h]
        o = o * lax.rsqrt(jnp.mean(o * o, axis=-1, keepdims=True) + RMS_EPS) * nw_ref[...] * gate[:, sl]
        o_ref[0, :, sl] = o.astype(o_ref.dtype)


def hgrn2(proj, lb_raw, norm_w, n_heads, layer_idx, ts=128, nh=4):
    B, S, _ = proj.shape
    ts = _pick(S, ts)
    assert n_heads % nh == 0
    H = n_heads // nh
    L = lb_raw.shape[0]
    w = nh * HEAD_DIM

    def col(off):
        return pl.BlockSpec((1, ts, w), lambda b, h, s: (b, s, off * H + h))

    return pl.pallas_call(
        functools.partial(_hgrn_body, ts=ts, layer_idx=layer_idx, nh=nh),
        grid=(B, H, S // ts),
        in_specs=[col(0), col(1), col(2), col(3),
                  pl.BlockSpec((L, w), lambda b, h, s: (0, h)),
                  pl.BlockSpec((1, HEAD_DIM), lambda b, h, s: (0, 0))],
        out_specs=pl.BlockSpec((1, ts, w), lambda b, h, s: (b, s, h)),
        out_shape=jax.ShapeDtypeStruct((B, S, n_heads * HEAD_DIM), BF16),
        scratch_shapes=[pltpu.VMEM((nh, HEAD_DIM, HEAD_DIM), F32)],
        compiler_params=_cparams("parallel", "parallel", "arbitrary"),
        name="hgrn2",
    )(proj, proj, proj, proj, lb_raw.astype(F32), norm_w.reshape(1, HEAD_DIM).astype(F32))


def _sb_block(q, k, v, carry, after01, scale, mask):
    z = lax.dot_general(q, k, _NT, preferred_element_type=F32) * scale
    sp = jnp.maximum(z, 0.0) + jnp.log(1.0 + jnp.exp(-jnp.abs(z)))
    lm = -sp
    if mask is not None:
        lm = jnp.where(mask, lm, 0.0)
    rev = _split_dot(lm, after01)
    w = jnp.exp(z - sp + rev + carry)
    if mask is not None:
        w = jnp.where(mask, w, 0.0)
    contrib = jnp.dot(w.astype(BF16), v, preferred_element_type=F32)
    return contrib, carry + rev[:, 0:1] + lm[:, 0:1]


def _sb_body(q_ref, k_ref, v_ref, o_ref, *, tq, scale, nh):
    qi = pl.program_id(2)
    row = lax.broadcasted_iota(jnp.int32, (tq, tq), 0)
    col = lax.broadcasted_iota(jnp.int32, (tq, tq), 1)
    after01 = jnp.where(row > col, 1.0, 0.0).astype(BF16)
    heads = [slice(h * HEAD_DIM, (h + 1) * HEAD_DIM) for h in range(nh)]
    qs = [q_ref[0, :, sl] for sl in heads]

    def blocks(k0, carries, mask):
        res = [_sb_block(qs[h], k_ref[0, pl.ds(k0, tq), heads[h]], v_ref[0, pl.ds(k0, tq), heads[h]],
                         carries[h], after01, scale, mask) for h in range(nh)]
        return tuple(r[0] for r in res), tuple(r[1] for r in res)

    accs, carries = blocks(pl.multiple_of(qi * tq, tq), (jnp.zeros((tq, 1), F32),) * nh, col < row)

    def cond(c):
        j, _, carries = c
        top = functools.reduce(jnp.maximum, [jnp.max(cr) for cr in carries])
        return jnp.logical_and(j >= 0, top > EXP_ZERO_BELOW)

    def body(c):
        j, accs, carries = c
        contribs, carries = blocks(pl.multiple_of(j * tq, tq), carries, None)
        return j - 1, tuple(a + cb for a, cb in zip(accs, contribs)), carries

    _, accs, _ = lax.while_loop(cond, body, (qi - 1, accs, carries))
    for h in range(nh):
        o_ref[0, :, heads[h]] = accs[h].astype(o_ref.dtype)


def stick_breaking(proj, col_q, col_k, col_v, n_heads, tq=256, nh=2):
    B, S, _ = proj.shape
    tq = _pick(S, tq)
    assert n_heads % nh == 0 and col_q % nh == 0 and col_k % nh == 0 and col_v % nh == 0
    w = nh * HEAD_DIM
    full = lambda off: pl.BlockSpec((1, S, w), lambda b, h, i: (b, 0, off // nh + h))
    return pl.pallas_call(
        functools.partial(_sb_body, tq=tq, scale=HEAD_DIM ** -0.5, nh=nh),
        grid=(B, n_heads // nh, S // tq),
        in_specs=[pl.BlockSpec((1, tq, w), lambda b, h, i: (b, i, col_q // nh + h)),
                  full(col_k), full(col_v)],
        out_specs=pl.BlockSpec((1, tq, w), lambda b, h, i: (b, i, h)),
        out_shape=jax.ShapeDtypeStruct((B, S, n_heads * HEAD_DIM), BF16),
        compiler_params=_cparams("parallel", "parallel", "arbitrary"),
        name="stick_breaking",
    )(proj, proj, proj)


def _rope_body(x_ref, cos_ref, sin_ref, o_ref, *, n_heads, mult):
    cos = cos_ref[...] * mult
    sin = sin_ref[...] * mult
    for h in range(n_heads):
        sl = slice(h * HEAD_DIM, (h + 1) * HEAD_DIM)
        t = x_ref[0, :, sl].astype(F32)
        o_ref[0, :, sl] = (t * cos + pltpu.roll(t, HEAD_DIM // 2, 1) * sin).astype(o_ref.dtype)


def rope(x, col0, n_heads, cos, sin, mult=1.0, ts=256):
    B, S, _ = x.shape
    ts = _pick(S, ts)
    w = n_heads * HEAD_DIM
    assert col0 % w == 0
    cb = col0 // w
    tab = pl.BlockSpec((ts, HEAD_DIM), lambda b, s: (s, 0))
    return pl.pallas_call(
        functools.partial(_rope_body, n_heads=n_heads, mult=mult),
        grid=(B, S // ts),
        in_specs=[pl.BlockSpec((1, ts, w), lambda b, s: (b, s, cb)), tab, tab],
        out_specs=pl.BlockSpec((1, ts, w), lambda b, s: (b, s, 0)),
        out_shape=jax.ShapeDtypeStruct((B, S, w), BF16),
        compiler_params=_cparams("parallel", "parallel"),
        name="rope",
    )(x, cos, sin)


def _gelu(x):
    return 0.5 * x * (1.0 + lax.erf(x * (2.0 ** -0.5)))


def _compress_body(x_ref, pe_ref, w1_ref, w2_ref, o_ref, *, batch, k_mult):
    x = x_ref[0, 0]
    half = x.shape[1]
    n16 = x.shape[0]
    y1 = jnp.dot(x, w1_ref[0, :half], preferred_element_type=F32)
    y2 = jnp.dot(x, w1_ref[0, half:], preferred_element_type=F32)
    bias = jnp.dot(pe_ref[0], w1_ref[0], preferred_element_type=F32)[0:1]
    hid = _gelu(y1 + pltpu.roll(y2, n16 - 1, 0) + bias)
    out = jnp.dot(hid.astype(BF16), w2_ref[0], preferred_element_type=F32)
    mult = jnp.where(pl.program_id(0) < batch, k_mult, 1.0)
    o_ref[0, 0] = (out * mult).astype(o_ref.dtype)


def compress(x16, pe, w1, w2, k_mult):
    two, B, G, n16, wide = x16.shape
    x16 = x16.reshape(two * B, G, n16, wide)
    out = pl.pallas_call(
        functools.partial(_compress_body, batch=B, k_mult=k_mult),
        grid=(two * B, G),
        in_specs=[pl.BlockSpec((1, 1, n16, wide), lambda i, g: (i, g, 0, 0)),
                  pl.BlockSpec((1, 8, 2 * wide), lambda i, g: (i // B, 0, 0)),
                  pl.BlockSpec((1, 2 * wide, HEAD_DIM), lambda i, g: (i // B, 0, 0)),
                  pl.BlockSpec((1, HEAD_DIM, HEAD_DIM), lambda i, g: (i // B, 0, 0))],
        out_specs=pl.BlockSpec((1, 1, n16, HEAD_DIM), lambda i, g: (i, g, 0, 0)),
        out_shape=jax.ShapeDtypeStruct((two * B, G, n16, HEAD_DIM), BF16),
        compiler_params=_cparams("parallel", "parallel"),
        name="nsa_compress",
    )(x16, pe, w1, w2)
    return out.reshape(two, B, G, n16, HEAD_DIM)


def _stack_heads(q_ref, rep):
    return jnp.concatenate([q_ref[0, :, r * HEAD_DIM:(r + 1) * HEAD_DIM] for r in range(rep)], axis=0)


def _cmp_body(q_ref, kc_ref, vc_ref, o_ref, imp_ref, *, tq, rep):
    q0 = pl.program_id(2) * tq
    kc = kc_ref[0, 0]
    n16 = kc.shape[0]
    rows = rep * tq
    t = q0 + lax.broadcasted_iota(jnp.int32, (tq, n16), 0)
    n = lax.broadcasted_iota(jnp.int32, (tq, n16), 1)
    bias = jnp.where(n * CMP_STRIDE + (CMP_LEN - 1) <= t, 0.0, NEG_INF)
    s = lax.dot_general(_stack_heads(q_ref, rep), kc, _NT, preferred_element_type=F32)
    s3 = s.reshape(rep, tq, n16) + bias[None]
    m = jnp.maximum(jnp.max(s3, axis=-1, keepdims=True), 0.1 * NEG_INF)
    p = jnp.exp2(s3 - m)
    den = jnp.sum(p, axis=-1, keepdims=True)
    pn = p * (1.0 / jnp.where(den > 0.0, den, 1.0))
    o = jnp.dot(pn.reshape(rows, n16).astype(BF16), vc_ref[0, 0], preferred_element_type=F32)
    for r in range(rep):
        o_ref[0, :, r * HEAD_DIM:(r + 1) * HEAD_DIM] = o[r * tq:(r + 1) * tq].astype(o_ref.dtype)

    cn = lax.broadcasted_iota(jnp.int32, (n16, LANES), 0) * CMP_STRIDE
    cj = lax.broadcasted_iota(jnp.int32, (n16, LANES), 1) * SLC_LEN
    ov01 = jnp.where((cn < cj + SLC_LEN) & (cn + CMP_LEN > cj), 1.0, 0.0).astype(BF16)
    imp_ref[0, 0] = _split_dot(jnp.sum(pn, axis=0), ov01)


def cmp_attention(q, kc, vc, rep, tq=128):
    B, S, _ = q.shape
    G, n16 = kc.shape[1], kc.shape[2]
    tq = _pick(S, tq)
    w = rep * HEAD_DIM
    kv = pl.BlockSpec((1, 1, n16, HEAD_DIM), lambda b, g, i: (b, g, 0, 0))
    return pl.pallas_call(
        functools.partial(_cmp_body, tq=tq, rep=rep),
        grid=(B, G, S // tq),
        in_specs=[pl.BlockSpec((1, tq, w), lambda b, g, i: (b, i, g)), kv, kv],
        out_specs=[pl.BlockSpec((1, tq, w), lambda b, g, i: (b, i, g)),
                   pl.BlockSpec((1, 1, tq, LANES), lambda b, g, i: (b, g, i, 0))],
        out_shape=[jax.ShapeDtypeStruct((B, S, G * w), BF16),
                   jax.ShapeDtypeStruct((B, G, S, LANES), F32)],
        compiler_params=_cparams("parallel", "parallel", "parallel"),
        name="nsa_cmp",
    )(q, kc, vc)


def _topk_body(imp_ref, sel_ref, *, tq, n_slc):
    q0 = pl.program_id(2) * tq
    imp = imp_ref[0, 0]
    tt = q0 + lax.broadcasted_iota(jnp.int32, (tq, LANES), 0)
    j = lax.broadcasted_iota(jnp.int32, (tq, LANES), 1)
    cur = tt // SLC_LEN
    forced = (j == 0) | (j == cur) | (j == cur - 1)
    allowed = j * SLC_LEN <= tt
    score = jnp.where(forced, FORCE_SCORE, jnp.where(allowed, imp, -1.0))
    score = jnp.where(j < n_slc, score, -jnp.inf)
    jf = j.astype(F32)
    sel = jnp.zeros((tq, LANES), F32)
    for _ in range(min(SLC_TOP, n_slc)):
        m = jnp.max(score, axis=-1, keepdims=True)
        first = jnp.min(jnp.where(score == m, jf, float(LANES)), axis=-1, keepdims=True)
        pick = jf == first
        sel = jnp.where(pick, 1.0, sel)
        score = jnp.where(pick, -jnp.inf, score)
    sel_ref[0, 0] = sel.astype(sel_ref.dtype)


def select_blocks(imp, tq=1024):
    B, G, S, _ = imp.shape
    n_slc = S // SLC_LEN
    assert n_slc <= LANES
    tq = _pick(S, tq)
    spec = pl.BlockSpec((1, 1, tq, LANES), lambda b, g, i: (b, g, i, 0))
    return pl.pallas_call(
        functools.partial(_topk_body, tq=tq, n_slc=n_slc),
        grid=(B, G, S // tq),
        in_specs=[spec],
        out_specs=spec,
        out_shape=jax.ShapeDtypeStruct((B, G, S, LANES), BF16),
        compiler_params=_cparams("parallel", "parallel", "parallel"),
        name="nsa_topk",
    )(imp)


def _gqa_body(*refs, tq, tk, rep, mode):
    q0 = pl.program_id(2) * tq
    if mode == "sel":
        q_ref, k_ref, v_ref, sel_ref, o_ref, s_ref, acc_ref = refs
        key_blk = lax.broadcasted_iota(jnp.int32, (tk, LANES), 0) // SLC_LEN
        lane_blk = lax.broadcasted_iota(jnp.int32, (tk, LANES), 1)
        unpicked = ((sel_ref[0, 0].astype(F32) - 1.0) * (-NEG_INF)).astype(BF16)
        q2 = jnp.concatenate([_stack_heads(q_ref, rep), jnp.concatenate([unpicked] * rep, axis=0)], axis=1)
    else:
        q_ref, k_ref, v_ref, o_ref, s_ref, acc_ref = refs
        q2 = _stack_heads(q_ref, rep)
    kpos = lax.broadcasted_iota(jnp.int32, (tk, tq), 0)
    t = q0 + lax.broadcasted_iota(jnp.int32, (tk, tq), 1)
    cols = [slice(r * tq, (r + 1) * tq) for r in range(rep)]

    def put_scores(slot, kj):
        k0 = pl.multiple_of(kj * tk, tk)
        keys = k_ref[0, pl.ds(k0, tk), :]
        if mode == "sel":
            onehot = jnp.where(key_blk + kj * (tk // SLC_LEN) == lane_blk, 1.0, 0.0).astype(BF16)
            keys = jnp.concatenate([keys, onehot], axis=1)
        s_ref[slot] = lax.dot_general(keys, q2, _NT, preferred_element_type=F32)

    def tile(slot, kj, m, l, diagonal):
        k0 = pl.multiple_of(kj * tk, tk)
        kp = k0 + kpos
        if mode == "sel":
            bias = jnp.where(kp <= t, 0.0, NEG_INF) if diagonal else None
        else:
            ok = kp > t - WINDOW
            if diagonal:
                ok = ok & (kp <= t)
            bias = jnp.where(ok, 0.0, NEG_INF)
        ps, m_new, l_new, scale = [], [], [], []
        for r in range(rep):
            s = s_ref[slot, :, cols[r]]
            if bias is None:
                mr = jnp.maximum(m[r], jnp.max(s, axis=0, keepdims=True))
                p = jnp.exp2(s - mr)
            else:
                mr = jnp.maximum(m[r], jnp.max(s + bias, axis=0, keepdims=True))
                p = jnp.exp2((s - mr) + bias)
            a = jnp.exp2(m[r] - mr)
            ps.append(p.astype(BF16))
            m_new.append(mr)
            l_new.append(a * l[r] + jnp.sum(p, axis=0, keepdims=True))
            scale.append(a)
        pt = jnp.concatenate(ps, axis=1)
        pv = lax.dot_general(v_ref[0, pl.ds(k0, tk), :], pt, _TN, preferred_element_type=F32)
        acc_ref[...] = jnp.concatenate(scale, axis=1) * acc_ref[...] + pv
        return tuple(m_new), tuple(l_new)

    kd = q0 // tk
    lo = 0 if mode == "sel" else jnp.maximum(q0 - (WINDOW - 1), 0) // tk
    put_scores(0, kd)
    put_scores(1, lo)
    acc_ref[...] = jnp.zeros_like(acc_ref)
    stats = tile(0, kd, (jnp.full((1, tq), NEG_INF, F32),) * rep, (jnp.zeros((1, tq), F32),) * rep, True)

    def pair(i, stats):
        kj = lo + 2 * i
        put_scores(0, kj + 1)
        stats = tile(1, kj, *stats, False)
        put_scores(1, kj + 2)
        return tile(0, kj + 1, *stats, False)

    n_off = kd - lo
    stats = lax.fori_loop(0, n_off // 2, pair, stats)
    m, l = lax.cond(n_off % 2 == 1, lambda st: tile(1, kd - 1, *st, False), lambda st: st, stats)
    for r in range(rep):
        out = acc_ref[:, cols[r]] * (1.0 / l[r])
        o_ref[0, :, r * HEAD_DIM:(r + 1) * HEAD_DIM] = out.T.astype(o_ref.dtype)


def gqa_attention(q, k, v, v_col0, rep, mode, sel=None, tq=128, tk=512):
    B, S, _ = q.shape
    G = k.shape[2] // HEAD_DIM
    tq, tk = _pick(S, tq), _pick(S, tk)
    assert tk % tq == 0
    w = rep * HEAD_DIM
    in_specs = [pl.BlockSpec((1, tq, w), lambda b, g, i: (b, i, g)),
                pl.BlockSpec((1, S, HEAD_DIM), lambda b, g, i: (b, 0, g)),
                pl.BlockSpec((1, S, HEAD_DIM), lambda b, g, i: (b, 0, v_col0 + g))]
    args = [q, k, v]
    scratch = [pltpu.VMEM((2, tk, rep * tq), F32), pltpu.VMEM((HEAD_DIM, rep * tq), F32)]
    if mode == "sel":
        assert S // SLC_LEN <= LANES and tk % SLC_LEN == 0
        in_specs += [pl.BlockSpec((1, 1, tq, LANES), lambda b, g, i: (b, g, i, 0))]
        args += [sel]
    return pl.pallas_call(
        functools.partial(_gqa_body, tq=tq, tk=tk, rep=rep, mode=mode),
        grid=(B, G, S // tq),
        in_specs=in_specs,
        out_specs=pl.BlockSpec((1, tq, w), lambda b, g, i: (b, i, g)),
        out_shape=jax.ShapeDtypeStruct((B, S, G * w), BF16),
        scratch_shapes=scratch,
        compiler_params=_cparams("parallel", "parallel", "arbitrary"),
        name="nsa_" + mode,
    )(*args)


def _nsa_gate_body(gl_ref, oc_ref, os_ref, ow_ref, o_ref, *, n_heads):
    ng = 3 * n_heads
    gl = gl_ref[:, :ng].astype(F32)
    gate = 1.0 / (1.0 + jnp.exp(-gl))
    src = lax.broadcasted_iota(jnp.int32, (ng, n_heads * HEAD_DIM), 0)
    head = lax.broadcasted_iota(jnp.int32, (ng, n_heads * HEAD_DIM), 1) // HEAD_DIM
    out = None
    for c, ref in enumerate((oc_ref, os_ref, ow_ref)):
        spread01 = jnp.where(src == head * 3 + c, 1.0, 0.0).astype(BF16)
        term = _split_dot(gate, spread01) * ref[...].astype(F32)
        out = term if out is None else out + term
    o_ref[...] = out.astype(o_ref.dtype)


def nsa_gate(gl, oc, os_, ow, n_heads, tm=256):
    M, W = oc.shape
    tm = _pick(M, tm)
    row = pl.BlockSpec((tm, W), lambda i: (i, 0))
    return pl.pallas_call(
        functools.partial(_nsa_gate_body, n_heads=n_heads),
        grid=(M // tm,),
        in_specs=[pl.BlockSpec((tm, gl.shape[1]), lambda i: (i, 0)), row, row, row],
        out_specs=row,
        out_shape=jax.ShapeDtypeStruct((M, W), BF16),
        compiler_params=_cparams("parallel"),
        name="nsa_gate",
    )(gl, oc, os_, ow)


def _xattn_body(q_ref, kv_ref, o_ref, *, n_heads, scale):
    w = n_heads * HEAD_DIM
    for h in range(n_heads):
        sl = slice(h * HEAD_DIM, (h + 1) * HEAD_DIM)
        k = kv_ref[0, :, sl]
        v = kv_ref[0, :, w + h * HEAD_DIM:w + (h + 1) * HEAD_DIM]
        s = lax.dot_general(q_ref[0, :, sl], k, _NT, preferred_element_type=F32) * scale
        p = jnp.exp(s - jnp.max(s, axis=-1, keepdims=True))
        p = p / jnp.sum(p, axis=-1, keepdims=True)
        o_ref[0, :, sl] = jnp.dot(p.astype(BF16), v, preferred_element_type=F32).astype(o_ref.dtype)


def mem_attention(q, kv, n_heads, tq=512):
    B, S, w = q.shape
    n_mem = kv.shape[1]
    tq = _pick(S, tq)
    return pl.pallas_call(
        functools.partial(_xattn_body, n_heads=n_heads, scale=HEAD_DIM ** -0.5),
        grid=(B, S // tq),
        in_specs=[pl.BlockSpec((1, tq, w), lambda b, i: (b, i, 0)),
                  pl.BlockSpec((1, n_mem, 2 * w), lambda b, i: (b, 0, 0))],
        out_specs=pl.BlockSpec((1, tq, w), lambda b, i: (b, i, 0)),
        out_shape=jax.ShapeDtypeStruct((B, S, w), BF16),
        compiler_params=_cparams("parallel", "parallel"),
        name="mem_attention",
    )(q, kv)


def _ffn_up_body(x_ref, wa_ref, wu_ref, cw_ref, o_ref, tail_ref, *, tiles_per_seq):
    x = x_ref[...]
    a = jnp.dot(x, wa_ref[...], preferred_element_type=F32)
    tm = a.shape[0]
    j = pl.program_id(1)
    first = (pl.program_id(0) % tiles_per_seq) == 0
    prev = jnp.where(first, 0.0, tail_ref[j])
    tail_ref[j] = a[tm - 8:]
    rowi = lax.broadcasted_iota(jnp.int32, a.shape, 0)
    a1 = jnp.where(rowi >= 1, pltpu.roll(a, 1, 0), prev[7:8])
    a2 = jnp.where(rowi >= 2, pltpu.roll(a, 2, 0), jnp.where(rowi == 1, prev[7:8], prev[6:7]))
    cw = cw_ref[...]
    gate = _gelu(cw[2:3] * a + cw[1:2] * a1 + cw[0:1] * a2)
    u = jnp.dot(x, wu_ref[...], preferred_element_type=F32)
    o_ref[...] = (gate * u).astype(o_ref.dtype)


def ffn_up_glu(x, w_up, conv_w, seq_len, tm=1024, tn=256):
    M, K = x.shape
    Fd = w_up.shape[1] // 2
    tm, tn = _pick(seq_len, tm), _pick(Fd, tn)
    nj = Fd // tn
    return pl.pallas_call(
        functools.partial(_ffn_up_body, tiles_per_seq=seq_len // tm),
        grid=(M // tm, nj),
        in_specs=[pl.BlockSpec((tm, K), lambda i, j: (i, 0)),
                  pl.BlockSpec((K, tn), lambda i, j: (0, j)),
                  pl.BlockSpec((K, tn), lambda i, j: (0, j + nj)),
                  pl.BlockSpec((CONV_W, tn), lambda i, j: (0, j))],
        out_specs=pl.BlockSpec((tm, tn), lambda i, j: (i, j)),
        out_shape=jax.ShapeDtypeStruct((M, Fd), BF16),
        scratch_shapes=[pltpu.VMEM((nj, 8, tn), F32)],
        compiler_params=_cparams("arbitrary", "arbitrary"),
        name="ffn_up_glu",
    )(x, w_up, w_up, conv_w.astype(F32))


def _rope_tables(S):
    half = HEAD_DIM // 2
    inv_freq = ROPE_THETA ** (-jnp.arange(half, dtype=F32) / half)
    ang = jnp.arange(S, dtype=F32)[:, None] * inv_freq[None, :]
    cos, sin = jnp.cos(ang), jnp.sin(ang)
    return jnp.concatenate([cos, cos], axis=-1), jnp.concatenate([-sin, sin], axis=-1)


def _hgrn_sb_mixer(hb, B, S, w_in, lb_raw, norm_w, w_out, e):
    width = w_out.shape[0]
    a_heads = width // (2 * HEAD_DIM)
    b_heads = a_heads
    proj = matmul(hb, w_in.astype(BF16)).reshape(B, S, -1)
    o_a = hgrn2(proj, lb_raw, norm_w, a_heads, e)
    o_b = stick_breaking(proj, 4 * a_heads, 4 * a_heads + b_heads, 4 * a_heads + 2 * b_heads, b_heads)
    o = jnp.concatenate([o_a, o_b], axis=-1).reshape(B * S, width)
    return matmul(o, w_out.astype(BF16))


def _nsa_mixer(hb, B, S, w_in, cmp_pos, cmp_w1, cmp_w2, w_out, cos, sin):
    G = NSA_KV_HEADS
    q_w = w_out.shape[0]
    n_heads = q_w // HEAD_DIM
    rep = n_heads // G
    kv_w = G * HEAD_DIM
    main_w = q_w + 6 * kv_w
    w_in = w_in.astype(BF16)
    proj = matmul(hb, w_in, n_cols=main_w).reshape(B, S, main_w)
    gl = matmul(hb, w_in, col0=main_w, n_cols=LANES, tn=LANES)

    log2_scale = HEAD_DIM ** -0.5 * LOG2E
    q_rot = rope(proj, 0, n_heads, cos, sin, mult=log2_scale)
    ks_rot = rope(proj, q_w + 2 * kv_w, G, cos, sin)
    kw_rot = rope(proj, q_w + 4 * kv_w, G, cos, sin)

    n16 = S // CMP_STRIDE
    kvc_in = proj[:, :, q_w:q_w + 2 * kv_w].reshape(B, S, 2, G, HEAD_DIM)
    x16 = kvc_in.transpose(2, 0, 3, 1, 4).reshape(2, B, G, n16, CMP_STRIDE * HEAD_DIM)
    pe = jnp.broadcast_to(cmp_pos.reshape(2, 1, CMP_LEN * HEAD_DIM), (2, 8, CMP_LEN * HEAD_DIM)).astype(BF16)
    w1 = cmp_w1.reshape(2, CMP_LEN * HEAD_DIM, HEAD_DIM).astype(BF16)
    kvc = compress(x16, pe, w1, cmp_w2.astype(BF16), log2_scale)

    o_c, imp = cmp_attention(proj, kvc[0], kvc[1], rep)
    sel = select_blocks(imp)
    col = lambda off: (q_w + off * kv_w) // HEAD_DIM
    o_s = gqa_attention(q_rot, ks_rot, proj, col(3), rep, "sel", sel=sel, tq=128, tk=512)
    o_w = gqa_attention(q_rot, kw_rot, proj, col(5), rep, "win", tq=256, tk=256)
    o = nsa_gate(gl, o_c.reshape(B * S, q_w), o_s.reshape(B * S, q_w), o_w.reshape(B * S, q_w), n_heads)
    return matmul(o, w_out.astype(BF16))


def kernel(x, mem, ab_w_in, hgrn_lb, hgrn_norm_w, ab_w_out, nsa_w_in, nsa_cmp_pos, nsa_cmp_w1,
           nsa_cmp_w2, nsa_w_out, xa_w_q, xa_w_kv, xa_w_o, ffn_w_up, ffn_conv, ffn_w_down, ln_g, ln_b):
    B, S, D = x.shape
    depth = ln_g.shape[0]
    alpha = (2 * depth) ** 0.25
    n_mem = mem.shape[1]
    cos, sin = _rope_tables(S)
    h = x.reshape(B * S, D).astype(F32)
    hb = h.astype(BF16)
    memb = mem.reshape(B * n_mem, D).astype(BF16)
    for layer in range(depth):
        if layer % 2 == 0:
            e = layer // 2
            mix = _hgrn_sb_mixer(hb, B, S, ab_w_in[e], hgrn_lb, hgrn_norm_w[e], ab_w_out[e], e)
        else:
            o = layer // 2
            mix = _nsa_mixer(hb, B, S, nsa_w_in[o], nsa_cmp_pos[o], nsa_cmp_w1[o], nsa_cmp_w2[o],
                             nsa_w_out[o], cos, sin)
        h, hb = add_layer_norm(h, mix, ln_g[layer, 0], ln_b[layer, 0], alpha)

        xq = matmul(hb, xa_w_q[layer].astype(BF16)).reshape(B, S, -1)
        xkv = matmul(memb, xa_w_kv[layer].astype(BF16)).reshape(B, n_mem, -1)
        xo = mem_attention(xq, xkv, XA_HEADS).reshape(B * S, -1)
        h, hb = add_layer_norm(h, matmul(xo, xa_w_o[layer].astype(BF16)), ln_g[layer, 1], ln_b[layer, 1], alpha)

        gated = ffn_up_glu(hb, ffn_w_up[layer].astype(BF16), ffn_conv[layer], S)
        h, hb = add_layer_norm(h, matmul(gated, ffn_w_down[layer].astype(BF16), tk=5504),
                               ln_g[layer, 2], ln_b[layer, 2], alpha)
    return h.reshape(B, S, D).astype(x.dtype)
```

```python
import functools

import jax
import jax.numpy as jnp
from jax import lax
from jax.experimental import pallas as pl
from jax.experimental.pallas import tpu as pltpu

F32 = jnp.float32
BF16 = jnp.bfloat16

HEAD_DIM = 128
LANES = 128
HGRN_SUB = 16
HGRN_SAFE_LOG_DECAY = -60.0
NSA_KV_HEADS = 4
CMP_LEN = 32
CMP_STRIDE = 16
SLC_LEN = 64
SLC_TOP = 16
WINDOW = 512
XA_HEADS = 4
CONV_W = 3
ROPE_THETA = 10000.0
LN_EPS = 1e-5
RMS_EPS = 1e-6
NEG_INF = -1e30
FORCE_SCORE = 1e9
EXP_ZERO_BELOW = -104.0
SOFTPLUS_CLAMP = 80.0
LOG2E = 1.4426950408889634
VMEM_LIMIT = 52 * 1024 * 1024

_NT = (((1,), (1,)), ((), ()))
_TN = (((0,), (0,)), ((), ()))


def _cparams(*sem):
    return pltpu.CompilerParams(dimension_semantics=sem, vmem_limit_bytes=VMEM_LIMIT)


def _split_dot(a, b01):
    hi = a.astype(BF16)
    lo = (a - hi.astype(F32)).astype(BF16)
    return (jnp.dot(hi, b01, preferred_element_type=F32)
            + jnp.dot(lo, b01, preferred_element_type=F32))


def _mm_body(x_ref, w_ref, o_ref, *scratch, nk):
    prod = jnp.dot(x_ref[...], w_ref[...], preferred_element_type=F32)
    if nk == 1:
        o_ref[...] = prod.astype(o_ref.dtype)
        return
    acc_ref, = scratch
    k = pl.program_id(2)

    @pl.when(k == 0)
    def _():
        acc_ref[...] = prod

    @pl.when(k > 0)
    def _():
        acc_ref[...] += prod

    @pl.when(k == nk - 1)
    def _():
        o_ref[...] = acc_ref[...].astype(o_ref.dtype)


def _pick(n, pref):
    if n <= pref:
        return n
    t = pref
    while t >= LANES:
        if n % t == 0:
            return t
        t -= LANES
    return n


def matmul(x, w, out_dtype=BF16, tm=1024, tn=512, tk=4096, col0=0, n_cols=None):
    M, K = x.shape
    N = w.shape[1] if n_cols is None else n_cols
    tm, tn, tk = _pick(M, tm), _pick(N, tn), _pick(K, tk)
    assert col0 % tn == 0
    cb = col0 // tn
    nk = K // tk
    scratch = [] if nk == 1 else [pltpu.VMEM((tm, tn), F32)]
    return pl.pallas_call(
        functools.partial(_mm_body, nk=nk),
        grid=(M // tm, N // tn, nk),
        in_specs=[pl.BlockSpec((tm, tk), lambda i, j, k: (i, k)),
                  pl.BlockSpec((tk, tn), lambda i, j, k: (k, j + cb))],
        out_specs=pl.BlockSpec((tm, tn), lambda i, j, k: (i, j)),
        out_shape=jax.ShapeDtypeStruct((M, N), out_dtype),
        scratch_shapes=scratch,
        compiler_params=_cparams("parallel", "parallel", "arbitrary"),
        name="matmul",
    )(x, w)


def _mm_f32w_body(x_ref, w_ref, o_ref, wb_ref):
    @pl.when(pl.program_id(1) == 0)
    def _():
        wb_ref[...] = w_ref[0].astype(BF16)

    o_ref[...] = jnp.dot(x_ref[...], wb_ref[...], preferred_element_type=F32).astype(o_ref.dtype)


def matmul_f32w(x, w, layer, out_dtype=BF16, tm=1024, tn=512, col0=0, n_cols=None):
    M, K = x.shape
    N = w.shape[2] if n_cols is None else n_cols
    tm, tn = _pick(M, tm), _pick(N, tn)
    assert col0 % tn == 0
    cb = col0 // tn
    return pl.pallas_call(
        _mm_f32w_body,
        grid=(N // tn, M // tm),
        in_specs=[pl.BlockSpec((tm, K), lambda j, i: (i, 0)),
                  pl.BlockSpec((1, K, tn), lambda j, i: (layer, 0, j + cb))],
        out_specs=pl.BlockSpec((tm, tn), lambda j, i: (i, j)),
        out_shape=jax.ShapeDtypeStruct((M, N), out_dtype),
        scratch_shapes=[pltpu.VMEM((K, tn), BF16)],
        compiler_params=_cparams("parallel", "arbitrary"),
        name="matmul_f32w",
    )(x, w)


def _add_ln_body(h_ref, m_ref, g_ref, b_ref, o32_ref, o16_ref, *, alpha):
    y = alpha * h_ref[...] + m_ref[...].astype(F32)
    mu = jnp.mean(y, axis=-1, keepdims=True)
    d = y - mu
    var = jnp.mean(d * d, axis=-1, keepdims=True)
    out = d * lax.rsqrt(var + LN_EPS) * g_ref[...] + b_ref[...]
    o32_ref[...] = out
    o16_ref[...] = out.astype(BF16)


def add_layer_norm(h, mix, g, b, alpha, tm=256):
    M, D = h.shape
    tm = _pick(M, tm)
    row = pl.BlockSpec((tm, D), lambda i: (i, 0))
    vec = pl.BlockSpec((1, D), lambda i: (0, 0))
    return pl.pallas_call(
        functools.partial(_add_ln_body, alpha=alpha),
        grid=(M // tm,),
        in_specs=[row, row, vec, vec],
        out_specs=[row, row],
        out_shape=[jax.ShapeDtypeStruct((M, D), F32), jax.ShapeDtypeStruct((M, D), BF16)],
        compiler_params=_cparams("parallel"),
        name="add_layer_norm",
    )(h, mix, g.reshape(1, D).astype(F32), b.reshape(1, D).astype(F32))


def _hgrn_body(q_ref, f_ref, i_ref, g_ref, lb_ref, nw_ref, o_ref, st_ref, *, ts, layer_idx, nh):
    C = HGRN_SUB
    nsub = ts // C
    heads = [slice(h * HEAD_DIM, (h + 1) * HEAD_DIM) for h in range(nh)]

    @pl.when(pl.program_id(2) == 0)
    def _():
        st_ref[...] = jnp.zeros_like(st_ref)

    lbr = lb_ref[...]
    ex = jnp.exp(lbr - jnp.max(lbr, axis=0, keepdims=True))
    sm = ex / jnp.sum(ex, axis=0, keepdims=True)
    lb = jnp.sum(sm[:layer_idx + 1], axis=0, keepdims=True)

    q = q_ref[0].astype(F32)
    z = f_ref[0].astype(F32)
    v = i_ref[0]
    e = jnp.exp(-jnp.abs(z))
    r = 1.0 / (1.0 + e)
    pos = z >= 0
    sig = jnp.where(pos, r, e * r)
    nsig = jnp.where(pos, e * r, r)
    logf = jnp.log(lb + (1.0 - lb) * sig)
    k = (1.0 - lb) * nsig

    hi = logf.astype(BF16)
    lo = (logf - hi.astype(F32)).astype(BF16)
    row = lax.broadcasted_iota(jnp.int32, (ts, ts), 0)
    col = lax.broadcasted_iota(jnp.int32, (ts, ts), 1)

    def cumdot(m01):
        return jnp.dot(m01, hi, preferred_element_type=F32) + jnp.dot(m01, lo, preferred_element_type=F32)

    def whole_tile(sts):
        b = cumdot(jnp.where(row >= col, 1.0, 0.0).astype(BF16))
        bl = b[ts - 1:ts]
        qd = (q * jnp.exp(b)).astype(BF16)
        kinv = (k * jnp.exp(-b)).astype(BF16)
        kd = (k * jnp.exp(bl - b)).astype(BF16)
        dec = jnp.exp(bl)
        outs, new = [], []
        for h, sl in enumerate(heads):
            dmat = lax.dot_general(qd[:, sl], kinv[:, sl], _NT, preferred_element_type=F32)
            dmat = jnp.where(row >= col, dmat, 0.0)
            o = jnp.dot(dmat.astype(BF16), v[:, sl], preferred_element_type=F32)
            outs.append(o + lax.dot_general(qd[:, sl], sts[h].astype(BF16), _NT, preferred_element_type=F32))
            new.append(sts[h] * dec[:, sl] + lax.dot_general(v[:, sl], kd[:, sl], _TN, preferred_element_type=F32))
        return tuple(outs), tuple(new)

    def sub_chunks(sts):
        same = (row // C) == (col // C)
        b = cumdot(jnp.where(same & (row >= col), 1.0, 0.0).astype(BF16))
        bl = cumdot(jnp.where(same, 1.0, 0.0).astype(BF16))
        qd = (q * jnp.exp(b)).astype(BF16)
        kd = (k * jnp.exp(bl - b)).astype(BF16)
        dec = jnp.exp(bl)
        tri = (lax.broadcasted_iota(jnp.int32, (C, C, HEAD_DIM), 0)
               >= lax.broadcasted_iota(jnp.int32, (C, C, HEAD_DIM), 1))
        outs, new = [], []
        for h, hs in enumerate(heads):
            st = sts[h]
            parts = []
            for n in range(nsub):
                sl = slice(n * C, (n + 1) * C)
                bn, qn, kn = b[sl, hs], q[sl, hs], k[sl, hs]
                diff = bn[:, None, :] - bn[None, :, :]
                ee = jnp.exp(jnp.where(tri, diff, NEG_INF))
                dmat = jnp.sum(qn[:, None, :] * (kn[None, :, :] * ee), axis=-1)
                o_n = jnp.dot(dmat.astype(BF16), v[sl, hs], preferred_element_type=F32)
                o_n = o_n + lax.dot_general(qd[sl, hs], st.astype(BF16), _NT, preferred_element_type=F32)
                upd = lax.dot_general(v[sl, hs], kd[sl, hs], _TN, preferred_element_type=F32)
                st = st * dec[n * C:n * C + 1, hs] + upd
                parts.append(o_n)
            outs.append(jnp.concatenate(parts, axis=0))
            new.append(st)
        return tuple(outs), tuple(new)

    tile_decay = jnp.min(jnp.sum(logf, axis=0, keepdims=True))
    outs, sts = lax.cond(tile_decay > HGRN_SAFE_LOG_DECAY, whole_tile, sub_chunks,
                         tuple(st_ref[h] for h in range(nh)))
    gt = g_ref[0].astype(F32)
    gate = gt / (1.0 + jnp.exp(-gt))
    for h, sl in enumerate(heads):
        st_ref[h] = sts[h]
        o = outs[h]
        o = o * lax.rsqrt(jnp.mean(o * o, axis=-1, keepdims=True) + RMS_EPS) * nw_ref[...] * gate[:, sl]
        o_ref[0, :, sl] = o.astype(o_ref.dtype)


def hgrn2(proj, lb_raw, norm_w, n_heads, layer_idx, ts=128, nh=4):
    B, S, _ = proj.shape
    ts = _pick(S, ts)
    assert n_heads % nh == 0
    H = n_heads // nh
    L = lb_raw.shape[0]
    w = nh * HEAD_DIM

    def col(off):
        return pl.BlockSpec((1, ts, w), lambda b, h, s: (b, s, off * H + h))

    return pl.pallas_call(
        functools.partial(_hgrn_body, ts=ts, layer_idx=layer_idx, nh=nh),
        grid=(B, H, S // ts),
        in_specs=[col(0), col(1), col(2), col(3),
                  pl.BlockSpec((L, w), lambda b, h, s: (0, h)),
                  pl.BlockSpec((1, HEAD_DIM), lambda b, h, s: (0, 0))],
        out_specs=pl.BlockSpec((1, ts, w), lambda b, h, s: (b, s, h)),
        out_shape=jax.ShapeDtypeStruct((B, S, n_heads * HEAD_DIM), BF16),
        scratch_shapes=[pltpu.VMEM((nh, HEAD_DIM, HEAD_DIM), F32)],
        compiler_params=_cparams("parallel", "parallel", "arbitrary"),
        name="hgrn2",
    )(proj, proj, proj, proj, lb_raw.astype(F32), norm_w.reshape(1, HEAD_DIM).astype(F32))


def _sb_block(q, k, v, carry, after01, scale, mask):
    z = lax.dot_general(q, k, _NT, preferred_element_type=F32) * scale
    sp = jnp.maximum(z, jnp.log(1.0 + jnp.exp(jnp.minimum(z, SOFTPLUS_CLAMP))))
    spm = sp if mask is None else jnp.where(mask, sp, 0.0)
    rev = _split_dot(spm, after01)
    w = jnp.exp(z - sp - rev - carry)
    if mask is not None:
        w = jnp.where(mask, w, 0.0)
    contrib = jnp.dot(w.astype(BF16), v, preferred_element_type=F32)
    return contrib, carry + rev[:, 0:1] + spm[:, 0:1]


def _sb_body(q_ref, k_ref, v_ref, o_ref, *, tq, scale, nh):
    qi = pl.program_id(2)
    row = lax.broadcasted_iota(jnp.int32, (tq, tq), 0)
    col = lax.broadcasted_iota(jnp.int32, (tq, tq), 1)
    after01 = jnp.where(row > col, 1.0, 0.0).astype(BF16)
    heads = [slice(h * HEAD_DIM, (h + 1) * HEAD_DIM) for h in range(nh)]
    qs = [q_ref[0, :, sl] for sl in heads]

    def blocks(k0, carries, mask):
        res = [_sb_block(qs[h], k_ref[0, pl.ds(k0, tq), heads[h]], v_ref[0, pl.ds(k0, tq), heads[h]],
                         carries[h], after01, scale, mask) for h in range(nh)]
        return tuple(r[0] for r in res), tuple(r[1] for r in res)

    accs, carries = blocks(pl.multiple_of(qi * tq, tq), (jnp.zeros((tq, 1), F32),) * nh, col < row)

    def cond(c):
        j, _, carries = c
        low = functools.reduce(jnp.minimum, [jnp.min(cr) for cr in carries])
        return jnp.logical_and(j >= 0, low < -EXP_ZERO_BELOW)

    def body(c):
        j, accs, carries = c
        contribs, carries = blocks(pl.multiple_of(j * tq, tq), carries, None)
        return j - 1, tuple(a + cb for a, cb in zip(accs, contribs)), carries

    _, accs, _ = lax.while_loop(cond, body, (qi - 1, accs, carries))
    for h in range(nh):
        o_ref[0, :, heads[h]] = accs[h].astype(o_ref.dtype)


def stick_breaking(proj, col_q, col_k, col_v, n_heads, tq=256, nh=2):
    B, S, _ = proj.shape
    tq = _pick(S, tq)
    assert n_heads % nh == 0 and col_q % nh == 0 and col_k % nh == 0 and col_v % nh == 0
    w = nh * HEAD_DIM
    full = lambda off: pl.BlockSpec((1, S, w), lambda b, h, i: (b, 0, off // nh + h))
    return pl.pallas_call(
        functools.partial(_sb_body, tq=tq, scale=HEAD_DIM ** -0.5, nh=nh),
        grid=(B, n_heads // nh, S // tq),
        in_specs=[pl.BlockSpec((1, tq, w), lambda b, h, i: (b, i, col_q // nh + h)),
                  full(col_k), full(col_v)],
        out_specs=pl.BlockSpec((1, tq, w), lambda b, h, i: (b, i, h)),
        out_shape=jax.ShapeDtypeStruct((B, S, n_heads * HEAD_DIM), BF16),
        compiler_params=_cparams("parallel", "parallel", "arbitrary"),
        name="stick_breaking",
    )(proj, proj, proj)


def _rope_body(x_ref, cos_ref, sin_ref, o_ref, *, n_heads, mult):
    cos = cos_ref[...] * mult
    sin = sin_ref[...] * mult
    for h in range(n_heads):
        sl = slice(h * HEAD_DIM, (h + 1) * HEAD_DIM)
        t = x_ref[0, :, sl].astype(F32)
        o_ref[0, :, sl] = (t * cos + pltpu.roll(t, HEAD_DIM // 2, 1) * sin).astype(o_ref.dtype)


def rope(x, col0, n_heads, cos, sin, mult=1.0, ts=256):
    B, S, _ = x.shape
    ts = _pick(S, ts)
    w = n_heads * HEAD_DIM
    assert col0 % w == 0
    cb = col0 // w
    tab = pl.BlockSpec((ts, HEAD_DIM), lambda b, s: (s, 0))
    return pl.pallas_call(
        functools.partial(_rope_body, n_heads=n_heads, mult=mult),
        grid=(B, S // ts),
        in_specs=[pl.BlockSpec((1, ts, w), lambda b, s: (b, s, cb)), tab, tab],
        out_specs=pl.BlockSpec((1, ts, w), lambda b, s: (b, s, 0)),
        out_shape=jax.ShapeDtypeStruct((B, S, w), BF16),
        compiler_params=_cparams("parallel", "parallel"),
        name="rope",
    )(x, cos, sin)


def _gelu(x):
    return 0.5 * x * (1.0 + lax.erf(x * (2.0 ** -0.5)))


def _compress_body(x_ref, pe_ref, w1_ref, w2_ref, o_ref, *, batch, k_mult):
    x = x_ref[0, 0]
    half = x.shape[1]
    n16 = x.shape[0]
    y1 = jnp.dot(x, w1_ref[0, :half], preferred_element_type=F32)
    y2 = jnp.dot(x, w1_ref[0, half:], preferred_element_type=F32)
    bias = jnp.dot(pe_ref[0], w1_ref[0], preferred_element_type=F32)[0:1]
    hid = _gelu(y1 + pltpu.roll(y2, n16 - 1, 0) + bias)
    out = jnp.dot(hid.astype(BF16), w2_ref[0], preferred_element_type=F32)
    mult = jnp.where(pl.program_id(0) < batch, k_mult, 1.0)
    o_ref[0, 0] = (out * mult).astype(o_ref.dtype)


def compress(x16, pe, w1, w2, k_mult):
    two, B, G, n16, wide = x16.shape
    x16 = x16.reshape(two * B, G, n16, wide)
    out = pl.pallas_call(
        functools.partial(_compress_body, batch=B, k_mult=k_mult),
        grid=(two * B, G),
        in_specs=[pl.BlockSpec((1, 1, n16, wide), lambda i, g: (i, g, 0, 0)),
                  pl.BlockSpec((1, 8, 2 * wide), lambda i, g: (i // B, 0, 0)),
                  pl.BlockSpec((1, 2 * wide, HEAD_DIM), lambda i, g: (i // B, 0, 0)),
                  pl.BlockSpec((1, HEAD_DIM, HEAD_DIM), lambda i, g: (i // B, 0, 0))],
        out_specs=pl.BlockSpec((1, 1, n16, HEAD_DIM), lambda i, g: (i, g, 0, 0)),
        out_shape=jax.ShapeDtypeStruct((two * B, G, n16, HEAD_DIM), BF16),
        compiler_params=_cparams("parallel", "parallel"),
        name="nsa_compress",
    )(x16, pe, w1, w2)
    return out.reshape(two, B, G, n16, HEAD_DIM)


def _stack_heads(q_ref, rep):
    return jnp.concatenate([q_ref[0, :, r * HEAD_DIM:(r + 1) * HEAD_DIM] for r in range(rep)], axis=0)


def _cmp_body(q_ref, kc_ref, vc_ref, o_ref, imp_ref, *, tq, rep):
    q0 = pl.program_id(2) * tq
    kc = kc_ref[0, 0]
    n16 = kc.shape[0]
    cols = [slice(r * tq, (r + 1) * tq) for r in range(rep)]
    st = lax.dot_general(kc, _stack_heads(q_ref, rep), _NT, preferred_element_type=F32)
    n = lax.broadcasted_iota(jnp.int32, (n16, tq), 0)
    t = q0 + lax.broadcasted_iota(jnp.int32, (n16, tq), 1)
    bias = jnp.where(n * CMP_STRIDE + (CMP_LEN - 1) <= t, 0.0, NEG_INF)
    ps = []
    psum = jnp.zeros((n16, tq), F32)
    for r in range(rep):
        s = st[:, cols[r]] + bias
        m = jnp.maximum(jnp.max(s, axis=0, keepdims=True), 0.1 * NEG_INF)
        p = jnp.exp2(s - m)
        den = jnp.sum(p, axis=0, keepdims=True)
        pn = p * (1.0 / jnp.where(den > 0.0, den, 1.0))
        ps.append(pn.astype(BF16))
        psum = psum + pn
    ot = lax.dot_general(vc_ref[0, 0], jnp.concatenate(ps, axis=1), _TN, preferred_element_type=F32)
    for r in range(rep):
        o_ref[0, :, r * HEAD_DIM:(r + 1) * HEAD_DIM] = ot[:, cols[r]].T.astype(o_ref.dtype)

    cj = lax.broadcasted_iota(jnp.int32, (LANES, n16), 0) * SLC_LEN
    cn = lax.broadcasted_iota(jnp.int32, (LANES, n16), 1) * CMP_STRIDE
    ov01 = jnp.where((cn < cj + SLC_LEN) & (cn + CMP_LEN > cj), 1.0, 0.0).astype(BF16)
    hi = psum.astype(BF16)
    lo = (psum - hi.astype(F32)).astype(BF16)
    imp_t = jnp.dot(ov01, hi, preferred_element_type=F32) + jnp.dot(ov01, lo, preferred_element_type=F32)
    imp_ref[0, 0] = imp_t.T


def cmp_attention(q, kc, vc, rep, tq=128):
    B, S, _ = q.shape
    G, n16 = kc.shape[1], kc.shape[2]
    tq = _pick(S, tq)
    w = rep * HEAD_DIM
    kv = pl.BlockSpec((1, 1, n16, HEAD_DIM), lambda b, g, i: (b, g, 0, 0))
    return pl.pallas_call(
        functools.partial(_cmp_body, tq=tq, rep=rep),
        grid=(B, G, S // tq),
        in_specs=[pl.BlockSpec((1, tq, w), lambda b, g, i: (b, i, g)), kv, kv],
        out_specs=[pl.BlockSpec((1, tq, w), lambda b, g, i: (b, i, g)),
                   pl.BlockSpec((1, 1, tq, LANES), lambda b, g, i: (b, g, i, 0))],
        out_shape=[jax.ShapeDtypeStruct((B, S, G * w), BF16),
                   jax.ShapeDtypeStruct((B, G, S, LANES), F32)],
        compiler_params=_cparams("parallel", "parallel", "parallel"),
        name="nsa_cmp",
    )(q, kc, vc)


def _topk_body(imp_ref, sel_ref, *, tq, n_slc):
    q0 = pl.program_id(2) * tq
    imp = imp_ref[0, 0]
    tt = q0 + lax.broadcasted_iota(jnp.int32, (tq, LANES), 0)
    j = lax.broadcasted_iota(jnp.int32, (tq, LANES), 1)
    cur = tt // SLC_LEN
    forced = (j == 0) | (j == cur) | (j == cur - 1)
    allowed = j * SLC_LEN <= tt
    score = jnp.where(forced, FORCE_SCORE, jnp.where(allowed, imp, -1.0))
    score = jnp.where(j < n_slc, score, -jnp.inf)
    jf = j.astype(F32)
    sel = jnp.zeros((tq, LANES), F32)
    for _ in range(min(SLC_TOP, n_slc)):
        m = jnp.max(score, axis=-1, keepdims=True)
        first = jnp.min(jnp.where(score == m, jf, float(LANES)), axis=-1, keepdims=True)
        pick = jf == first
        sel = jnp.where(pick, 1.0, sel)
        score = jnp.where(pick, -jnp.inf, score)
    sel_ref[0, 0] = sel.astype(sel_ref.dtype)


def select_blocks(imp, tq=1024):
    B, G, S, _ = imp.shape
    n_slc = S // SLC_LEN
    assert n_slc <= LANES
    tq = _pick(S, tq)
    spec = pl.BlockSpec((1, 1, tq, LANES), lambda b, g, i: (b, g, i, 0))
    return pl.pallas_call(
        functools.partial(_topk_body, tq=tq, n_slc=n_slc),
        grid=(B, G, S // tq),
        in_specs=[spec],
        out_specs=spec,
        out_shape=jax.ShapeDtypeStruct((B, G, S, LANES), BF16),
        compiler_params=_cparams("parallel", "parallel", "parallel"),
        name="nsa_topk",
    )(imp)


def _gqa_body(*refs, tq, tk, rep, mode):
    q0 = pl.program_id(2) * tq
    if mode == "sel":
        q_ref, k_ref, v_ref, sel_ref, o_ref, s_ref, acc_ref = refs
        key_blk = lax.broadcasted_iota(jnp.int32, (tk, LANES), 0) // SLC_LEN
        lane_blk = lax.broadcasted_iota(jnp.int32, (tk, LANES), 1)
        unpicked = ((sel_ref[0, 0].astype(F32) - 1.0) * (-NEG_INF)).astype(BF16)
        q2 = jnp.concatenate([_stack_heads(q_ref, rep), jnp.concatenate([unpicked] * rep, axis=0)], axis=1)
    else:
        q_ref, k_ref, v_ref, o_ref, s_ref, acc_ref = refs
        q2 = _stack_heads(q_ref, rep)
    kpos = lax.broadcasted_iota(jnp.int32, (tk, tq), 0)
    t = q0 + lax.broadcasted_iota(jnp.int32, (tk, tq), 1)
    cols = [slice(r * tq, (r + 1) * tq) for r in range(rep)]

    def put_scores(slot, kj):
        k0 = pl.multiple_of(kj * tk, tk)
        keys = k_ref[0, pl.ds(k0, tk), :]
        if mode == "sel":
            onehot = jnp.where(key_blk + kj * (tk // SLC_LEN) == lane_blk, 1.0, 0.0).astype(BF16)
            keys = jnp.concatenate([keys, onehot], axis=1)
        s_ref[slot] = lax.dot_general(keys, q2, _NT, preferred_element_type=F32)

    def tile(slot, kj, m, l, diagonal):
        k0 = pl.multiple_of(kj * tk, tk)
        kp = k0 + kpos
        if mode == "sel":
            bias = jnp.where(kp <= t, 0.0, NEG_INF) if diagonal else None
        else:
            ok = kp > t - WINDOW
            if diagonal:
                ok = ok & (kp <= t)
            bias = jnp.where(ok, 0.0, NEG_INF)
        ps, m_new, l_new, scale = [], [], [], []
        for r in range(rep):
            s = s_ref[slot, :, cols[r]]
            if bias is None:
                mr = jnp.maximum(m[r], jnp.max(s, axis=0, keepdims=True))
                p = jnp.exp2(s - mr)
            else:
                mr = jnp.maximum(m[r], jnp.max(s + bias, axis=0, keepdims=True))
                p = jnp.exp2((s - mr) + bias)
            a = jnp.exp2(m[r] - mr)
            ps.append(p.astype(BF16))
            m_new.append(mr)
            l_new.append(a * l[r] + jnp.sum(p, axis=0, keepdims=True))
            scale.append(a)
        pt = jnp.concatenate(ps, axis=1)
        pv = lax.dot_general(v_ref[0, pl.ds(k0, tk), :], pt, _TN, preferred_element_type=F32)
        acc_ref[...] = jnp.concatenate(scale, axis=1) * acc_ref[...] + pv
        return tuple(m_new), tuple(l_new)

    kd = q0 // tk
    lo = 0 if mode == "sel" else jnp.maximum(q0 - (WINDOW - 1), 0) // tk
    put_scores(0, kd)
    put_scores(1, lo)
    acc_ref[...] = jnp.zeros_like(acc_ref)
    stats = tile(0, kd, (jnp.full((1, tq), NEG_INF, F32),) * rep, (jnp.zeros((1, tq), F32),) * rep, True)

    def pair(i, stats):
        kj = lo + 2 * i
        put_scores(0, kj + 1)
        stats = tile(1, kj, *stats, False)
        put_scores(1, kj + 2)
        return tile(0, kj + 1, *stats, False)

    n_off = kd - lo
    stats = lax.fori_loop(0, n_off // 2, pair, stats)
    m, l = lax.cond(n_off % 2 == 1, lambda st: tile(1, kd - 1, *st, False), lambda st: st, stats)
    for r in range(rep):
        out = acc_ref[:, cols[r]] * (1.0 / l[r])
        o_ref[0, :, r * HEAD_DIM:(r + 1) * HEAD_DIM] = out.T.astype(o_ref.dtype)


def gqa_attention(q, k, v, v_col0, rep, mode, sel=None, tq=128, tk=512):
    B, S, _ = q.shape
    G = k.shape[2] // HEAD_DIM
    tq, tk = _pick(S, tq), _pick(S, tk)
    assert tk % tq == 0
    w = rep * HEAD_DIM
    in_specs = [pl.BlockSpec((1, tq, w), lambda b, g, i: (b, i, g)),
                pl.BlockSpec((1, S, HEAD_DIM), lambda b, g, i: (b, 0, g)),
                pl.BlockSpec((1, S, HEAD_DIM), lambda b, g, i: (b, 0, v_col0 + g))]
    args = [q, k, v]
    scratch = [pltpu.VMEM((2, tk, rep * tq), F32), pltpu.VMEM((HEAD_DIM, rep * tq), F32)]
    if mode == "sel":
        assert S // SLC_LEN <= LANES and tk % SLC_LEN == 0
        in_specs += [pl.BlockSpec((1, 1, tq, LANES), lambda b, g, i: (b, g, i, 0))]
        args += [sel]
    return pl.pallas_call(
        functools.partial(_gqa_body, tq=tq, tk=tk, rep=rep, mode=mode),
        grid=(B, G, S // tq),
        in_specs=in_specs,
        out_specs=pl.BlockSpec((1, tq, w), lambda b, g, i: (b, i, g)),
        out_shape=jax.ShapeDtypeStruct((B, S, G * w), BF16),
        scratch_shapes=scratch,
        compiler_params=_cparams("parallel", "parallel", "arbitrary"),
        name="nsa_" + mode,
    )(*args)


def _nsa_gate_body(gl_ref, oc_ref, os_ref, ow_ref, o_ref, *, n_heads):
    ng = 3 * n_heads
    gl = gl_ref[:, :ng].astype(F32)
    gate = 1.0 / (1.0 + jnp.exp(-gl))
    src = lax.broadcasted_iota(jnp.int32, (ng, n_heads * HEAD_DIM), 0)
    head = lax.broadcasted_iota(jnp.int32, (ng, n_heads * HEAD_DIM), 1) // HEAD_DIM
    out = None
    for c, ref in enumerate((oc_ref, os_ref, ow_ref)):
        spread01 = jnp.where(src == head * 3 + c, 1.0, 0.0).astype(BF16)
        term = _split_dot(gate, spread01) * ref[...].astype(F32)
        out = term if out is None else out + term
    o_ref[...] = out.astype(o_ref.dtype)


def nsa_gate(gl, oc, os_, ow, n_heads, tm=256):
    M, W = oc.shape
    tm = _pick(M, tm)
    row = pl.BlockSpec((tm, W), lambda i: (i, 0))
    return pl.pallas_call(
        functools.partial(_nsa_gate_body, n_heads=n_heads),
        grid=(M // tm,),
        in_specs=[pl.BlockSpec((tm, gl.shape[1]), lambda i: (i, 0)), row, row, row],
        out_specs=row,
        out_shape=jax.ShapeDtypeStruct((M, W), BF16),
        compiler_params=_cparams("parallel"),
        name="nsa_gate",
    )(gl, oc, os_, ow)


def _xattn_body(q_ref, kv_ref, o_ref, *, n_heads, scale):
    w = n_heads * HEAD_DIM
    for h in range(n_heads):
        sl = slice(h * HEAD_DIM, (h + 1) * HEAD_DIM)
        k = kv_ref[0, :, sl]
        v = kv_ref[0, :, w + h * HEAD_DIM:w + (h + 1) * HEAD_DIM]
        s = lax.dot_general(q_ref[0, :, sl], k, _NT, preferred_element_type=F32) * scale
        p = jnp.exp(s - jnp.max(s, axis=-1, keepdims=True))
        p = p / jnp.sum(p, axis=-1, keepdims=True)
        o_ref[0, :, sl] = jnp.dot(p.astype(BF16), v, preferred_element_type=F32).astype(o_ref.dtype)


def mem_attention(q, kv, n_heads, tq=512):
    B, S, w = q.shape
    n_mem = kv.shape[1]
    tq = _pick(S, tq)
    return pl.pallas_call(
        functools.partial(_xattn_body, n_heads=n_heads, scale=HEAD_DIM ** -0.5),
        grid=(B, S // tq),
        in_specs=[pl.BlockSpec((1, tq, w), lambda b, i: (b, i, 0)),
                  pl.BlockSpec((1, n_mem, 2 * w), lambda b, i: (b, 0, 0))],
        out_specs=pl.BlockSpec((1, tq, w), lambda b, i: (b, i, 0)),
        out_shape=jax.ShapeDtypeStruct((B, S, w), BF16),
        compiler_params=_cparams("parallel", "parallel"),
        name="mem_attention",
    )(q, kv)


def _ffn_up_body(x_ref, wa_ref, wu_ref, cw_ref, o_ref, wab_ref, wub_ref, tail_ref, *, tiles_per_seq):
    i = pl.program_id(1)

    @pl.when(i == 0)
    def _():
        wab_ref[...] = wa_ref[0].astype(BF16)
        wub_ref[...] = wu_ref[0].astype(BF16)

    x = x_ref[...]
    a = jnp.dot(x, wab_ref[...], preferred_element_type=F32)
    tm = a.shape[0]
    first = (i % tiles_per_seq) == 0
    prev = jnp.where(first, 0.0, tail_ref[...])
    tail_ref[...] = a[tm - 8:]
    rowi = lax.broadcasted_iota(jnp.int32, a.shape, 0)
    a1 = jnp.where(rowi >= 1, pltpu.roll(a, 1, 0), prev[7:8])
    a2 = jnp.where(rowi >= 2, pltpu.roll(a, 2, 0), jnp.where(rowi == 1, prev[7:8], prev[6:7]))
    cw = cw_ref[...]
    gate = _gelu(cw[2:3] * a + cw[1:2] * a1 + cw[0:1] * a2)
    u = jnp.dot(x, wub_ref[...], preferred_element_type=F32)
    o_ref[...] = (gate * u).astype(o_ref.dtype)


def ffn_up_glu(x, w_up, layer, conv_w, seq_len, tm=1024, tn=256):
    M, K = x.shape
    Fd = w_up.shape[2] // 2
    tm, tn = _pick(seq_len, tm), _pick(Fd, tn)
    nj = Fd // tn
    return pl.pallas_call(
        functools.partial(_ffn_up_body, tiles_per_seq=seq_len // tm),
        grid=(nj, M // tm),
        in_specs=[pl.BlockSpec((tm, K), lambda j, i: (i, 0)),
                  pl.BlockSpec((1, K, tn), lambda j, i: (layer, 0, j)),
                  pl.BlockSpec((1, K, tn), lambda j, i: (layer, 0, j + nj)),
                  pl.BlockSpec((CONV_W, tn), lambda j, i: (0, j))],
        out_specs=pl.BlockSpec((tm, tn), lambda j, i: (i, j)),
        out_shape=jax.ShapeDtypeStruct((M, Fd), BF16),
        scratch_shapes=[pltpu.VMEM((K, tn), BF16), pltpu.VMEM((K, tn), BF16), pltpu.VMEM((8, tn), F32)],
        compiler_params=_cparams("parallel", "arbitrary"),
        name="ffn_up_glu",
    )(x, w_up, w_up, conv_w.astype(F32))


def _rope_tables(S):
    half = HEAD_DIM // 2
    inv_freq = ROPE_THETA ** (-jnp.arange(half, dtype=F32) / half)
    ang = jnp.arange(S, dtype=F32)[:, None] * inv_freq[None, :]
    cos, sin = jnp.cos(ang), jnp.sin(ang)
    return jnp.concatenate([cos, cos], axis=-1), jnp.concatenate([-sin, sin], axis=-1)


def _hgrn_sb_mixer(hb, B, S, w_in_all, lb_raw, norm_w, w_out, e):
    width = w_out.shape[0]
    a_heads = width // (2 * HEAD_DIM)
    b_heads = a_heads
    proj = matmul_f32w(hb, w_in_all, e).reshape(B, S, -1)
    o_a = hgrn2(proj, lb_raw, norm_w, a_heads, e)
    o_b = stick_breaking(proj, 4 * a_heads, 4 * a_heads + b_heads, 4 * a_heads + 2 * b_heads, b_heads)
    o = jnp.concatenate([o_a, o_b], axis=-1).reshape(B * S, width)
    return matmul(o, w_out.astype(BF16))


def _nsa_mixer(hb, B, S, w_in_all, o, cmp_pos, cmp_w1, cmp_w2, w_out, cos, sin):
    G = NSA_KV_HEADS
    q_w = w_out.shape[0]
    n_heads = q_w // HEAD_DIM
    rep = n_heads // G
    kv_w = G * HEAD_DIM
    main_w = q_w + 6 * kv_w
    proj = matmul_f32w(hb, w_in_all, o, n_cols=main_w).reshape(B, S, main_w)
    gl = matmul_f32w(hb, w_in_all, o, col0=main_w, n_cols=LANES, tn=LANES)

    log2_scale = HEAD_DIM ** -0.5 * LOG2E
    q_rot = rope(proj, 0, n_heads, cos, sin, mult=log2_scale)
    ks_rot = rope(proj, q_w + 2 * kv_w, G, cos, sin)
    kw_rot = rope(proj, q_w + 4 * kv_w, G, cos, sin)

    n16 = S // CMP_STRIDE
    kvc_in = proj[:, :, q_w:q_w + 2 * kv_w].reshape(B, S, 2, G, HEAD_DIM)
    x16 = kvc_in.transpose(2, 0, 3, 1, 4).reshape(2, B, G, n16, CMP_STRIDE * HEAD_DIM)
    pe = jnp.broadcast_to(cmp_pos.reshape(2, 1, CMP_LEN * HEAD_DIM), (2, 8, CMP_LEN * HEAD_DIM)).astype(BF16)
    w1 = cmp_w1.reshape(2, CMP_LEN * HEAD_DIM, HEAD_DIM).astype(BF16)
    kvc = compress(x16, pe, w1, cmp_w2.astype(BF16), log2_scale)

    o_c, imp = cmp_attention(proj, kvc[0], kvc[1], rep)
    sel = select_blocks(imp)
    col = lambda off: (q_w + off * kv_w) // HEAD_DIM
    o_s = gqa_attention(q_rot, ks_rot, proj, col(3), rep, "sel", sel=sel, tq=128, tk=512)
    o_w = gqa_attention(q_rot, kw_rot, proj, col(5), rep, "win", tq=256, tk=256)
    o = nsa_gate(gl, o_c.reshape(B * S, q_w), o_s.reshape(B * S, q_w), o_w.reshape(B * S, q_w), n_heads)
    return matmul(o, w_out.astype(BF16))


def kernel(x, mem, ab_w_in, hgrn_lb, hgrn_norm_w, ab_w_out, nsa_w_in, nsa_cmp_pos, nsa_cmp_w1,
           nsa_cmp_w2, nsa_w_out, xa_w_q, xa_w_kv, xa_w_o, ffn_w_up, ffn_conv, ffn_w_down, ln_g, ln_b):
    B, S, D = x.shape
    depth = ln_g.shape[0]
    alpha = (2 * depth) ** 0.25
    n_mem = mem.shape[1]
    cos, sin = _rope_tables(S)
    h = x.reshape(B * S, D).astype(F32)
    hb = h.astype(BF16)
    memb = mem.reshape(B * n_mem, D).astype(BF16)
    for layer in range(depth):
        if layer % 2 == 0:
            e = layer // 2
            mix = _hgrn_sb_mixer(hb, B, S, ab_w_in, hgrn_lb, hgrn_norm_w[e], ab_w_out[e], e)
        else:
            o = layer // 2
            mix = _nsa_mixer(hb, B, S, nsa_w_in, o, nsa_cmp_pos[o], nsa_cmp_w1[o], nsa_cmp_w2[o],
                             nsa_w_out[o], cos, sin)
        h, hb = add_layer_norm(h, mix, ln_g[layer, 0], ln_b[layer, 0], alpha)

        xq = matmul(hb, xa_w_q[layer].astype(BF16)).reshape(B, S, -1)
        xkv = matmul(memb, xa_w_kv[layer].astype(BF16)).reshape(B, n_mem, -1)
        xo = mem_attention(xq, xkv, XA_HEADS).reshape(B * S, -1)
        h, hb = add_layer_norm(h, matmul(xo, xa_w_o[layer].astype(BF16)), ln_g[layer, 1], ln_b[layer, 1], alpha)

        gated = ffn_up_glu(hb, ffn_w_up, layer, ffn_conv[layer], S)
        h, hb = add_layer_norm(h, matmul(gated, ffn_w_down[layer].astype(BF16), tk=5504),
                               ln_g[layer, 2], ln_b[layer, 2], alpha)
    return h.reshape(B, S, D).astype(x.dtype)
```

```python
import functools

import jax
import jax.numpy as jnp
from jax import lax
from jax.experimental import pallas as pl
from jax.experimental.pallas import tpu as pltpu

F32 = jnp.float32
BF16 = jnp.bfloat16

HEAD_DIM = 128
LANES = 128
HGRN_SUB = 16
HGRN_SAFE_LOG_DECAY = -60.0
NSA_KV_HEADS = 4
CMP_LEN = 32
CMP_STRIDE = 16
SLC_LEN = 64
SLC_TOP = 16
WINDOW = 512
XA_HEADS = 4
CONV_W = 3
ROPE_THETA = 10000.0
LN_EPS = 1e-5
RMS_EPS = 1e-6
NEG_INF = -1e30
FORCE_SCORE = 1e9
EXP_ZERO_BELOW = -104.0
SOFTPLUS_CLAMP = 80.0
LOG2E = 1.4426950408889634
VMEM_LIMIT = 52 * 1024 * 1024

_NT = (((1,), (1,)), ((), ()))
_TN = (((0,), (0,)), ((), ()))


def _cparams(*sem):
    return pltpu.CompilerParams(dimension_semantics=sem, vmem_limit_bytes=VMEM_LIMIT)


def _split_dot(a, b01):
    hi = a.astype(BF16)
    lo = (a - hi.astype(F32)).astype(BF16)
    return (jnp.dot(hi, b01, preferred_element_type=F32)
            + jnp.dot(lo, b01, preferred_element_type=F32))


def _mm_body(x_ref, w_ref, o_ref, *scratch, nk):
    prod = jnp.dot(x_ref[...], w_ref[...], preferred_element_type=F32)
    if nk == 1:
        o_ref[...] = prod.astype(o_ref.dtype)
        return
    acc_ref, = scratch
    k = pl.program_id(2)

    @pl.when(k == 0)
    def _():
        acc_ref[...] = prod

    @pl.when(k > 0)
    def _():
        acc_ref[...] += prod

    @pl.when(k == nk - 1)
    def _():
        o_ref[...] = acc_ref[...].astype(o_ref.dtype)


def _pick(n, pref):
    if n <= pref:
        return n
    t = pref
    while t >= LANES:
        if n % t == 0:
            return t
        t -= LANES
    return n


def matmul(x, w, out_dtype=BF16, tm=1024, tn=512, tk=4096, col0=0, n_cols=None):
    M, K = x.shape
    N = w.shape[1] if n_cols is None else n_cols
    tm, tn, tk = _pick(M, tm), _pick(N, tn), _pick(K, tk)
    assert col0 % tn == 0
    cb = col0 // tn
    nk = K // tk
    scratch = [] if nk == 1 else [pltpu.VMEM((tm, tn), F32)]
    return pl.pallas_call(
        functools.partial(_mm_body, nk=nk),
        grid=(M // tm, N // tn, nk),
        in_specs=[pl.BlockSpec((tm, tk), lambda i, j, k: (i, k)),
                  pl.BlockSpec((tk, tn), lambda i, j, k: (k, j + cb))],
        out_specs=pl.BlockSpec((tm, tn), lambda i, j, k: (i, j)),
        out_shape=jax.ShapeDtypeStruct((M, N), out_dtype),
        scratch_shapes=scratch,
        compiler_params=_cparams("parallel", "parallel", "arbitrary"),
        name="matmul",
    )(x, w)


def _mm_f32w_body(x_ref, w_ref, o_ref, wb_ref):
    @pl.when(pl.program_id(1) == 0)
    def _():
        wb_ref[...] = w_ref[0].astype(BF16)

    o_ref[...] = jnp.dot(x_ref[...], wb_ref[...], preferred_element_type=F32).astype(o_ref.dtype)


def matmul_f32w(x, w, layer, out_dtype=BF16, tm=1024, tn=512, col0=0, n_cols=None):
    M, K = x.shape
    N = w.shape[2] if n_cols is None else n_cols
    tm, tn = _pick(M, tm), _pick(N, tn)
    assert col0 % tn == 0
    cb = col0 // tn
    return pl.pallas_call(
        _mm_f32w_body,
        grid=(N // tn, M // tm),
        in_specs=[pl.BlockSpec((tm, K), lambda j, i: (i, 0)),
                  pl.BlockSpec((1, K, tn), lambda j, i: (layer, 0, j + cb))],
        out_specs=pl.BlockSpec((tm, tn), lambda j, i: (i, j)),
        out_shape=jax.ShapeDtypeStruct((M, N), out_dtype),
        scratch_shapes=[pltpu.VMEM((K, tn), BF16)],
        compiler_params=_cparams("parallel", "arbitrary"),
        name="matmul_f32w",
    )(x, w)


def _layer_norm(y, g, b):
    mu = jnp.mean(y, axis=-1, keepdims=True)
    d = y - mu
    var = jnp.mean(d * d, axis=-1, keepdims=True)
    return d * lax.rsqrt(var + LN_EPS) * g + b


def _add_ln_body(h_ref, m_ref, g_ref, b_ref, o32_ref, o16_ref, *, alpha):
    out = _layer_norm(alpha * h_ref[...] + m_ref[...].astype(F32), g_ref[...], b_ref[...])
    o32_ref[...] = out
    o16_ref[...] = out.astype(BF16)


def add_layer_norm(h, mix, g, b, alpha, tm=256):
    M, D = h.shape
    tm = _pick(M, tm)
    row = pl.BlockSpec((tm, D), lambda i: (i, 0))
    vec = pl.BlockSpec((1, D), lambda i: (0, 0))
    return pl.pallas_call(
        functools.partial(_add_ln_body, alpha=alpha),
        grid=(M // tm,),
        in_specs=[row, row, vec, vec],
        out_specs=[row, row],
        out_shape=[jax.ShapeDtypeStruct((M, D), F32), jax.ShapeDtypeStruct((M, D), BF16)],
        compiler_params=_cparams("parallel"),
        name="add_layer_norm",
    )(h, mix, g.reshape(1, D).astype(F32), b.reshape(1, D).astype(F32))


def _hgrn_body(q_ref, f_ref, i_ref, g_ref, lb_ref, nw_ref, o_ref, st_ref, *, ts, layer_idx, nh):
    C = HGRN_SUB
    nsub = ts // C
    heads = [slice(h * HEAD_DIM, (h + 1) * HEAD_DIM) for h in range(nh)]

    @pl.when(pl.program_id(2) == 0)
    def _():
        st_ref[...] = jnp.zeros_like(st_ref)

    lbr = lb_ref[...]
    ex = jnp.exp(lbr - jnp.max(lbr, axis=0, keepdims=True))
    sm = ex / jnp.sum(ex, axis=0, keepdims=True)
    lb = jnp.sum(sm[:layer_idx + 1], axis=0, keepdims=True)

    q = q_ref[0].astype(F32)
    z = f_ref[0].astype(F32)
    v = i_ref[0]
    e = jnp.exp(-jnp.abs(z))
    r = 1.0 / (1.0 + e)
    pos = z >= 0
    sig = jnp.where(pos, r, e * r)
    nsig = jnp.where(pos, e * r, r)
    logf = jnp.log(lb + (1.0 - lb) * sig)
    k = (1.0 - lb) * nsig

    hi = logf.astype(BF16)
    lo = (logf - hi.astype(F32)).astype(BF16)
    row = lax.broadcasted_iota(jnp.int32, (ts, ts), 0)
    col = lax.broadcasted_iota(jnp.int32, (ts, ts), 1)

    def cumdot(m01):
        return jnp.dot(m01, hi, preferred_element_type=F32) + jnp.dot(m01, lo, preferred_element_type=F32)

    def whole_tile(sts):
        b = cumdot(jnp.where(row >= col, 1.0, 0.0).astype(BF16))
        bl = b[ts - 1:ts]
        qd = (q * jnp.exp(b)).astype(BF16)
        kinv = (k * jnp.exp(-b)).astype(BF16)
        kd = (k * jnp.exp(bl - b)).astype(BF16)
        dec = jnp.exp(bl)
        outs, new = [], []
        for h, sl in enumerate(heads):
            dmat = lax.dot_general(qd[:, sl], kinv[:, sl], _NT, preferred_element_type=F32)
            dmat = jnp.where(row >= col, dmat, 0.0)
            o = jnp.dot(dmat.astype(BF16), v[:, sl], preferred_element_type=F32)
            outs.append(o + lax.dot_general(qd[:, sl], sts[h].astype(BF16), _NT, preferred_element_type=F32))
            new.append(sts[h] * dec[:, sl] + lax.dot_general(v[:, sl], kd[:, sl], _TN, preferred_element_type=F32))
        return tuple(outs), tuple(new)

    def sub_chunks(sts):
        same = (row // C) == (col // C)
        b = cumdot(jnp.where(same & (row >= col), 1.0, 0.0).astype(BF16))
        bl = cumdot(jnp.where(same, 1.0, 0.0).astype(BF16))
        qd = (q * jnp.exp(b)).astype(BF16)
        kd = (k * jnp.exp(bl - b)).astype(BF16)
        dec = jnp.exp(bl)
        tri = (lax.broadcasted_iota(jnp.int32, (C, C, HEAD_DIM), 0)
               >= lax.broadcasted_iota(jnp.int32, (C, C, HEAD_DIM), 1))
        outs, new = [], []
        for h, hs in enumerate(heads):
            st = sts[h]
            parts = []
            for n in range(nsub):
                sl = slice(n * C, (n + 1) * C)
                bn, qn, kn = b[sl, hs], q[sl, hs], k[sl, hs]
                diff = bn[:, None, :] - bn[None, :, :]
                ee = jnp.exp(jnp.where(tri, diff, NEG_INF))
                dmat = jnp.sum(qn[:, None, :] * (kn[None, :, :] * ee), axis=-1)
                o_n = jnp.dot(dmat.astype(BF16), v[sl, hs], preferred_element_type=F32)
                o_n = o_n + lax.dot_general(qd[sl, hs], st.astype(BF16), _NT, preferred_element_type=F32)
                upd = lax.dot_general(v[sl, hs], kd[sl, hs], _TN, preferred_element_type=F32)
                st = st * dec[n * C:n * C + 1, hs] + upd
                parts.append(o_n)
            outs.append(jnp.concatenate(parts, axis=0))
            new.append(st)
        return tuple(outs), tuple(new)

    tile_decay = jnp.min(jnp.sum(logf, axis=0, keepdims=True))
    outs, sts = lax.cond(tile_decay > HGRN_SAFE_LOG_DECAY, whole_tile, sub_chunks,
                         tuple(st_ref[h] for h in range(nh)))
    gt = g_ref[0].astype(F32)
    gate = gt / (1.0 + jnp.exp(-gt))
    for h, sl in enumerate(heads):
        st_ref[h] = sts[h]
        o = outs[h]
        o = o * lax.rsqrt(jnp.mean(o * o, axis=-1, keepdims=True) + RMS_EPS) * nw_ref[...] * gate[:, sl]
        o_ref[0, :, sl] = o.astype(o_ref.dtype)


def hgrn2(proj, lb_raw, norm_w, n_heads, layer_idx, ts=128, nh=4):
    B, S, _ = proj.shape
    ts = _pick(S, ts)
    assert n_heads % nh == 0
    H = n_heads // nh
    L = lb_raw.shape[0]
    w = nh * HEAD_DIM

    def col(off):
        return pl.BlockSpec((1, ts, w), lambda b, h, s: (b, s, off * H + h))

    return pl.pallas_call(
        functools.partial(_hgrn_body, ts=ts, layer_idx=layer_idx, nh=nh),
        grid=(B, H, S // ts),
        in_specs=[col(0), col(1), col(2), col(3),
                  pl.BlockSpec((L, w), lambda b, h, s: (0, h)),
                  pl.BlockSpec((1, HEAD_DIM), lambda b, h, s: (0, 0))],
        out_specs=pl.BlockSpec((1, ts, w), lambda b, h, s: (b, s, h)),
        out_shape=jax.ShapeDtypeStruct((B, S, n_heads * HEAD_DIM), BF16),
        scratch_shapes=[pltpu.VMEM((nh, HEAD_DIM, HEAD_DIM), F32)],
        compiler_params=_cparams("parallel", "parallel", "arbitrary"),
        name="hgrn2",
    )(proj, proj, proj, proj, lb_raw.astype(F32), norm_w.reshape(1, HEAD_DIM).astype(F32))


def _sb_block(q, k, v, carry, after01, scale, mask):
    z = lax.dot_general(q, k, _NT, preferred_element_type=F32) * scale
    sp = jnp.maximum(z, jnp.log(1.0 + jnp.exp(jnp.minimum(z, SOFTPLUS_CLAMP))))
    spm = sp if mask is None else jnp.where(mask, sp, 0.0)
    rev = _split_dot(spm, after01)
    w = jnp.exp(z - sp - rev - carry)
    if mask is not None:
        w = jnp.where(mask, w, 0.0)
    contrib = jnp.dot(w.astype(BF16), v, preferred_element_type=F32)
    return contrib, carry + rev[:, 0:1] + spm[:, 0:1]


def _sb_body(q_ref, k_ref, v_ref, o_ref, *, tq, scale, nh):
    qi = pl.program_id(2)
    row = lax.broadcasted_iota(jnp.int32, (tq, tq), 0)
    col = lax.broadcasted_iota(jnp.int32, (tq, tq), 1)
    after01 = jnp.where(row > col, 1.0, 0.0).astype(BF16)
    heads = [slice(h * HEAD_DIM, (h + 1) * HEAD_DIM) for h in range(nh)]
    qs = [q_ref[0, :, sl] for sl in heads]

    def blocks(k0, carries, mask):
        res = [_sb_block(qs[h], k_ref[0, pl.ds(k0, tq), heads[h]], v_ref[0, pl.ds(k0, tq), heads[h]],
                         carries[h], after01, scale, mask) for h in range(nh)]
        return tuple(r[0] for r in res), tuple(r[1] for r in res)

    accs, carries = blocks(pl.multiple_of(qi * tq, tq), (jnp.zeros((tq, 1), F32),) * nh, col < row)

    def cond(c):
        j, _, carries = c
        low = functools.reduce(jnp.minimum, [jnp.min(cr) for cr in carries])
        return jnp.logical_and(j >= 0, low < -EXP_ZERO_BELOW)

    def body(c):
        j, accs, carries = c
        contribs, carries = blocks(pl.multiple_of(j * tq, tq), carries, None)
        return j - 1, tuple(a + cb for a, cb in zip(accs, contribs)), carries

    _, accs, _ = lax.while_loop(cond, body, (qi - 1, accs, carries))
    for h in range(nh):
        o_ref[0, :, heads[h]] = accs[h].astype(o_ref.dtype)


def stick_breaking(proj, col_q, col_k, col_v, n_heads, tq=256, nh=2):
    B, S, _ = proj.shape
    tq = _pick(S, tq)
    assert n_heads % nh == 0 and col_q % nh == 0 and col_k % nh == 0 and col_v % nh == 0
    w = nh * HEAD_DIM
    full = lambda off: pl.BlockSpec((1, S, w), lambda b, h, i: (b, 0, off // nh + h))
    return pl.pallas_call(
        functools.partial(_sb_body, tq=tq, scale=HEAD_DIM ** -0.5, nh=nh),
        grid=(B, n_heads // nh, S // tq),
        in_specs=[pl.BlockSpec((1, tq, w), lambda b, h, i: (b, i, col_q // nh + h)),
                  full(col_k), full(col_v)],
        out_specs=pl.BlockSpec((1, tq, w), lambda b, h, i: (b, i, h)),
        out_shape=jax.ShapeDtypeStruct((B, S, n_heads * HEAD_DIM), BF16),
        compiler_params=_cparams("parallel", "parallel", "arbitrary"),
        name="stick_breaking",
    )(proj, proj, proj)


def _rope_body(x_ref, cos_ref, sin_ref, o_ref, *, n_heads, mult):
    cos = cos_ref[...] * mult
    sin = sin_ref[...] * mult
    for h in range(n_heads):
        sl = slice(h * HEAD_DIM, (h + 1) * HEAD_DIM)
        t = x_ref[0, :, sl].astype(F32)
        o_ref[0, :, sl] = (t * cos + pltpu.roll(t, HEAD_DIM // 2, 1) * sin).astype(o_ref.dtype)


def rope(x, col0, n_heads, cos, sin, mult=1.0, ts=256):
    B, S, _ = x.shape
    ts = _pick(S, ts)
    w = n_heads * HEAD_DIM
    assert col0 % w == 0
    cb = col0 // w
    tab = pl.BlockSpec((ts, HEAD_DIM), lambda b, s: (s, 0))
    return pl.pallas_call(
        functools.partial(_rope_body, n_heads=n_heads, mult=mult),
        grid=(B, S // ts),
        in_specs=[pl.BlockSpec((1, ts, w), lambda b, s: (b, s, cb)), tab, tab],
        out_specs=pl.BlockSpec((1, ts, w), lambda b, s: (b, s, 0)),
        out_shape=jax.ShapeDtypeStruct((B, S, w), BF16),
        compiler_params=_cparams("parallel", "parallel"),
        name="rope",
    )(x, cos, sin)


def _gelu(x):
    return 0.5 * x * (1.0 + lax.erf(x * (2.0 ** -0.5)))


def _compress_body(x_ref, pe_ref, w1_ref, w2_ref, o_ref, *, batch, k_mult):
    x = x_ref[0, 0]
    half = x.shape[1]
    n16 = x.shape[0]
    y1 = jnp.dot(x, w1_ref[0, :half], preferred_element_type=F32)
    y2 = jnp.dot(x, w1_ref[0, half:], preferred_element_type=F32)
    bias = jnp.dot(pe_ref[0], w1_ref[0], preferred_element_type=F32)[0:1]
    hid = _gelu(y1 + pltpu.roll(y2, n16 - 1, 0) + bias)
    out = jnp.dot(hid.astype(BF16), w2_ref[0], preferred_element_type=F32)
    mult = jnp.where(pl.program_id(0) < batch, k_mult, 1.0)
    o_ref[0, 0] = (out * mult).astype(o_ref.dtype)


def compress(x16, pe, w1, w2, k_mult):
    two, B, G, n16, wide = x16.shape
    x16 = x16.reshape(two * B, G, n16, wide)
    out = pl.pallas_call(
        functools.partial(_compress_body, batch=B, k_mult=k_mult),
        grid=(two * B, G),
        in_specs=[pl.BlockSpec((1, 1, n16, wide), lambda i, g: (i, g, 0, 0)),
                  pl.BlockSpec((1, 8, 2 * wide), lambda i, g: (i // B, 0, 0)),
                  pl.BlockSpec((1, 2 * wide, HEAD_DIM), lambda i, g: (i // B, 0, 0)),
                  pl.BlockSpec((1, HEAD_DIM, HEAD_DIM), lambda i, g: (i // B, 0, 0))],
        out_specs=pl.BlockSpec((1, 1, n16, HEAD_DIM), lambda i, g: (i, g, 0, 0)),
        out_shape=jax.ShapeDtypeStruct((two * B, G, n16, HEAD_DIM), BF16),
        compiler_params=_cparams("parallel", "parallel"),
        name="nsa_compress",
    )(x16, pe, w1, w2)
    return out.reshape(two, B, G, n16, HEAD_DIM)


def _stack_heads(q_ref, rep):
    return jnp.concatenate([q_ref[0, :, r * HEAD_DIM:(r + 1) * HEAD_DIM] for r in range(rep)], axis=0)


def _cmp_body(q_ref, kc_ref, vc_ref, o_ref, imp_ref, *, tq, rep):
    q0 = pl.program_id(2) * tq
    kc = kc_ref[0, 0]
    n16 = kc.shape[0]
    cols = [slice(r * tq, (r + 1) * tq) for r in range(rep)]
    st = lax.dot_general(kc, _stack_heads(q_ref, rep), _NT, preferred_element_type=F32)
    n = lax.broadcasted_iota(jnp.int32, (n16, tq), 0)
    t = q0 + lax.broadcasted_iota(jnp.int32, (n16, tq), 1)
    bias = jnp.where(n * CMP_STRIDE + (CMP_LEN - 1) <= t, 0.0, NEG_INF)
    ps = []
    psum = jnp.zeros((n16, tq), F32)
    for r in range(rep):
        s = st[:, cols[r]] + bias
        m = jnp.maximum(jnp.max(s, axis=0, keepdims=True), 0.1 * NEG_INF)
        p = jnp.exp2(s - m)
        den = jnp.sum(p, axis=0, keepdims=True)
        pn = p * (1.0 / jnp.where(den > 0.0, den, 1.0))
        ps.append(pn.astype(BF16))
        psum = psum + pn
    ot = lax.dot_general(vc_ref[0, 0], jnp.concatenate(ps, axis=1), _TN, preferred_element_type=F32)
    for r in range(rep):
        o_ref[0, :, r * HEAD_DIM:(r + 1) * HEAD_DIM] = ot[:, cols[r]].T.astype(o_ref.dtype)

    cj = lax.broadcasted_iota(jnp.int32, (LANES, n16), 0) * SLC_LEN
    cn = lax.broadcasted_iota(jnp.int32, (LANES, n16), 1) * CMP_STRIDE
    ov01 = jnp.where((cn < cj + SLC_LEN) & (cn + CMP_LEN > cj), 1.0, 0.0).astype(BF16)
    hi = psum.astype(BF16)
    lo = (psum - hi.astype(F32)).astype(BF16)
    imp_t = jnp.dot(ov01, hi, preferred_element_type=F32) + jnp.dot(ov01, lo, preferred_element_type=F32)
    imp_ref[0, 0] = imp_t.T


def cmp_attention(q, kc, vc, rep, tq=128):
    B, S, _ = q.shape
    G, n16 = kc.shape[1], kc.shape[2]
    tq = _pick(S, tq)
    w = rep * HEAD_DIM
    kv = pl.BlockSpec((1, 1, n16, HEAD_DIM), lambda b, g, i: (b, g, 0, 0))
    return pl.pallas_call(
        functools.partial(_cmp_body, tq=tq, rep=rep),
        grid=(B, G, S // tq),
        in_specs=[pl.BlockSpec((1, tq, w), lambda b, g, i: (b, i, g)), kv, kv],
        out_specs=[pl.BlockSpec((1, tq, w), lambda b, g, i: (b, i, g)),
                   pl.BlockSpec((1, 1, tq, LANES), lambda b, g, i: (b, g, i, 0))],
        out_shape=[jax.ShapeDtypeStruct((B, S, G * w), BF16),
                   jax.ShapeDtypeStruct((B, G, S, LANES), F32)],
        compiler_params=_cparams("parallel", "parallel", "parallel"),
        name="nsa_cmp",
    )(q, kc, vc)


def _topk_body(imp_ref, sel_ref, *, tq, n_slc):
    q0 = pl.program_id(2) * tq
    imp = imp_ref[0, 0]
    tt = q0 + lax.broadcasted_iota(jnp.int32, (tq, LANES), 0)
    j = lax.broadcasted_iota(jnp.int32, (tq, LANES), 1)
    cur = tt // SLC_LEN
    forced = (j == 0) | (j == cur) | (j == cur - 1)
    allowed = j * SLC_LEN <= tt
    score = jnp.where(forced, FORCE_SCORE, jnp.where(allowed, imp, -1.0))
    score = jnp.where(j < n_slc, score, -jnp.inf)
    jf = j.astype(F32)
    sel = jnp.zeros((tq, LANES), F32)
    for _ in range(min(SLC_TOP, n_slc)):
        m = jnp.max(score, axis=-1, keepdims=True)
        first = jnp.min(jnp.where(score == m, jf, float(LANES)), axis=-1, keepdims=True)
        pick = jf == first
        sel = jnp.where(pick, 1.0, sel)
        score = jnp.where(pick, -jnp.inf, score)
    sel_ref[0, 0] = sel.astype(sel_ref.dtype)


def select_blocks(imp, tq=1024):
    B, G, S, _ = imp.shape
    n_slc = S // SLC_LEN
    assert n_slc <= LANES
    tq = _pick(S, tq)
    spec = pl.BlockSpec((1, 1, tq, LANES), lambda b, g, i: (b, g, i, 0))
    return pl.pallas_call(
        functools.partial(_topk_body, tq=tq, n_slc=n_slc),
        grid=(B, G, S // tq),
        in_specs=[spec],
        out_specs=spec,
        out_shape=jax.ShapeDtypeStruct((B, G, S, LANES), BF16),
        compiler_params=_cparams("parallel", "parallel", "parallel"),
        name="nsa_topk",
    )(imp)


def _gqa_body(*refs, tq, tk, rep, mode):
    q0 = pl.program_id(2) * tq
    if mode == "sel":
        q_ref, k_ref, v_ref, sel_ref, o_ref, s_ref, acc_ref = refs
        key_blk = lax.broadcasted_iota(jnp.int32, (tk, LANES), 0) // SLC_LEN
        lane_blk = lax.broadcasted_iota(jnp.int32, (tk, LANES), 1)
        unpicked = ((sel_ref[0, 0].astype(F32) - 1.0) * (-NEG_INF)).astype(BF16)
        q2 = jnp.concatenate([_stack_heads(q_ref, rep), jnp.concatenate([unpicked] * rep, axis=0)], axis=1)
    else:
        q_ref, k_ref, v_ref, o_ref, s_ref, acc_ref = refs
        q2 = _stack_heads(q_ref, rep)
    kpos = lax.broadcasted_iota(jnp.int32, (tk, tq), 0)
    t = q0 + lax.broadcasted_iota(jnp.int32, (tk, tq), 1)
    cols = [slice(r * tq, (r + 1) * tq) for r in range(rep)]

    def put_scores(slot, kj):
        k0 = pl.multiple_of(kj * tk, tk)
        keys = k_ref[0, pl.ds(k0, tk), :]
        if mode == "sel":
            onehot = jnp.where(key_blk + kj * (tk // SLC_LEN) == lane_blk, 1.0, 0.0).astype(BF16)
            keys = jnp.concatenate([keys, onehot], axis=1)
        s_ref[slot] = lax.dot_general(keys, q2, _NT, preferred_element_type=F32)

    def tile(slot, kj, m, l, diagonal):
        k0 = pl.multiple_of(kj * tk, tk)
        kp = k0 + kpos
        if mode == "sel":
            bias = jnp.where(kp <= t, 0.0, NEG_INF) if diagonal else None
        else:
            ok = kp > t - WINDOW
            if diagonal:
                ok = ok & (kp <= t)
            bias = jnp.where(ok, 0.0, NEG_INF)
        ps, m_new, l_new, scale = [], [], [], []
        for r in range(rep):
            s = s_ref[slot, :, cols[r]]
            if bias is None:
                mr = jnp.maximum(m[r], jnp.max(s, axis=0, keepdims=True))
                p = jnp.exp2(s - mr)
            else:
                mr = jnp.maximum(m[r], jnp.max(s + bias, axis=0, keepdims=True))
                p = jnp.exp2((s - mr) + bias)
            a = jnp.exp2(m[r] - mr)
            ps.append(p.astype(BF16))
            m_new.append(mr)
            l_new.append(a * l[r] + jnp.sum(p, axis=0, keepdims=True))
            scale.append(a)
        pt = jnp.concatenate(ps, axis=1)
        pv = lax.dot_general(v_ref[0, pl.ds(k0, tk), :], pt, _TN, preferred_element_type=F32)
        acc_ref[...] = jnp.concatenate(scale, axis=1) * acc_ref[...] + pv
        return tuple(m_new), tuple(l_new)

    kd = q0 // tk
    lo = 0 if mode == "sel" else jnp.maximum(q0 - (WINDOW - 1), 0) // tk
    put_scores(0, kd)
    put_scores(1, lo)
    acc_ref[...] = jnp.zeros_like(acc_ref)
    stats = tile(0, kd, (jnp.full((1, tq), NEG_INF, F32),) * rep, (jnp.zeros((1, tq), F32),) * rep, True)

    def pair(i, stats):
        kj = lo + 2 * i
        put_scores(0, kj + 1)
        stats = tile(1, kj, *stats, False)
        put_scores(1, kj + 2)
        return tile(0, kj + 1, *stats, False)

    n_off = kd - lo
    stats = lax.fori_loop(0, n_off // 2, pair, stats)
    m, l = lax.cond(n_off % 2 == 1, lambda st: tile(1, kd - 1, *st, False), lambda st: st, stats)
    for r in range(rep):
        out = acc_ref[:, cols[r]] * (1.0 / l[r])
        o_ref[0, :, r * HEAD_DIM:(r + 1) * HEAD_DIM] = out.T.astype(o_ref.dtype)


def gqa_attention(q, k, v, v_col0, rep, mode, sel=None, tq=128, tk=512):
    B, S, _ = q.shape
    G = k.shape[2] // HEAD_DIM
    tq, tk = _pick(S, tq), _pick(S, tk)
    assert tk % tq == 0
    w = rep * HEAD_DIM
    in_specs = [pl.BlockSpec((1, tq, w), lambda b, g, i: (b, i, g)),
                pl.BlockSpec((1, S, HEAD_DIM), lambda b, g, i: (b, 0, g)),
                pl.BlockSpec((1, S, HEAD_DIM), lambda b, g, i: (b, 0, v_col0 + g))]
    args = [q, k, v]
    scratch = [pltpu.VMEM((2, tk, rep * tq), F32), pltpu.VMEM((HEAD_DIM, rep * tq), F32)]
    if mode == "sel":
        assert S // SLC_LEN <= LANES and tk % SLC_LEN == 0
        in_specs += [pl.BlockSpec((1, 1, tq, LANES), lambda b, g, i: (b, g, i, 0))]
        args += [sel]
    return pl.pallas_call(
        functools.partial(_gqa_body, tq=tq, tk=tk, rep=rep, mode=mode),
        grid=(B, G, S // tq),
        in_specs=in_specs,
        out_specs=pl.BlockSpec((1, tq, w), lambda b, g, i: (b, i, g)),
        out_shape=jax.ShapeDtypeStruct((B, S, G * w), BF16),
        scratch_shapes=scratch,
        compiler_params=_cparams("parallel", "parallel", "arbitrary"),
        name="nsa_" + mode,
    )(*args)


def _nsa_gate_body(gl_ref, oc_ref, os_ref, ow_ref, o_ref, *, n_heads):
    ng = 3 * n_heads
    gl = gl_ref[:, :ng].astype(F32)
    gate = 1.0 / (1.0 + jnp.exp(-gl))
    src = lax.broadcasted_iota(jnp.int32, (ng, n_heads * HEAD_DIM), 0)
    head = lax.broadcasted_iota(jnp.int32, (ng, n_heads * HEAD_DIM), 1) // HEAD_DIM
    out = None
    for c, ref in enumerate((oc_ref, os_ref, ow_ref)):
        spread01 = jnp.where(src == head * 3 + c, 1.0, 0.0).astype(BF16)
        term = _split_dot(gate, spread01) * ref[...].astype(F32)
        out = term if out is None else out + term
    o_ref[...] = out.astype(o_ref.dtype)


def nsa_gate(gl, oc, os_, ow, n_heads, tm=256):
    M, W = oc.shape
    tm = _pick(M, tm)
    row = pl.BlockSpec((tm, W), lambda i: (i, 0))
    return pl.pallas_call(
        functools.partial(_nsa_gate_body, n_heads=n_heads),
        grid=(M // tm,),
        in_specs=[pl.BlockSpec((tm, gl.shape[1]), lambda i: (i, 0)), row, row, row],
        out_specs=row,
        out_shape=jax.ShapeDtypeStruct((M, W), BF16),
        compiler_params=_cparams("parallel"),
        name="nsa_gate",
    )(gl, oc, os_, ow)


def _xattn_body(h_ref, hb_ref, kv_ref, wq_ref, wo_ref, g_ref, b_ref, o32_ref, o16_ref, *, n_heads, scale, alpha):
    w = n_heads * HEAD_DIM
    q = jnp.dot(hb_ref[0], wq_ref[...], preferred_element_type=F32).astype(BF16)
    outs = []
    for hd in range(n_heads):
        sl = slice(hd * HEAD_DIM, (hd + 1) * HEAD_DIM)
        k = kv_ref[0, :, sl]
        v = kv_ref[0, :, w + hd * HEAD_DIM:w + (hd + 1) * HEAD_DIM]
        s = lax.dot_general(q[:, sl], k, _NT, preferred_element_type=F32) * scale
        p = jnp.exp(s - jnp.max(s, axis=-1, keepdims=True))
        p = p / jnp.sum(p, axis=-1, keepdims=True)
        outs.append(jnp.dot(p.astype(BF16), v, preferred_element_type=F32).astype(BF16))
    mix = jnp.dot(jnp.concatenate(outs, axis=1), wo_ref[...], preferred_element_type=F32)
    out = _layer_norm(alpha * h_ref[0] + mix, g_ref[...], b_ref[...])
    o32_ref[0] = out
    o16_ref[0] = out.astype(BF16)


def mem_attention_block(h, hb, kv, w_q, w_o, g, b, n_heads, alpha, tq=256):
    B, S, D = h.shape
    n_mem = kv.shape[1]
    w = n_heads * HEAD_DIM
    tq = _pick(S, tq)
    row = pl.BlockSpec((1, tq, D), lambda bb, i: (bb, i, 0))
    vec = pl.BlockSpec((1, D), lambda bb, i: (0, 0))
    return pl.pallas_call(
        functools.partial(_xattn_body, n_heads=n_heads, scale=HEAD_DIM ** -0.5, alpha=alpha),
        grid=(B, S // tq),
        in_specs=[row, row,
                  pl.BlockSpec((1, n_mem, 2 * w), lambda bb, i: (bb, 0, 0)),
                  pl.BlockSpec((D, w), lambda bb, i: (0, 0)),
                  pl.BlockSpec((w, D), lambda bb, i: (0, 0)), vec, vec],
        out_specs=[row, row],
        out_shape=[jax.ShapeDtypeStruct((B, S, D), F32), jax.ShapeDtypeStruct((B, S, D), BF16)],
        compiler_params=_cparams("parallel", "parallel"),
        name="mem_attention_block",
    )(h, hb, kv, w_q, w_o, g.reshape(1, D).astype(F32), b.reshape(1, D).astype(F32))


def _ffn_up_body(x_ref, wa_ref, wu_ref, cw_ref, o_ref, wab_ref, wub_ref, tail_ref, *, tiles_per_seq):
    i = pl.program_id(1)

    @pl.when(i == 0)
    def _():
        wab_ref[...] = wa_ref[0].astype(BF16)
        wub_ref[...] = wu_ref[0].astype(BF16)

    x = x_ref[...]
    a = jnp.dot(x, wab_ref[...], preferred_element_type=F32)
    tm = a.shape[0]
    first = (i % tiles_per_seq) == 0
    prev = jnp.where(first, 0.0, tail_ref[...])
    tail_ref[...] = a[tm - 8:]
    rowi = lax.broadcasted_iota(jnp.int32, a.shape, 0)
    a1 = jnp.where(rowi >= 1, pltpu.roll(a, 1, 0), prev[7:8])
    a2 = jnp.where(rowi >= 2, pltpu.roll(a, 2, 0), jnp.where(rowi == 1, prev[7:8], prev[6:7]))
    cw = cw_ref[...]
    gate = _gelu(cw[2:3] * a + cw[1:2] * a1 + cw[0:1] * a2)
    u = jnp.dot(x, wub_ref[...], preferred_element_type=F32)
    o_ref[...] = (gate * u).astype(o_ref.dtype)


def ffn_up_glu(x, w_up, layer, conv_w, seq_len, tm=1024, tn=256):
    M, K = x.shape
    Fd = w_up.shape[2] // 2
    tm, tn = _pick(seq_len, tm), _pick(Fd, tn)
    nj = Fd // tn
    return pl.pallas_call(
        functools.partial(_ffn_up_body, tiles_per_seq=seq_len // tm),
        grid=(nj, M // tm),
        in_specs=[pl.BlockSpec((tm, K), lambda j, i: (i, 0)),
                  pl.BlockSpec((1, K, tn), lambda j, i: (layer, 0, j)),
                  pl.BlockSpec((1, K, tn), lambda j, i: (layer, 0, j + nj)),
                  pl.BlockSpec((CONV_W, tn), lambda j, i: (0, j))],
        out_specs=pl.BlockSpec((tm, tn), lambda j, i: (i, j)),
        out_shape=jax.ShapeDtypeStruct((M, Fd), BF16),
        scratch_shapes=[pltpu.VMEM((K, tn), BF16), pltpu.VMEM((K, tn), BF16), pltpu.VMEM((8, tn), F32)],
        compiler_params=_cparams("parallel", "arbitrary"),
        name="ffn_up_glu",
    )(x, w_up, w_up, conv_w.astype(F32))


def _rope_tables(S):
    half = HEAD_DIM // 2
    inv_freq = ROPE_THETA ** (-jnp.arange(half, dtype=F32) / half)
    ang = jnp.arange(S, dtype=F32)[:, None] * inv_freq[None, :]
    cos, sin = jnp.cos(ang), jnp.sin(ang)
    return jnp.concatenate([cos, cos], axis=-1), jnp.concatenate([-sin, sin], axis=-1)


def _hgrn_sb_mixer(hb, B, S, w_in_all, lb_raw, norm_w, w_out, e):
    width = w_out.shape[0]
    a_heads = width // (2 * HEAD_DIM)
    b_heads = a_heads
    proj = matmul_f32w(hb, w_in_all, e).reshape(B, S, -1)
    o_a = hgrn2(proj, lb_raw, norm_w, a_heads, e)
    o_b = stick_breaking(proj, 4 * a_heads, 4 * a_heads + b_heads, 4 * a_heads + 2 * b_heads, b_heads)
    o = jnp.concatenate([o_a, o_b], axis=-1).reshape(B * S, width)
    return matmul(o, w_out.astype(BF16))


def _nsa_mixer(hb, B, S, w_in_all, o, cmp_pos, cmp_w1, cmp_w2, w_out, cos, sin):
    G = NSA_KV_HEADS
    q_w = w_out.shape[0]
    n_heads = q_w // HEAD_DIM
    rep = n_heads // G
    kv_w = G * HEAD_DIM
    main_w = q_w + 6 * kv_w
    proj = matmul_f32w(hb, w_in_all, o, n_cols=main_w).reshape(B, S, main_w)
    gl = matmul_f32w(hb, w_in_all, o, col0=main_w, n_cols=LANES, tn=LANES)

    log2_scale = HEAD_DIM ** -0.5 * LOG2E
    q_rot = rope(proj, 0, n_heads, cos, sin, mult=log2_scale)
    ks_rot = rope(proj, q_w + 2 * kv_w, G, cos, sin)
    kw_rot = rope(proj, q_w + 4 * kv_w, G, cos, sin)

    n16 = S // CMP_STRIDE
    kvc_in = proj[:, :, q_w:q_w + 2 * kv_w].reshape(B, S, 2, G, HEAD_DIM)
    x16 = kvc_in.transpose(2, 0, 3, 1, 4).reshape(2, B, G, n16, CMP_STRIDE * HEAD_DIM)
    pe = jnp.broadcast_to(cmp_pos.reshape(2, 1, CMP_LEN * HEAD_DIM), (2, 8, CMP_LEN * HEAD_DIM)).astype(BF16)
    w1 = cmp_w1.reshape(2, CMP_LEN * HEAD_DIM, HEAD_DIM).astype(BF16)
    kvc = compress(x16, pe, w1, cmp_w2.astype(BF16), log2_scale)

    o_c, imp = cmp_attention(proj, kvc[0], kvc[1], rep)
    sel = select_blocks(imp)
    col = lambda off: (q_w + off * kv_w) // HEAD_DIM
    o_s = gqa_attention(q_rot, ks_rot, proj, col(3), rep, "sel", sel=sel, tq=256, tk=512)
    o_w = gqa_attention(q_rot, kw_rot, proj, col(5), rep, "win", tq=256, tk=256)
    o = nsa_gate(gl, o_c.reshape(B * S, q_w), o_s.reshape(B * S, q_w), o_w.reshape(B * S, q_w), n_heads)
    return matmul(o, w_out.astype(BF16))


def kernel(x, mem, ab_w_in, hgrn_lb, hgrn_norm_w, ab_w_out, nsa_w_in, nsa_cmp_pos, nsa_cmp_w1,
           nsa_cmp_w2, nsa_w_out, xa_w_q, xa_w_kv, xa_w_o, ffn_w_up, ffn_conv, ffn_w_down, ln_g, ln_b):
    B, S, D = x.shape
    depth = ln_g.shape[0]
    alpha = (2 * depth) ** 0.25
    n_mem = mem.shape[1]
    cos, sin = _rope_tables(S)
    h = x.reshape(B * S, D).astype(F32)
    hb = h.astype(BF16)
    memb = mem.reshape(B * n_mem, D).astype(BF16)
    for layer in range(depth):
        if layer % 2 == 0:
            e = layer // 2
            mix = _hgrn_sb_mixer(hb, B, S, ab_w_in, hgrn_lb, hgrn_norm_w[e], ab_w_out[e], e)
        else:
            o = layer // 2
            mix = _nsa_mixer(hb, B, S, nsa_w_in, o, nsa_cmp_pos[o], nsa_cmp_w1[o], nsa_cmp_w2[o],
                             nsa_w_out[o], cos, sin)
        h, hb = add_layer_norm(h, mix, ln_g[layer, 0], ln_b[layer, 0], alpha)

        xkv = matmul(memb, xa_w_kv[layer].astype(BF16)).reshape(B, n_mem, -1)
        h3, hb3 = mem_attention_block(h.reshape(B, S, D), hb.reshape(B, S, D), xkv, xa_w_q[layer].astype(BF16),
                                      xa_w_o[layer].astype(BF16), ln_g[layer, 1], ln_b[layer, 1], XA_HEADS, alpha)
        h, hb = h3.reshape(B * S, D), hb3.reshape(B * S, D)

        gated = ffn_up_glu(hb, ffn_w_up, layer, ffn_conv[layer], S)
        h, hb = add_layer_norm(h, matmul(gated, ffn_w_down[layer].astype(BF16), tk=5504),
                               ln_g[layer, 2], ln_b[layer, 2], alpha)
    return h.reshape(B, S, D).astype(x.dtype)
```

```python
import functools

import jax
import jax.numpy as jnp
from jax import lax
from jax.experimental import pallas as pl
from jax.experimental.pallas import tpu as pltpu

F32 = jnp.float32
BF16 = jnp.bfloat16

HEAD_DIM = 128
LANES = 128
HGRN_SUB = 16
HGRN_SAFE_LOG_DECAY = -60.0
NSA_KV_HEADS = 4
CMP_LEN = 32
CMP_STRIDE = 16
SLC_LEN = 64
SLC_TOP = 16
WINDOW = 512
XA_HEADS = 4
CONV_W = 3
ROPE_THETA = 10000.0
LN_EPS = 1e-5
RMS_EPS = 1e-6
NEG_INF = -1e30
FORCE_SCORE = 1e9
EXP_ZERO_BELOW = -104.0
SOFTPLUS_CLAMP = 80.0
LOG2E = 1.4426950408889634
VMEM_LIMIT = 52 * 1024 * 1024

_NT = (((1,), (1,)), ((), ()))
_TN = (((0,), (0,)), ((), ()))


def _cparams(*sem):
    return pltpu.CompilerParams(dimension_semantics=sem, vmem_limit_bytes=VMEM_LIMIT)


def _split_dot(a, b01):
    hi = a.astype(BF16)
    lo = (a - hi.astype(F32)).astype(BF16)
    return (jnp.dot(hi, b01, preferred_element_type=F32)
            + jnp.dot(lo, b01, preferred_element_type=F32))


def _mm_body(x_ref, w_ref, o_ref, *scratch, nk):
    prod = jnp.dot(x_ref[...], w_ref[...], preferred_element_type=F32)
    if nk == 1:
        o_ref[...] = prod.astype(o_ref.dtype)
        return
    acc_ref, = scratch
    k = pl.program_id(2)

    @pl.when(k == 0)
    def _():
        acc_ref[...] = prod

    @pl.when(k > 0)
    def _():
        acc_ref[...] += prod

    @pl.when(k == nk - 1)
    def _():
        o_ref[...] = acc_ref[...].astype(o_ref.dtype)


def _pick(n, pref):
    if n <= pref:
        return n
    t = pref
    while t >= LANES:
        if n % t == 0:
            return t
        t -= LANES
    return n


def matmul(x, w, out_dtype=BF16, tm=1024, tn=512, tk=4096, col0=0, n_cols=None):
    M, K = x.shape
    N = w.shape[1] if n_cols is None else n_cols
    tm, tn, tk = _pick(M, tm), _pick(N, tn), _pick(K, tk)
    assert col0 % tn == 0
    cb = col0 // tn
    nk = K // tk
    scratch = [] if nk == 1 else [pltpu.VMEM((tm, tn), F32)]
    return pl.pallas_call(
        functools.partial(_mm_body, nk=nk),
        grid=(M // tm, N // tn, nk),
        in_specs=[pl.BlockSpec((tm, tk), lambda i, j, k: (i, k)),
                  pl.BlockSpec((tk, tn), lambda i, j, k: (k, j + cb))],
        out_specs=pl.BlockSpec((tm, tn), lambda i, j, k: (i, j)),
        out_shape=jax.ShapeDtypeStruct((M, N), out_dtype),
        scratch_shapes=scratch,
        compiler_params=_cparams("parallel", "parallel", "arbitrary"),
        name="matmul",
    )(x, w)


def _mm_f32w_body(x_ref, w_ref, o_ref, wb_ref):
    @pl.when(pl.program_id(1) == 0)
    def _():
        wb_ref[...] = w_ref[0].astype(BF16)

    o_ref[...] = jnp.dot(x_ref[...], wb_ref[...], preferred_element_type=F32).astype(o_ref.dtype)


def matmul_f32w(x, w, layer, out_dtype=BF16, tm=1024, tn=512, col0=0, n_cols=None):
    M, K = x.shape
    N = w.shape[2] if n_cols is None else n_cols
    tm, tn = _pick(M, tm), _pick(N, tn)
    assert col0 % tn == 0
    cb = col0 // tn
    return pl.pallas_call(
        _mm_f32w_body,
        grid=(N // tn, M // tm),
        in_specs=[pl.BlockSpec((tm, K), lambda j, i: (i, 0)),
                  pl.BlockSpec((1, K, tn), lambda j, i: (layer, 0, j + cb))],
        out_specs=pl.BlockSpec((tm, tn), lambda j, i: (i, j)),
        out_shape=jax.ShapeDtypeStruct((M, N), out_dtype),
        scratch_shapes=[pltpu.VMEM((K, tn), BF16)],
        compiler_params=_cparams("parallel", "arbitrary"),
        name="matmul_f32w",
    )(x, w)


def _layer_norm(y, g, b):
    mu = jnp.mean(y, axis=-1, keepdims=True)
    d = y - mu
    var = jnp.mean(d * d, axis=-1, keepdims=True)
    return d * lax.rsqrt(var + LN_EPS) * g + b


def _add_ln_body(h_ref, m_ref, g_ref, b_ref, o32_ref, o16_ref, *, alpha):
    out = _layer_norm(alpha * h_ref[...] + m_ref[...].astype(F32), g_ref[...], b_ref[...])
    o32_ref[...] = out
    o16_ref[...] = out.astype(BF16)


def add_layer_norm(h, mix, g, b, alpha, tm=256):
    M, D = h.shape
    tm = _pick(M, tm)
    row = pl.BlockSpec((tm, D), lambda i: (i, 0))
    vec = pl.BlockSpec((1, D), lambda i: (0, 0))
    return pl.pallas_call(
        functools.partial(_add_ln_body, alpha=alpha),
        grid=(M // tm,),
        in_specs=[row, row, vec, vec],
        out_specs=[row, row],
        out_shape=[jax.ShapeDtypeStruct((M, D), F32), jax.ShapeDtypeStruct((M, D), BF16)],
        compiler_params=_cparams("parallel"),
        name="add_layer_norm",
    )(h, mix, g.reshape(1, D).astype(F32), b.reshape(1, D).astype(F32))


def _hgrn_body(q_ref, f_ref, i_ref, g_ref, lb_ref, nw_ref, o_ref, st_ref, *, ts, layer_idx, nh):
    C = HGRN_SUB
    nsub = ts // C
    heads = [slice(h * HEAD_DIM, (h + 1) * HEAD_DIM) for h in range(nh)]

    @pl.when(pl.program_id(2) == 0)
    def _():
        st_ref[...] = jnp.zeros_like(st_ref)

    lbr = lb_ref[...]
    ex = jnp.exp(lbr - jnp.max(lbr, axis=0, keepdims=True))
    sm = ex / jnp.sum(ex, axis=0, keepdims=True)
    lb = jnp.sum(sm[:layer_idx + 1], axis=0, keepdims=True)

    q = q_ref[0].astype(F32)
    z = f_ref[0].astype(F32)
    v = i_ref[0]
    e = jnp.exp(-jnp.abs(z))
    r = 1.0 / (1.0 + e)
    pos = z >= 0
    sig = jnp.where(pos, r, e * r)
    nsig = jnp.where(pos, e * r, r)
    logf = jnp.log(lb + (1.0 - lb) * sig)
    k = (1.0 - lb) * nsig

    hi = logf.astype(BF16)
    lo = (logf - hi.astype(F32)).astype(BF16)
    row = lax.broadcasted_iota(jnp.int32, (ts, ts), 0)
    col = lax.broadcasted_iota(jnp.int32, (ts, ts), 1)

    def cumdot(m01):
        return jnp.dot(m01, hi, preferred_element_type=F32) + jnp.dot(m01, lo, preferred_element_type=F32)

    def whole_tile(sts):
        b = cumdot(jnp.where(row >= col, 1.0, 0.0).astype(BF16))
        bl = b[ts - 1:ts]
        qd = (q * jnp.exp(b)).astype(BF16)
        kinv = (k * jnp.exp(-b)).astype(BF16)
        kd = (k * jnp.exp(bl - b)).astype(BF16)
        dec = jnp.exp(bl)
        outs, new = [], []
        for h, sl in enumerate(heads):
            dmat = lax.dot_general(qd[:, sl], kinv[:, sl], _NT, preferred_element_type=F32)
            dmat = jnp.where(row >= col, dmat, 0.0)
            o = jnp.dot(dmat.astype(BF16), v[:, sl], preferred_element_type=F32)
            outs.append(o + lax.dot_general(qd[:, sl], sts[h].astype(BF16), _NT, preferred_element_type=F32))
            new.append(sts[h] * dec[:, sl] + lax.dot_general(v[:, sl], kd[:, sl], _TN, preferred_element_type=F32))
        return tuple(outs), tuple(new)

    def sub_chunks(sts):
        same = (row // C) == (col // C)
        b = cumdot(jnp.where(same & (row >= col), 1.0, 0.0).astype(BF16))
        bl = cumdot(jnp.where(same, 1.0, 0.0).astype(BF16))
        qd = (q * jnp.exp(b)).astype(BF16)
        kd = (k * jnp.exp(bl - b)).astype(BF16)
        dec = jnp.exp(bl)
        tri = (lax.broadcasted_iota(jnp.int32, (C, C, HEAD_DIM), 0)
               >= lax.broadcasted_iota(jnp.int32, (C, C, HEAD_DIM), 1))
        outs, new = [], []
        for h, hs in enumerate(heads):
            st = sts[h]
            parts = []
            for n in range(nsub):
                sl = slice(n * C, (n + 1) * C)
                bn, qn, kn = b[sl, hs], q[sl, hs], k[sl, hs]
                diff = bn[:, None, :] - bn[None, :, :]
                ee = jnp.exp(jnp.where(tri, diff, NEG_INF))
                dmat = jnp.sum(qn[:, None, :] * (kn[None, :, :] * ee), axis=-1)
                o_n = jnp.dot(dmat.astype(BF16), v[sl, hs], preferred_element_type=F32)
                o_n = o_n + lax.dot_general(qd[sl, hs], st.astype(BF16), _NT, preferred_element_type=F32)
                upd = lax.dot_general(v[sl, hs], kd[sl, hs], _TN, preferred_element_type=F32)
                st = st * dec[n * C:n * C + 1, hs] + upd
                parts.append(o_n)
            outs.append(jnp.concatenate(parts, axis=0))
            new.append(st)
        return tuple(outs), tuple(new)

    tile_decay = jnp.min(jnp.sum(logf, axis=0, keepdims=True))
    outs, sts = lax.cond(tile_decay > HGRN_SAFE_LOG_DECAY, whole_tile, sub_chunks,
                         tuple(st_ref[h] for h in range(nh)))
    gt = g_ref[0].astype(F32)
    gate = gt / (1.0 + jnp.exp(-gt))
    for h, sl in enumerate(heads):
        st_ref[h] = sts[h]
        o = outs[h]
        o = o * lax.rsqrt(jnp.mean(o * o, axis=-1, keepdims=True) + RMS_EPS) * nw_ref[...] * gate[:, sl]
        o_ref[0, :, sl] = o.astype(o_ref.dtype)


def hgrn2(proj, lb_raw, norm_w, n_heads, layer_idx, ts=128, nh=4):
    B, S, _ = proj.shape
    ts = _pick(S, ts)
    assert n_heads % nh == 0
    H = n_heads // nh
    L = lb_raw.shape[0]
    w = nh * HEAD_DIM

    def col(off):
        return pl.BlockSpec((1, ts, w), lambda b, h, s: (b, s, off * H + h))

    return pl.pallas_call(
        functools.partial(_hgrn_body, ts=ts, layer_idx=layer_idx, nh=nh),
        grid=(B, H, S // ts),
        in_specs=[col(0), col(1), col(2), col(3),
                  pl.BlockSpec((L, w), lambda b, h, s: (0, h)),
                  pl.BlockSpec((1, HEAD_DIM), lambda b, h, s: (0, 0))],
        out_specs=pl.BlockSpec((1, ts, w), lambda b, h, s: (b, s, h)),
        out_shape=jax.ShapeDtypeStruct((B, S, n_heads * HEAD_DIM), BF16),
        scratch_shapes=[pltpu.VMEM((nh, HEAD_DIM, HEAD_DIM), F32)],
        compiler_params=_cparams("parallel", "parallel", "arbitrary"),
        name="hgrn2",
    )(proj, proj, proj, proj, lb_raw.astype(F32), norm_w.reshape(1, HEAD_DIM).astype(F32))


def _sb_block(q, k, v, carry, after01, scale, mask):
    z = lax.dot_general(q, k, _NT, preferred_element_type=F32) * scale
    sp = jnp.maximum(z, jnp.log(1.0 + jnp.exp(jnp.minimum(z, SOFTPLUS_CLAMP))))
    spm = sp if mask is None else jnp.where(mask, sp, 0.0)
    rev = _split_dot(spm, after01)
    w = jnp.exp(z - sp - rev - carry)
    if mask is not None:
        w = jnp.where(mask, w, 0.0)
    contrib = jnp.dot(w.astype(BF16), v, preferred_element_type=F32)
    return contrib, carry + rev[:, 0:1] + spm[:, 0:1]


def _sb_body(q_ref, k_ref, v_ref, o_ref, *, tq, scale, nh):
    qi = pl.program_id(2)
    row = lax.broadcasted_iota(jnp.int32, (tq, tq), 0)
    col = lax.broadcasted_iota(jnp.int32, (tq, tq), 1)
    after01 = jnp.where(row > col, 1.0, 0.0).astype(BF16)
    heads = [slice(h * HEAD_DIM, (h + 1) * HEAD_DIM) for h in range(nh)]
    qs = [q_ref[0, :, sl] for sl in heads]

    def blocks(k0, carries, mask):
        res = [_sb_block(qs[h], k_ref[0, pl.ds(k0, tq), heads[h]], v_ref[0, pl.ds(k0, tq), heads[h]],
                         carries[h], after01, scale, mask) for h in range(nh)]
        return tuple(r[0] for r in res), tuple(r[1] for r in res)

    accs, carries = blocks(pl.multiple_of(qi * tq, tq), (jnp.zeros((tq, 1), F32),) * nh, col < row)

    def cond(c):
        j, _, carries = c
        low = functools.reduce(jnp.minimum, [jnp.min(cr) for cr in carries])
        return jnp.logical_and(j >= 0, low < -EXP_ZERO_BELOW)

    def body(c):
        j, accs, carries = c
        contribs, carries = blocks(pl.multiple_of(j * tq, tq), carries, None)
        return j - 1, tuple(a + cb for a, cb in zip(accs, contribs)), carries

    _, accs, _ = lax.while_loop(cond, body, (qi - 1, accs, carries))
    for h in range(nh):
        o_ref[0, :, heads[h]] = accs[h].astype(o_ref.dtype)


def stick_breaking(proj, col_q, col_k, col_v, n_heads, tq=256, nh=4):
    B, S, _ = proj.shape
    tq = _pick(S, tq)
    assert n_heads % nh == 0 and col_q % nh == 0 and col_k % nh == 0 and col_v % nh == 0
    w = nh * HEAD_DIM
    full = lambda off: pl.BlockSpec((1, S, w), lambda b, h, i: (b, 0, off // nh + h))
    return pl.pallas_call(
        functools.partial(_sb_body, tq=tq, scale=HEAD_DIM ** -0.5, nh=nh),
        grid=(B, n_heads // nh, S // tq),
        in_specs=[pl.BlockSpec((1, tq, w), lambda b, h, i: (b, i, col_q // nh + h)),
                  full(col_k), full(col_v)],
        out_specs=pl.BlockSpec((1, tq, w), lambda b, h, i: (b, i, h)),
        out_shape=jax.ShapeDtypeStruct((B, S, n_heads * HEAD_DIM), BF16),
        compiler_params=_cparams("parallel", "parallel", "arbitrary"),
        name="stick_breaking",
    )(proj, proj, proj)


def _rope_body(x_ref, cos_ref, sin_ref, o_ref, *, n_heads, mult):
    cos = cos_ref[...] * mult
    sin = sin_ref[...] * mult
    for h in range(n_heads):
        sl = slice(h * HEAD_DIM, (h + 1) * HEAD_DIM)
        t = x_ref[0, :, sl].astype(F32)
        o_ref[0, :, sl] = (t * cos + pltpu.roll(t, HEAD_DIM // 2, 1) * sin).astype(o_ref.dtype)


def rope(x, col0, n_heads, cos, sin, mult=1.0, ts=256):
    B, S, _ = x.shape
    ts = _pick(S, ts)
    w = n_heads * HEAD_DIM
    assert col0 % w == 0
    cb = col0 // w
    tab = pl.BlockSpec((ts, HEAD_DIM), lambda b, s: (s, 0))
    return pl.pallas_call(
        functools.partial(_rope_body, n_heads=n_heads, mult=mult),
        grid=(B, S // ts),
        in_specs=[pl.BlockSpec((1, ts, w), lambda b, s: (b, s, cb)), tab, tab],
        out_specs=pl.BlockSpec((1, ts, w), lambda b, s: (b, s, 0)),
        out_shape=jax.ShapeDtypeStruct((B, S, w), BF16),
        compiler_params=_cparams("parallel", "parallel"),
        name="rope",
    )(x, cos, sin)


def _gelu(x):
    return 0.5 * x * (1.0 + lax.erf(x * (2.0 ** -0.5)))


def _compress_body(x_ref, pe_ref, w1_ref, w2_ref, o_ref, *, batch, k_mult):
    x = x_ref[0, 0]
    half = x.shape[1]
    n16 = x.shape[0]
    y1 = jnp.dot(x, w1_ref[0, :half], preferred_element_type=F32)
    y2 = jnp.dot(x, w1_ref[0, half:], preferred_element_type=F32)
    bias = jnp.dot(pe_ref[0], w1_ref[0], preferred_element_type=F32)[0:1]
    hid = _gelu(y1 + pltpu.roll(y2, n16 - 1, 0) + bias)
    out = jnp.dot(hid.astype(BF16), w2_ref[0], preferred_element_type=F32)
    mult = jnp.where(pl.program_id(0) < batch, k_mult, 1.0)
    o_ref[0, 0] = (out * mult).astype(o_ref.dtype)


def compress(x16, pe, w1, w2, k_mult):
    two, B, G, n16, wide = x16.shape
    x16 = x16.reshape(two * B, G, n16, wide)
    out = pl.pallas_call(
        functools.partial(_compress_body, batch=B, k_mult=k_mult),
        grid=(two * B, G),
        in_specs=[pl.BlockSpec((1, 1, n16, wide), lambda i, g: (i, g, 0, 0)),
                  pl.BlockSpec((1, 8, 2 * wide), lambda i, g: (i // B, 0, 0)),
                  pl.BlockSpec((1, 2 * wide, HEAD_DIM), lambda i, g: (i // B, 0, 0)),
                  pl.BlockSpec((1, HEAD_DIM, HEAD_DIM), lambda i, g: (i // B, 0, 0))],
        out_specs=pl.BlockSpec((1, 1, n16, HEAD_DIM), lambda i, g: (i, g, 0, 0)),
        out_shape=jax.ShapeDtypeStruct((two * B, G, n16, HEAD_DIM), BF16),
        compiler_params=_cparams("parallel", "parallel"),
        name="nsa_compress",
    )(x16, pe, w1, w2)
    return out.reshape(two, B, G, n16, HEAD_DIM)


def _stack_heads(q_ref, rep):
    return jnp.concatenate([q_ref[0, :, r * HEAD_DIM:(r + 1) * HEAD_DIM] for r in range(rep)], axis=0)


def _cmp_body(q_ref, kc_ref, vc_ref, o_ref, imp_ref, *, tq, rep):
    q0 = pl.program_id(2) * tq
    n16 = kc_ref.shape[2]
    cols = [slice(r * tq, (r + 1) * tq) for r in range(rep)]
    q2 = _stack_heads(q_ref, rep)

    def attend(nb):
        st = lax.dot_general(kc_ref[0, 0, :nb], q2, _NT, preferred_element_type=F32)
        n = lax.broadcasted_iota(jnp.int32, (nb, tq), 0)
        t = q0 + lax.broadcasted_iota(jnp.int32, (nb, tq), 1)
        bias = jnp.where(n * CMP_STRIDE + (CMP_LEN - 1) <= t, 0.0, NEG_INF)
        ps = []
        psum = jnp.zeros((nb, tq), F32)
        for r in range(rep):
            s = st[:, cols[r]] + bias
            m = jnp.maximum(jnp.max(s, axis=0, keepdims=True), 0.1 * NEG_INF)
            p = jnp.exp2(s - m)
            den = jnp.sum(p, axis=0, keepdims=True)
            pn = p * (1.0 / jnp.where(den > 0.0, den, 1.0))
            ps.append(pn.astype(BF16))
            psum = psum + pn
        ot = lax.dot_general(vc_ref[0, 0, :nb], jnp.concatenate(ps, axis=1), _TN,
                             preferred_element_type=F32)
        for r in range(rep):
            o_ref[0, :, r * HEAD_DIM:(r + 1) * HEAD_DIM] = ot[:, cols[r]].T.astype(o_ref.dtype)

        cj = lax.broadcasted_iota(jnp.int32, (LANES, nb), 0) * SLC_LEN
        cn = lax.broadcasted_iota(jnp.int32, (LANES, nb), 1) * CMP_STRIDE
        ov01 = jnp.where((cn < cj + SLC_LEN) & (cn + CMP_LEN > cj), 1.0, 0.0).astype(BF16)
        hi = psum.astype(BF16)
        lo = (psum - hi.astype(F32)).astype(BF16)
        imp_t = jnp.dot(ov01, hi, preferred_element_type=F32) + jnp.dot(ov01, lo, preferred_element_type=F32)
        imp_ref[0, 0] = imp_t.T

    n_chunks = max(n16 // LANES, 1)
    chunk = n16 // n_chunks
    last_valid = (q0 + tq - CMP_LEN) // CMP_STRIDE
    need = jnp.clip(last_valid // chunk + 1, 1, n_chunks)
    for c in range(1, n_chunks + 1):
        pl.when(need == c)(functools.partial(attend, c * chunk))


def cmp_attention(q, kc, vc, rep, tq=128):
    B, S, _ = q.shape
    G, n16 = kc.shape[1], kc.shape[2]
    tq = _pick(S, tq)
    w = rep * HEAD_DIM
    kv = pl.BlockSpec((1, 1, n16, HEAD_DIM), lambda b, g, i: (b, g, 0, 0))
    return pl.pallas_call(
        functools.partial(_cmp_body, tq=tq, rep=rep),
        grid=(B, G, S // tq),
        in_specs=[pl.BlockSpec((1, tq, w), lambda b, g, i: (b, i, g)), kv, kv],
        out_specs=[pl.BlockSpec((1, tq, w), lambda b, g, i: (b, i, g)),
                   pl.BlockSpec((1, 1, tq, LANES), lambda b, g, i: (b, g, i, 0))],
        out_shape=[jax.ShapeDtypeStruct((B, S, G * w), BF16),
                   jax.ShapeDtypeStruct((B, G, S, LANES), F32)],
        compiler_params=_cparams("parallel", "parallel", "parallel"),
        name="nsa_cmp",
    )(q, kc, vc)


def _topk_body(imp_ref, sel_ref, *, tq, n_slc):
    q0 = pl.program_id(2) * tq
    imp = imp_ref[0, 0]
    tt = q0 + lax.broadcasted_iota(jnp.int32, (tq, LANES), 0)
    j = lax.broadcasted_iota(jnp.int32, (tq, LANES), 1)
    cur = tt // SLC_LEN
    forced = (j == 0) | (j == cur) | (j == cur - 1)
    allowed = j * SLC_LEN <= tt
    score = jnp.where(forced, FORCE_SCORE, jnp.where(allowed, imp, -1.0))
    score = jnp.where(j < n_slc, score, -jnp.inf)
    jf = j.astype(F32)
    sel = jnp.zeros((tq, LANES), F32)
    for _ in range(min(SLC_TOP, n_slc)):
        m = jnp.max(score, axis=-1, keepdims=True)
        first = jnp.min(jnp.where(score == m, jf, float(LANES)), axis=-1, keepdims=True)
        pick = jf == first
        sel = jnp.where(pick, 1.0, sel)
        score = jnp.where(pick, -jnp.inf, score)
    sel_ref[0, 0] = sel.astype(sel_ref.dtype)


def select_blocks(imp, tq=1024):
    B, G, S, _ = imp.shape
    n_slc = S // SLC_LEN
    assert n_slc <= LANES
    tq = _pick(S, tq)
    spec = pl.BlockSpec((1, 1, tq, LANES), lambda b, g, i: (b, g, i, 0))
    return pl.pallas_call(
        functools.partial(_topk_body, tq=tq, n_slc=n_slc),
        grid=(B, G, S // tq),
        in_specs=[spec],
        out_specs=spec,
        out_shape=jax.ShapeDtypeStruct((B, G, S, LANES), BF16),
        compiler_params=_cparams("parallel", "parallel", "parallel"),
        name="nsa_topk",
    )(imp)


def _gqa_body(*refs, tq, tk, rep, mode):
    q0 = pl.program_id(2) * tq
    if mode == "sel":
        q_ref, k_ref, v_ref, sel_ref, o_ref, s_ref, acc_ref = refs
        key_blk = lax.broadcasted_iota(jnp.int32, (tk, LANES), 0) // SLC_LEN
        lane_blk = lax.broadcasted_iota(jnp.int32, (tk, LANES), 1)
        unpicked = ((sel_ref[0, 0].astype(F32) - 1.0) * (-NEG_INF)).astype(BF16)
        q2 = jnp.concatenate([_stack_heads(q_ref, rep), jnp.concatenate([unpicked] * rep, axis=0)], axis=1)
    else:
        q_ref, k_ref, v_ref, o_ref, s_ref, acc_ref = refs
        q2 = _stack_heads(q_ref, rep)
    kpos = lax.broadcasted_iota(jnp.int32, (tk, tq), 0)
    t = q0 + lax.broadcasted_iota(jnp.int32, (tk, tq), 1)
    cols = [slice(r * tq, (r + 1) * tq) for r in range(rep)]

    def put_scores(slot, kj):
        k0 = pl.multiple_of(kj * tk, tk)
        keys = k_ref[0, pl.ds(k0, tk), :]
        if mode == "sel":
            onehot = jnp.where(key_blk + kj * (tk // SLC_LEN) == lane_blk, 1.0, 0.0).astype(BF16)
            keys = jnp.concatenate([keys, onehot], axis=1)
        s_ref[slot] = lax.dot_general(keys, q2, _NT, preferred_element_type=F32)

    def tile(slot, kj, m, l, diagonal):
        k0 = pl.multiple_of(kj * tk, tk)
        kp = k0 + kpos
        if mode == "sel":
            bias = jnp.where(kp <= t, 0.0, NEG_INF) if diagonal else None
        else:
            ok = kp > t - WINDOW
            if diagonal:
                ok = ok & (kp <= t)
            bias = jnp.where(ok, 0.0, NEG_INF)
        ps, m_new, l_new, scale = [], [], [], []
        for r in range(rep):
            s = s_ref[slot, :, cols[r]]
            if bias is None:
                mr = jnp.maximum(m[r], jnp.max(s, axis=0, keepdims=True))
                p = jnp.exp2(s - mr)
            else:
                mr = jnp.maximum(m[r], jnp.max(s + bias, axis=0, keepdims=True))
                p = jnp.exp2((s - mr) + bias)
            a = jnp.exp2(m[r] - mr)
            ps.append(p.astype(BF16))
            m_new.append(mr)
            l_new.append(a * l[r] + jnp.sum(p, axis=0, keepdims=True))
            scale.append(a)
        pt = jnp.concatenate(ps, axis=1)
        pv = lax.dot_general(v_ref[0, pl.ds(k0, tk), :], pt, _TN, preferred_element_type=F32)
        acc_ref[...] = jnp.concatenate(scale, axis=1) * acc_ref[...] + pv
        return tuple(m_new), tuple(l_new)

    kd = q0 // tk
    lo = 0 if mode == "sel" else jnp.maximum(q0 - (WINDOW - 1), 0) // tk
    put_scores(0, kd)
    put_scores(1, lo)
    acc_ref[...] = jnp.zeros_like(acc_ref)
    stats = tile(0, kd, (jnp.full((1, tq), NEG_INF, F32),) * rep, (jnp.zeros((1, tq), F32),) * rep, True)

    def pair(i, stats):
        kj = lo + 2 * i
        put_scores(0, kj + 1)
        stats = tile(1, kj, *stats, False)
        put_scores(1, kj + 2)
        return tile(0, kj + 1, *stats, False)

    n_off = kd - lo
    stats = lax.fori_loop(0, n_off // 2, pair, stats)
    m, l = lax.cond(n_off % 2 == 1, lambda st: tile(1, kd - 1, *st, False), lambda st: st, stats)
    for r in range(rep):
        out = acc_ref[:, cols[r]] * (1.0 / l[r])
        o_ref[0, :, r * HEAD_DIM:(r + 1) * HEAD_DIM] = out.T.astype(o_ref.dtype)


def gqa_attention(q, k, v, v_col0, rep, mode, sel=None, tq=128, tk=512):
    B, S, _ = q.shape
    G = k.shape[2] // HEAD_DIM
    tq, tk = _pick(S, tq), _pick(S, tk)
    assert tk % tq == 0
    w = rep * HEAD_DIM
    in_specs = [pl.BlockSpec((1, tq, w), lambda b, g, i: (b, i, g)),
                pl.BlockSpec((1, S, HEAD_DIM), lambda b, g, i: (b, 0, g)),
                pl.BlockSpec((1, S, HEAD_DIM), lambda b, g, i: (b, 0, v_col0 + g))]
    args = [q, k, v]
    scratch = [pltpu.VMEM((2, tk, rep * tq), F32), pltpu.VMEM((HEAD_DIM, rep * tq), F32)]
    if mode == "sel":
        assert S // SLC_LEN <= LANES and tk % SLC_LEN == 0
        in_specs += [pl.BlockSpec((1, 1, tq, LANES), lambda b, g, i: (b, g, i, 0))]
        args += [sel]
    return pl.pallas_call(
        functools.partial(_gqa_body, tq=tq, tk=tk, rep=rep, mode=mode),
        grid=(B, G, S // tq),
        in_specs=in_specs,
        out_specs=pl.BlockSpec((1, tq, w), lambda b, g, i: (b, i, g)),
        out_shape=jax.ShapeDtypeStruct((B, S, G * w), BF16),
        scratch_shapes=scratch,
        compiler_params=_cparams("parallel", "parallel", "arbitrary"),
        name="nsa_" + mode,
    )(*args)


def _nsa_gate_body(gl_ref, oc_ref, os_ref, ow_ref, o_ref, *, n_heads):
    ng = 3 * n_heads
    gl = gl_ref[:, :ng].astype(F32)
    gate = 1.0 / (1.0 + jnp.exp(-gl))
    src = lax.broadcasted_iota(jnp.int32, (ng, n_heads * HEAD_DIM), 0)
    head = lax.broadcasted_iota(jnp.int32, (ng, n_heads * HEAD_DIM), 1) // HEAD_DIM
    out = None
    for c, ref in enumerate((oc_ref, os_ref, ow_ref)):
        spread01 = jnp.where(src == head * 3 + c, 1.0, 0.0).astype(BF16)
        term = _split_dot(gate, spread01) * ref[...].astype(F32)
        out = term if out is None else out + term
    o_ref[...] = out.astype(o_ref.dtype)


def nsa_gate(gl, oc, os_, ow, n_heads, tm=256):
    M, W = oc.shape
    tm = _pick(M, tm)
    row = pl.BlockSpec((tm, W), lambda i: (i, 0))
    return pl.pallas_call(
        functools.partial(_nsa_gate_body, n_heads=n_heads),
        grid=(M // tm,),
        in_specs=[pl.BlockSpec((tm, gl.shape[1]), lambda i: (i, 0)), row, row, row],
        out_specs=row,
        out_shape=jax.ShapeDtypeStruct((M, W), BF16),
        compiler_params=_cparams("parallel"),
        name="nsa_gate",
    )(gl, oc, os_, ow)


def _xattn_body(h_ref, hb_ref, kv_ref, wq_ref, wo_ref, g_ref, b_ref, o32_ref, o16_ref, *, n_heads, scale, alpha):
    w = n_heads * HEAD_DIM
    q = jnp.dot(hb_ref[0], wq_ref[...], preferred_element_type=F32).astype(BF16)
    outs = []
    for hd in range(n_heads):
        sl = slice(hd * HEAD_DIM, (hd + 1) * HEAD_DIM)
        k = kv_ref[0, :, sl]
        v = kv_ref[0, :, w + hd * HEAD_DIM:w + (hd + 1) * HEAD_DIM]
        s = lax.dot_general(q[:, sl], k, _NT, preferred_element_type=F32) * scale
        p = jnp.exp(s - jnp.max(s, axis=-1, keepdims=True))
        p = p / jnp.sum(p, axis=-1, keepdims=True)
        outs.append(jnp.dot(p.astype(BF16), v, preferred_element_type=F32).astype(BF16))
    mix = jnp.dot(jnp.concatenate(outs, axis=1), wo_ref[...], preferred_element_type=F32)
    out = _layer_norm(alpha * h_ref[0] + mix, g_ref[...], b_ref[...])
    o32_ref[0] = out
    o16_ref[0] = out.astype(BF16)


def mem_attention_block(h, hb, kv, w_q, w_o, g, b, n_heads, alpha, tq=256):
    B, S, D = h.shape
    n_mem = kv.shape[1]
    w = n_heads * HEAD_DIM
    tq = _pick(S, tq)
    row = pl.BlockSpec((1, tq, D), lambda bb, i: (bb, i, 0))
    vec = pl.BlockSpec((1, D), lambda bb, i: (0, 0))
    return pl.pallas_call(
        functools.partial(_xattn_body, n_heads=n_heads, scale=HEAD_DIM ** -0.5, alpha=alpha),
        grid=(B, S // tq),
        in_specs=[row, row,
                  pl.BlockSpec((1, n_mem, 2 * w), lambda bb, i: (bb, 0, 0)),
                  pl.BlockSpec((D, w), lambda bb, i: (0, 0)),
                  pl.BlockSpec((w, D), lambda bb, i: (0, 0)), vec, vec],
        out_specs=[row, row],
        out_shape=[jax.ShapeDtypeStruct((B, S, D), F32), jax.ShapeDtypeStruct((B, S, D), BF16)],
        compiler_params=_cparams("parallel", "parallel"),
        name="mem_attention_block",
    )(h, hb, kv, w_q, w_o, g.reshape(1, D).astype(F32), b.reshape(1, D).astype(F32))


def _ffn_up_body(x_ref, wa_ref, wu_ref, cw_ref, o_ref, wab_ref, wub_ref, tail_ref, *, tiles_per_seq):
    i = pl.program_id(1)

    @pl.when(i == 0)
    def _():
        wab_ref[...] = wa_ref[0].astype(BF16)
        wub_ref[...] = wu_ref[0].astype(BF16)

    x = x_ref[...]
    a = jnp.dot(x, wab_ref[...], preferred_element_type=F32)
    tm = a.shape[0]
    first = (i % tiles_per_seq) == 0
    prev = jnp.where(first, 0.0, tail_ref[...])
    tail_ref[...] = a[tm - 8:]
    rowi = lax.broadcasted_iota(jnp.int32, a.shape, 0)
    a1 = jnp.where(rowi >= 1, pltpu.roll(a, 1, 0), prev[7:8])
    a2 = jnp.where(rowi >= 2, pltpu.roll(a, 2, 0), jnp.where(rowi == 1, prev[7:8], prev[6:7]))
    cw = cw_ref[...]
    gate = _gelu(cw[2:3] * a + cw[1:2] * a1 + cw[0:1] * a2)
    u = jnp.dot(x, wub_ref[...], preferred_element_type=F32)
    o_ref[...] = (gate * u).astype(o_ref.dtype)


def ffn_up_glu(x, w_up, layer, conv_w, seq_len, tm=1024, tn=256):
    M, K = x.shape
    Fd = w_up.shape[2] // 2
    tm, tn = _pick(seq_len, tm), _pick(Fd, tn)
    nj = Fd // tn
    return pl.pallas_call(
        functools.partial(_ffn_up_body, tiles_per_seq=seq_len // tm),
        grid=(nj, M // tm),
        in_specs=[pl.BlockSpec((tm, K), lambda j, i: (i, 0)),
                  pl.BlockSpec((1, K, tn), lambda j, i: (layer, 0, j)),
                  pl.BlockSpec((1, K, tn), lambda j, i: (layer, 0, j + nj)),
                  pl.BlockSpec((CONV_W, tn), lambda j, i: (0, j))],
        out_specs=pl.BlockSpec((tm, tn), lambda j, i: (i, j)),
        out_shape=jax.ShapeDtypeStruct((M, Fd), BF16),
        scratch_shapes=[pltpu.VMEM((K, tn), BF16), pltpu.VMEM((K, tn), BF16), pltpu.VMEM((8, tn), F32)],
        compiler_params=_cparams("parallel", "arbitrary"),
        name="ffn_up_glu",
    )(x, w_up, w_up, conv_w.astype(F32))


def _rope_tables(S):
    half = HEAD_DIM // 2
    inv_freq = ROPE_THETA ** (-jnp.arange(half, dtype=F32) / half)
    ang = jnp.arange(S, dtype=F32)[:, None] * inv_freq[None, :]
    cos, sin = jnp.cos(ang), jnp.sin(ang)
    return jnp.concatenate([cos, cos], axis=-1), jnp.concatenate([-sin, sin], axis=-1)


def _hgrn_sb_mixer(hb, B, S, w_in_all, lb_raw, norm_w, w_out, e):
    width = w_out.shape[0]
    a_heads = width // (2 * HEAD_DIM)
    b_heads = a_heads
    proj = matmul_f32w(hb, w_in_all, e).reshape(B, S, -1)
    o_a = hgrn2(proj, lb_raw, norm_w, a_heads, e)
    o_b = stick_breaking(proj, 4 * a_heads, 4 * a_heads + b_heads, 4 * a_heads + 2 * b_heads, b_heads)
    o = jnp.concatenate([o_a, o_b], axis=-1).reshape(B * S, width)
    return matmul(o, w_out.astype(BF16))


def _nsa_mixer(hb, B, S, w_in_all, o, cmp_pos, cmp_w1, cmp_w2, w_out, cos, sin):
    G = NSA_KV_HEADS
    q_w = w_out.shape[0]
    n_heads = q_w // HEAD_DIM
    rep = n_heads // G
    kv_w = G * HEAD_DIM
    main_w = q_w + 6 * kv_w
    proj = matmul_f32w(hb, w_in_all, o, n_cols=main_w).reshape(B, S, main_w)
    gl = matmul_f32w(hb, w_in_all, o, col0=main_w, n_cols=LANES, tn=LANES)

    log2_scale = HEAD_DIM ** -0.5 * LOG2E
    q_rot = rope(proj, 0, n_heads, cos, sin, mult=log2_scale)
    ks_rot = rope(proj, q_w + 2 * kv_w, G, cos, sin)
    kw_rot = rope(proj, q_w + 4 * kv_w, G, cos, sin)

    n16 = S // CMP_STRIDE
    kvc_in = proj[:, :, q_w:q_w + 2 * kv_w].reshape(B, S, 2, G, HEAD_DIM)
    x16 = kvc_in.transpose(2, 0, 3, 1, 4).reshape(2, B, G, n16, CMP_STRIDE * HEAD_DIM)
    pe = jnp.broadcast_to(cmp_pos.reshape(2, 1, CMP_LEN * HEAD_DIM), (2, 8, CMP_LEN * HEAD_DIM)).astype(BF16)
    w1 = cmp_w1.reshape(2, CMP_LEN * HEAD_DIM, HEAD_DIM).astype(BF16)
    kvc = compress(x16, pe, w1, cmp_w2.astype(BF16), log2_scale)

    o_c, imp = cmp_attention(proj, kvc[0], kvc[1], rep)
    sel = select_blocks(imp)
    col = lambda off: (q_w + off * kv_w) // HEAD_DIM
    o_s = gqa_attention(q_rot, ks_rot, proj, col(3), rep, "sel", sel=sel, tq=256, tk=512)
    o_w = gqa_attention(q_rot, kw_rot, proj, col(5), rep, "win", tq=256, tk=256)
    o = nsa_gate(gl, o_c.reshape(B * S, q_w), o_s.reshape(B * S, q_w), o_w.reshape(B * S, q_w), n_heads)
    return matmul(o, w_out.astype(BF16))


def kernel(x, mem, ab_w_in, hgrn_lb, hgrn_norm_w, ab_w_out, nsa_w_in, nsa_cmp_pos, nsa_cmp_w1,
           nsa_cmp_w2, nsa_w_out, xa_w_q, xa_w_kv, xa_w_o, ffn_w_up, ffn_conv, ffn_w_down, ln_g, ln_b):
    B, S, D = x.shape
    depth = ln_g.shape[0]
    alpha = (2 * depth) ** 0.25
    n_mem = mem.shape[1]
    cos, sin = _rope_tables(S)
    h = x.reshape(B * S, D).astype(F32)
    hb = h.astype(BF16)
    memb = mem.reshape(B * n_mem, D).astype(BF16)
    for layer in range(depth):
        if layer % 2 == 0:
            e = layer // 2
            mix = _hgrn_sb_mixer(hb, B, S, ab_w_in, hgrn_lb, hgrn_norm_w[e], ab_w_out[e], e)
        else:
            o = layer // 2
            mix = _nsa_mixer(hb, B, S, nsa_w_in, o, nsa_cmp_pos[o], nsa_cmp_w1[o], nsa_cmp_w2[o],
                             nsa_w_out[o], cos, sin)
        h, hb = add_layer_norm(h, mix, ln_g[layer, 0], ln_b[layer, 0], alpha)

        xkv = matmul(memb, xa_w_kv[layer].astype(BF16)).reshape(B, n_mem, -1)
        h3, hb3 = mem_attention_block(h.reshape(B, S, D), hb.reshape(B, S, D), xkv, xa_w_q[layer].astype(BF16),
                                      xa_w_o[layer].astype(BF16), ln_g[layer, 1], ln_b[layer, 1], XA_HEADS, alpha)
        h, hb = h3.reshape(B * S, D), hb3.reshape(B * S, D)

        gated = ffn_up_glu(hb, ffn_w_up, layer, ffn_conv[layer], S)
        h, hb = add_layer_norm(h, matmul(gated, ffn_w_down[layer].astype(BF16), tk=5504),
                               ln_g[layer, 2], ln_b[layer, 2], alpha)
    return h.reshape(B, S, D).astype(x.dtype)
```

```python
import functools

import jax
import jax.numpy as jnp
from jax import lax
from jax.experimental import pallas as pl
from jax.experimental.pallas import tpu as pltpu

F32 = jnp.float32
BF16 = jnp.bfloat16

HEAD_DIM = 128
LANES = 128
HGRN_SUB = 16
HGRN_SAFE_LOG_DECAY = -60.0
NSA_KV_HEADS = 4
CMP_LEN = 32
CMP_STRIDE = 16
SLC_LEN = 64
SLC_TOP = 16
WINDOW = 512
XA_HEADS = 4
CONV_W = 3
ROPE_THETA = 10000.0
LN_EPS = 1e-5
RMS_EPS = 1e-6
NEG_INF = -1e30
FORCE_SCORE = 1e9
EXP_ZERO_BELOW = -104.0
SOFTPLUS_CLAMP = 80.0
LOG2E = 1.4426950408889634
VMEM_LIMIT = 52 * 1024 * 1024

_NT = (((1,), (1,)), ((), ()))
_TN = (((0,), (0,)), ((), ()))


def _cparams(*sem):
    return pltpu.CompilerParams(dimension_semantics=sem, vmem_limit_bytes=VMEM_LIMIT)


def _split_dot(a, b01):
    hi = a.astype(BF16)
    lo = (a - hi.astype(F32)).astype(BF16)
    return (jnp.dot(hi, b01, preferred_element_type=F32)
            + jnp.dot(lo, b01, preferred_element_type=F32))


def _mm_body(x_ref, w_ref, o_ref, *scratch, nk):
    prod = jnp.dot(x_ref[...], w_ref[...], preferred_element_type=F32)
    if nk == 1:
        o_ref[...] = prod.astype(o_ref.dtype)
        return
    acc_ref, = scratch
    k = pl.program_id(2)

    @pl.when(k == 0)
    def _():
        acc_ref[...] = prod

    @pl.when(k > 0)
    def _():
        acc_ref[...] += prod

    @pl.when(k == nk - 1)
    def _():
        o_ref[...] = acc_ref[...].astype(o_ref.dtype)


def _pick(n, pref):
    if n <= pref:
        return n
    t = pref
    while t >= LANES:
        if n % t == 0:
            return t
        t -= LANES
    return n


def matmul(x, w, out_dtype=BF16, tm=1024, tn=512, tk=4096, col0=0, n_cols=None):
    M, K = x.shape
    N = w.shape[1] if n_cols is None else n_cols
    tm, tn, tk = _pick(M, tm), _pick(N, tn), _pick(K, tk)
    assert col0 % tn == 0
    cb = col0 // tn
    nk = K // tk
    scratch = [] if nk == 1 else [pltpu.VMEM((tm, tn), F32)]
    return pl.pallas_call(
        functools.partial(_mm_body, nk=nk),
        grid=(M // tm, N // tn, nk),
        in_specs=[pl.BlockSpec((tm, tk), lambda i, j, k: (i, k)),
                  pl.BlockSpec((tk, tn), lambda i, j, k: (k, j + cb))],
        out_specs=pl.BlockSpec((tm, tn), lambda i, j, k: (i, j)),
        out_shape=jax.ShapeDtypeStruct((M, N), out_dtype),
        scratch_shapes=scratch,
        compiler_params=_cparams("parallel", "parallel", "arbitrary"),
        name="matmul",
    )(x, w)


def _mm_f32w_body(x_ref, w_ref, o_ref, wb_ref):
    @pl.when(pl.program_id(1) == 0)
    def _():
        wb_ref[...] = w_ref[0].astype(BF16)

    o_ref[...] = jnp.dot(x_ref[...], wb_ref[...], preferred_element_type=F32).astype(o_ref.dtype)


def matmul_f32w(x, w, layer, out_dtype=BF16, tm=1024, tn=512, col0=0, n_cols=None):
    M, K = x.shape
    N = w.shape[2] if n_cols is None else n_cols
    tm, tn = _pick(M, tm), _pick(N, tn)
    assert col0 % tn == 0
    cb = col0 // tn
    return pl.pallas_call(
        _mm_f32w_body,
        grid=(N // tn, M // tm),
        in_specs=[pl.BlockSpec((tm, K), lambda j, i: (i, 0)),
                  pl.BlockSpec((1, K, tn), lambda j, i: (layer, 0, j + cb))],
        out_specs=pl.BlockSpec((tm, tn), lambda j, i: (i, j)),
        out_shape=jax.ShapeDtypeStruct((M, N), out_dtype),
        scratch_shapes=[pltpu.VMEM((K, tn), BF16)],
        compiler_params=_cparams("parallel", "arbitrary"),
        name="matmul_f32w",
    )(x, w)


def _layer_norm(y, g, b):
    mu = jnp.mean(y, axis=-1, keepdims=True)
    d = y - mu
    var = jnp.mean(d * d, axis=-1, keepdims=True)
    return d * lax.rsqrt(var + LN_EPS) * g + b


def _add_ln_body(h_ref, m_ref, g_ref, b_ref, o32_ref, o16_ref, *, alpha):
    out = _layer_norm(alpha * h_ref[...] + m_ref[...].astype(F32), g_ref[...], b_ref[...])
    o32_ref[...] = out
    o16_ref[...] = out.astype(BF16)


def add_layer_norm(h, mix, g, b, alpha, tm=256):
    M, D = h.shape
    tm = _pick(M, tm)
    row = pl.BlockSpec((tm, D), lambda i: (i, 0))
    vec = pl.BlockSpec((1, D), lambda i: (0, 0))
    return pl.pallas_call(
        functools.partial(_add_ln_body, alpha=alpha),
        grid=(M // tm,),
        in_specs=[row, row, vec, vec],
        out_specs=[row, row],
        out_shape=[jax.ShapeDtypeStruct((M, D), F32), jax.ShapeDtypeStruct((M, D), BF16)],
        compiler_params=_cparams("parallel"),
        name="add_layer_norm",
    )(h, mix, g.reshape(1, D).astype(F32), b.reshape(1, D).astype(F32))


def _hgrn_body(q_ref, f_ref, i_ref, g_ref, lb_ref, nw_ref, o_ref, st_ref, *, ts, layer_idx, nh):
    C = HGRN_SUB
    nsub = ts // C
    heads = [slice(h * HEAD_DIM, (h + 1) * HEAD_DIM) for h in range(nh)]

    @pl.when(pl.program_id(2) == 0)
    def _():
        st_ref[...] = jnp.zeros_like(st_ref)

    lbr = lb_ref[...]
    ex = jnp.exp(lbr - jnp.max(lbr, axis=0, keepdims=True))
    sm = ex / jnp.sum(ex, axis=0, keepdims=True)
    lb = jnp.sum(sm[:layer_idx + 1], axis=0, keepdims=True)

    q = q_ref[0].astype(F32)
    z = f_ref[0].astype(F32)
    v = i_ref[0]
    e = jnp.exp(-jnp.abs(z))
    r = 1.0 / (1.0 + e)
    pos = z >= 0
    sig = jnp.where(pos, r, e * r)
    nsig = jnp.where(pos, e * r, r)
    logf = jnp.log(lb + (1.0 - lb) * sig)
    k = (1.0 - lb) * nsig

    hi = logf.astype(BF16)
    lo = (logf - hi.astype(F32)).astype(BF16)
    row = lax.broadcasted_iota(jnp.int32, (ts, ts), 0)
    col = lax.broadcasted_iota(jnp.int32, (ts, ts), 1)

    def cumdot(m01):
        return jnp.dot(m01, hi, preferred_element_type=F32) + jnp.dot(m01, lo, preferred_element_type=F32)

    def whole_tile(sts):
        b = cumdot(jnp.where(row >= col, 1.0, 0.0).astype(BF16))
        bl = b[ts - 1:ts]
        qd = (q * jnp.exp(b)).astype(BF16)
        kinv = (k * jnp.exp(-b)).astype(BF16)
        kd = (k * jnp.exp(bl - b)).astype(BF16)
        dec = jnp.exp(bl)
        outs, new = [], []
        for h, sl in enumerate(heads):
            dmat = lax.dot_general(qd[:, sl], kinv[:, sl], _NT, preferred_element_type=F32)
            dmat = jnp.where(row >= col, dmat, 0.0)
            o = jnp.dot(dmat.astype(BF16), v[:, sl], preferred_element_type=F32)
            outs.append(o + lax.dot_general(qd[:, sl], sts[h].astype(BF16), _NT, preferred_element_type=F32))
            new.append(sts[h] * dec[:, sl] + lax.dot_general(v[:, sl], kd[:, sl], _TN, preferred_element_type=F32))
        return tuple(outs), tuple(new)

    def sub_chunks(sts):
        same = (row // C) == (col // C)
        b = cumdot(jnp.where(same & (row >= col), 1.0, 0.0).astype(BF16))
        bl = cumdot(jnp.where(same, 1.0, 0.0).astype(BF16))
        qd = (q * jnp.exp(b)).astype(BF16)
        kd = (k * jnp.exp(bl - b)).astype(BF16)
        dec = jnp.exp(bl)
        tri = (lax.broadcasted_iota(jnp.int32, (C, C, HEAD_DIM), 0)
               >= lax.broadcasted_iota(jnp.int32, (C, C, HEAD_DIM), 1))
        outs, new = [], []
        for h, hs in enumerate(heads):
            st = sts[h]
            parts = []
            for n in range(nsub):
                sl = slice(n * C, (n + 1) * C)
                bn, qn, kn = b[sl, hs], q[sl, hs], k[sl, hs]
                diff = bn[:, None, :] - bn[None, :, :]
                ee = jnp.exp(jnp.where(tri, diff, NEG_INF))
                dmat = jnp.sum(qn[:, None, :] * (kn[None, :, :] * ee), axis=-1)
                o_n = jnp.dot(dmat.astype(BF16), v[sl, hs], preferred_element_type=F32)
                o_n = o_n + lax.dot_general(qd[sl, hs], st.astype(BF16), _NT, preferred_element_type=F32)
                upd = lax.dot_general(v[sl, hs], kd[sl, hs], _TN, preferred_element_type=F32)
                st = st * dec[n * C:n * C + 1, hs] + upd
                parts.append(o_n)
            outs.append(jnp.concatenate(parts, axis=0))
            new.append(st)
        return tuple(outs), tuple(new)

    tile_decay = jnp.min(jnp.sum(logf, axis=0, keepdims=True))
    outs, sts = lax.cond(tile_decay > HGRN_SAFE_LOG_DECAY, whole_tile, sub_chunks,
                         tuple(st_ref[h] for h in range(nh)))
    gt = g_ref[0].astype(F32)
    gate = gt / (1.0 + jnp.exp(-gt))
    for h, sl in enumerate(heads):
        st_ref[h] = sts[h]
        o = outs[h]
        o = o * lax.rsqrt(jnp.mean(o * o, axis=-1, keepdims=True) + RMS_EPS) * nw_ref[...] * gate[:, sl]
        o_ref[0, :, sl] = o.astype(o_ref.dtype)


def hgrn2(proj, lb_raw, norm_w, n_heads, layer_idx, ts=128, nh=8):
    B, S, _ = proj.shape
    ts = _pick(S, ts)
    nh = min(nh, n_heads)
    assert n_heads % nh == 0
    H = n_heads // nh
    L = lb_raw.shape[0]
    w = nh * HEAD_DIM

    def col(off):
        return pl.BlockSpec((1, ts, w), lambda b, h, s: (b, s, off * H + h))

    return pl.pallas_call(
        functools.partial(_hgrn_body, ts=ts, layer_idx=layer_idx, nh=nh),
        grid=(B, H, S // ts),
        in_specs=[col(0), col(1), col(2), col(3),
                  pl.BlockSpec((L, w), lambda b, h, s: (0, h)),
                  pl.BlockSpec((1, HEAD_DIM), lambda b, h, s: (0, 0))],
        out_specs=pl.BlockSpec((1, ts, w), lambda b, h, s: (b, s, h)),
        out_shape=jax.ShapeDtypeStruct((B, S, n_heads * HEAD_DIM), BF16),
        scratch_shapes=[pltpu.VMEM((nh, HEAD_DIM, HEAD_DIM), F32)],
        compiler_params=_cparams("parallel", "parallel", "arbitrary"),
        name="hgrn2",
    )(proj, proj, proj, proj, lb_raw.astype(F32), norm_w.reshape(1, HEAD_DIM).astype(F32))


def _sb_block(q, k, v, carry, after01, scale, mask):
    z = lax.dot_general(q, k, _NT, preferred_element_type=F32) * scale
    sp = jnp.maximum(z, jnp.log(1.0 + jnp.exp(jnp.minimum(z, SOFTPLUS_CLAMP))))
    spm = sp if mask is None else jnp.where(mask, sp, 0.0)
    rev = _split_dot(spm, after01)
    w = jnp.exp(z - sp - rev - carry)
    if mask is not None:
        w = jnp.where(mask, w, 0.0)
    contrib = jnp.dot(w.astype(BF16), v, preferred_element_type=F32)
    return contrib, carry + rev[:, 0:1] + spm[:, 0:1]


def _sb_body(q_ref, k_ref, v_ref, o_ref, *, tq, scale, nh):
    qi = pl.program_id(2)
    row = lax.broadcasted_iota(jnp.int32, (tq, tq), 0)
    col = lax.broadcasted_iota(jnp.int32, (tq, tq), 1)
    after01 = jnp.where(row > col, 1.0, 0.0).astype(BF16)
    heads = [slice(h * HEAD_DIM, (h + 1) * HEAD_DIM) for h in range(nh)]
    qs = [q_ref[0, :, sl] for sl in heads]

    def blocks(k0, carries, mask):
        res = [_sb_block(qs[h], k_ref[0, pl.ds(k0, tq), heads[h]], v_ref[0, pl.ds(k0, tq), heads[h]],
                         carries[h], after01, scale, mask) for h in range(nh)]
        return tuple(r[0] for r in res), tuple(r[1] for r in res)

    accs, carries = blocks(pl.multiple_of(qi * tq, tq), (jnp.zeros((tq, 1), F32),) * nh, col < row)

    def cond(c):
        j, _, carries = c
        low = functools.reduce(jnp.minimum, [jnp.min(cr) for cr in carries])
        return jnp.logical_and(j >= 0, low < -EXP_ZERO_BELOW)

    def body(c):
        j, accs, carries = c
        contribs, carries = blocks(pl.multiple_of(j * tq, tq), carries, None)
        return j - 1, tuple(a + cb for a, cb in zip(accs, contribs)), carries

    _, accs, _ = lax.while_loop(cond, body, (qi - 1, accs, carries))
    for h in range(nh):
        o_ref[0, :, heads[h]] = accs[h].astype(o_ref.dtype)


def stick_breaking(proj, col_q, col_k, col_v, n_heads, tq=256, nh=4):
    B, S, _ = proj.shape
    tq = _pick(S, tq)
    assert n_heads % nh == 0 and col_q % nh == 0 and col_k % nh == 0 and col_v % nh == 0
    w = nh * HEAD_DIM
    full = lambda off: pl.BlockSpec((1, S, w), lambda b, h, i: (b, 0, off // nh + h))
    return pl.pallas_call(
        functools.partial(_sb_body, tq=tq, scale=HEAD_DIM ** -0.5, nh=nh),
        grid=(B, n_heads // nh, S // tq),
        in_specs=[pl.BlockSpec((1, tq, w), lambda b, h, i: (b, i, col_q // nh + h)),
                  full(col_k), full(col_v)],
        out_specs=pl.BlockSpec((1, tq, w), lambda b, h, i: (b, i, h)),
        out_shape=jax.ShapeDtypeStruct((B, S, n_heads * HEAD_DIM), BF16),
        compiler_params=_cparams("parallel", "parallel", "arbitrary"),
        name="stick_breaking",
    )(proj, proj, proj)


def _rope_body(x_ref, cos_ref, sin_ref, o_ref, *, n_heads, mult):
    cos = cos_ref[...] * mult
    sin = sin_ref[...] * mult
    for h in range(n_heads):
        sl = slice(h * HEAD_DIM, (h + 1) * HEAD_DIM)
        t = x_ref[0, :, sl].astype(F32)
        o_ref[0, :, sl] = (t * cos + pltpu.roll(t, HEAD_DIM // 2, 1) * sin).astype(o_ref.dtype)


def rope(x, col0, n_heads, cos, sin, mult=1.0, ts=256):
    B, S, _ = x.shape
    ts = _pick(S, ts)
    w = n_heads * HEAD_DIM
    assert col0 % w == 0
    cb = col0 // w
    tab = pl.BlockSpec((ts, HEAD_DIM), lambda b, s: (s, 0))
    return pl.pallas_call(
        functools.partial(_rope_body, n_heads=n_heads, mult=mult),
        grid=(B, S // ts),
        in_specs=[pl.BlockSpec((1, ts, w), lambda b, s: (b, s, cb)), tab, tab],
        out_specs=pl.BlockSpec((1, ts, w), lambda b, s: (b, s, 0)),
        out_shape=jax.ShapeDtypeStruct((B, S, w), BF16),
        compiler_params=_cparams("parallel", "parallel"),
        name="rope",
    )(x, cos, sin)


def _gelu(x):
    return 0.5 * x * (1.0 + lax.erf(x * (2.0 ** -0.5)))


def _compress_body(x_ref, pe_ref, w1_ref, w2_ref, o_ref, *, batch, k_mult):
    x = x_ref[0, 0]
    half = x.shape[1]
    n16 = x.shape[0]
    y1 = jnp.dot(x, w1_ref[0, :half], preferred_element_type=F32)
    y2 = jnp.dot(x, w1_ref[0, half:], preferred_element_type=F32)
    bias = jnp.dot(pe_ref[0], w1_ref[0], preferred_element_type=F32)[0:1]
    hid = _gelu(y1 + pltpu.roll(y2, n16 - 1, 0) + bias)
    out = jnp.dot(hid.astype(BF16), w2_ref[0], preferred_element_type=F32)
    mult = jnp.where(pl.program_id(0) < batch, k_mult, 1.0)
    o_ref[0, 0] = (out * mult).astype(o_ref.dtype)


def compress(x16, pe, w1, w2, k_mult):
    two, B, G, n16, wide = x16.shape
    x16 = x16.reshape(two * B, G, n16, wide)
    out = pl.pallas_call(
        functools.partial(_compress_body, batch=B, k_mult=k_mult),
        grid=(two * B, G),
        in_specs=[pl.BlockSpec((1, 1, n16, wide), lambda i, g: (i, g, 0, 0)),
                  pl.BlockSpec((1, 8, 2 * wide), lambda i, g: (i // B, 0, 0)),
                  pl.BlockSpec((1, 2 * wide, HEAD_DIM), lambda i, g: (i // B, 0, 0)),
                  pl.BlockSpec((1, HEAD_DIM, HEAD_DIM), lambda i, g: (i // B, 0, 0))],
        out_specs=pl.BlockSpec((1, 1, n16, HEAD_DIM), lambda i, g: (i, g, 0, 0)),
        out_shape=jax.ShapeDtypeStruct((two * B, G, n16, HEAD_DIM), BF16),
        compiler_params=_cparams("parallel", "parallel"),
        name="nsa_compress",
    )(x16, pe, w1, w2)
    return out.reshape(two, B, G, n16, HEAD_DIM)


def _stack_heads(q_ref, rep):
    return jnp.concatenate([q_ref[0, :, r * HEAD_DIM:(r + 1) * HEAD_DIM] for r in range(rep)], axis=0)


def _cmp_body(q_ref, kc_ref, vc_ref, o_ref, imp_ref, *, tq, rep):
    q0 = pl.program_id(2) * tq
    n16 = kc_ref.shape[2]
    cols = [slice(r * tq, (r + 1) * tq) for r in range(rep)]
    q2 = _stack_heads(q_ref, rep)

    def attend(nb):
        st = lax.dot_general(kc_ref[0, 0, :nb], q2, _NT, preferred_element_type=F32)
        n = lax.broadcasted_iota(jnp.int32, (nb, tq), 0)
        t = q0 + lax.broadcasted_iota(jnp.int32, (nb, tq), 1)
        bias = jnp.where(n * CMP_STRIDE + (CMP_LEN - 1) <= t, 0.0, NEG_INF)
        ps = []
        psum = jnp.zeros((nb, tq), F32)
        for r in range(rep):
            s = st[:, cols[r]] + bias
            m = jnp.maximum(jnp.max(s, axis=0, keepdims=True), 0.1 * NEG_INF)
            p = jnp.exp2(s - m)
            den = jnp.sum(p, axis=0, keepdims=True)
            pn = p * (1.0 / jnp.where(den > 0.0, den, 1.0))
            ps.append(pn.astype(BF16))
            psum = psum + pn
        ot = lax.dot_general(vc_ref[0, 0, :nb], jnp.concatenate(ps, axis=1), _TN,
                             preferred_element_type=F32)
        for r in range(rep):
            o_ref[0, :, r * HEAD_DIM:(r + 1) * HEAD_DIM] = ot[:, cols[r]].T.astype(o_ref.dtype)

        cj = lax.broadcasted_iota(jnp.int32, (LANES, nb), 0) * SLC_LEN
        cn = lax.broadcasted_iota(jnp.int32, (LANES, nb), 1) * CMP_STRIDE
        ov01 = jnp.where((cn < cj + SLC_LEN) & (cn + CMP_LEN > cj), 1.0, 0.0).astype(BF16)
        hi = psum.astype(BF16)
        lo = (psum - hi.astype(F32)).astype(BF16)
        imp_t = jnp.dot(ov01, hi, preferred_element_type=F32) + jnp.dot(ov01, lo, preferred_element_type=F32)
        imp_ref[0, 0] = imp_t.T

    n_chunks = max(n16 // LANES, 1)
    chunk = n16 // n_chunks
    last_valid = (q0 + tq - CMP_LEN) // CMP_STRIDE
    need = jnp.clip(last_valid // chunk + 1, 1, n_chunks)
    for c in range(1, n_chunks + 1):
        pl.when(need == c)(functools.partial(attend, c * chunk))


def cmp_attention(q, kc, vc, rep, tq=256):
    B, S, _ = q.shape
    G, n16 = kc.shape[1], kc.shape[2]
    tq = _pick(S, tq)
    w = rep * HEAD_DIM
    kv = pl.BlockSpec((1, 1, n16, HEAD_DIM), lambda b, g, i: (b, g, 0, 0))
    return pl.pallas_call(
        functools.partial(_cmp_body, tq=tq, rep=rep),
        grid=(B, G, S // tq),
        in_specs=[pl.BlockSpec((1, tq, w), lambda b, g, i: (b, i, g)), kv, kv],
        out_specs=[pl.BlockSpec((1, tq, w), lambda b, g, i: (b, i, g)),
                   pl.BlockSpec((1, 1, tq, LANES), lambda b, g, i: (b, g, i, 0))],
        out_shape=[jax.ShapeDtypeStruct((B, S, G * w), BF16),
                   jax.ShapeDtypeStruct((B, G, S, LANES), F32)],
        compiler_params=_cparams("parallel", "parallel", "parallel"),
        name="nsa_cmp",
    )(q, kc, vc)


def _topk_body(imp_ref, sel_ref, *, tq, n_slc):
    q0 = pl.program_id(2) * tq
    imp = imp_ref[0, 0]
    tt = q0 + lax.broadcasted_iota(jnp.int32, (tq, LANES), 0)
    j = lax.broadcasted_iota(jnp.int32, (tq, LANES), 1)
    cur = tt // SLC_LEN
    forced = (j == 0) | (j == cur) | (j == cur - 1)
    allowed = j * SLC_LEN <= tt
    score = jnp.where(forced, FORCE_SCORE, jnp.where(allowed, imp, -1.0))
    score = jnp.where(j < n_slc, score, -jnp.inf)
    jf = j.astype(F32)
    sel = jnp.zeros((tq, LANES), F32)
    for _ in range(min(SLC_TOP, n_slc)):
        m = jnp.max(score, axis=-1, keepdims=True)
        first = jnp.min(jnp.where(score == m, jf, float(LANES)), axis=-1, keepdims=True)
        pick = jf == first
        sel = jnp.where(pick, 1.0, sel)
        score = jnp.where(pick, -jnp.inf, score)
    sel_ref[0, 0] = sel.astype(sel_ref.dtype)


def select_blocks(imp, tq=1024):
    B, G, S, _ = imp.shape
    n_slc = S // SLC_LEN
    assert n_slc <= LANES
    tq = _pick(S, tq)
    spec = pl.BlockSpec((1, 1, tq, LANES), lambda b, g, i: (b, g, i, 0))
    return pl.pallas_call(
        functools.partial(_topk_body, tq=tq, n_slc=n_slc),
        grid=(B, G, S // tq),
        in_specs=[spec],
        out_specs=spec,
        out_shape=jax.ShapeDtypeStruct((B, G, S, LANES), BF16),
        compiler_params=_cparams("parallel", "parallel", "parallel"),
        name="nsa_topk",
    )(imp)


def _gqa_body(*refs, tq, tk, rep, mode):
    q0 = pl.program_id(2) * tq
    if mode == "sel":
        q_ref, k_ref, v_ref, sel_ref, o_ref, s_ref, acc_ref = refs
        key_blk = lax.broadcasted_iota(jnp.int32, (tk, LANES), 0) // SLC_LEN
        lane_blk = lax.broadcasted_iota(jnp.int32, (tk, LANES), 1)
        unpicked = ((sel_ref[0, 0].astype(F32) - 1.0) * (-NEG_INF)).astype(BF16)
        q2 = jnp.concatenate([_stack_heads(q_ref, rep), jnp.concatenate([unpicked] * rep, axis=0)], axis=1)
    else:
        q_ref, k_ref, v_ref, o_ref, s_ref, acc_ref = refs
        q2 = _stack_heads(q_ref, rep)
    kpos = lax.broadcasted_iota(jnp.int32, (tk, tq), 0)
    t = q0 + lax.broadcasted_iota(jnp.int32, (tk, tq), 1)
    cols = [slice(r * tq, (r + 1) * tq) for r in range(rep)]

    def put_scores(slot, kj):
        k0 = pl.multiple_of(kj * tk, tk)
        keys = k_ref[0, pl.ds(k0, tk), :]
        if mode == "sel":
            onehot = jnp.where(key_blk + kj * (tk // SLC_LEN) == lane_blk, 1.0, 0.0).astype(BF16)
            keys = jnp.concatenate([keys, onehot], axis=1)
        s_ref[slot] = lax.dot_general(keys, q2, _NT, preferred_element_type=F32)

    def tile(slot, kj, m, l, diagonal):
        k0 = pl.multiple_of(kj * tk, tk)
        kp = k0 + kpos
        if mode == "sel":
            bias = jnp.where(kp <= t, 0.0, NEG_INF) if diagonal else None
        else:
            ok = kp > t - WINDOW
            if diagonal:
                ok = ok & (kp <= t)
            bias = jnp.where(ok, 0.0, NEG_INF)
        ps, m_new, l_new, scale = [], [], [], []
        for r in range(rep):
            s = s_ref[slot, :, cols[r]]
            if bias is None:
                mr = jnp.maximum(m[r], jnp.max(s, axis=0, keepdims=True))
                p = jnp.exp2(s - mr)
            else:
                mr = jnp.maximum(m[r], jnp.max(s + bias, axis=0, keepdims=True))
                p = jnp.exp2((s - mr) + bias)
            a = jnp.exp2(m[r] - mr)
            ps.append(p.astype(BF16))
            m_new.append(mr)
            l_new.append(a * l[r] + jnp.sum(p, axis=0, keepdims=True))
            scale.append(a)
        pt = jnp.concatenate(ps, axis=1)
        pv = lax.dot_general(v_ref[0, pl.ds(k0, tk), :], pt, _TN, preferred_element_type=F32)
        acc_ref[...] = jnp.concatenate(scale, axis=1) * acc_ref[...] + pv
        return tuple(m_new), tuple(l_new)

    kd = q0 // tk
    lo = 0 if mode == "sel" else jnp.maximum(q0 - (WINDOW - 1), 0) // tk
    put_scores(0, kd)
    put_scores(1, lo)
    acc_ref[...] = jnp.zeros_like(acc_ref)
    stats = tile(0, kd, (jnp.full((1, tq), NEG_INF, F32),) * rep, (jnp.zeros((1, tq), F32),) * rep, True)

    def pair(i, stats):
        kj = lo + 2 * i
        put_scores(0, kj + 1)
        stats = tile(1, kj, *stats, False)
        put_scores(1, kj + 2)
        return tile(0, kj + 1, *stats, False)

    n_off = kd - lo
    stats = lax.fori_loop(0, n_off // 2, pair, stats)
    m, l = lax.cond(n_off % 2 == 1, lambda st: tile(1, kd - 1, *st, False), lambda st: st, stats)
    for r in range(rep):
        out = acc_ref[:, cols[r]] * (1.0 / l[r])
        o_ref[0, :, r * HEAD_DIM:(r + 1) * HEAD_DIM] = out.T.astype(o_ref.dtype)


def gqa_attention(q, k, v, v_col0, rep, mode, sel=None, tq=128, tk=512):
    B, S, _ = q.shape
    G = k.shape[2] // HEAD_DIM
    tq, tk = _pick(S, tq), _pick(S, tk)
    assert tk % tq == 0
    w = rep * HEAD_DIM
    in_specs = [pl.BlockSpec((1, tq, w), lambda b, g, i: (b, i, g)),
                pl.BlockSpec((1, S, HEAD_DIM), lambda b, g, i: (b, 0, g)),
                pl.BlockSpec((1, S, HEAD_DIM), lambda b, g, i: (b, 0, v_col0 + g))]
    args = [q, k, v]
    scratch = [pltpu.VMEM((2, tk, rep * tq), F32), pltpu.VMEM((HEAD_DIM, rep * tq), F32)]
    if mode == "sel":
        assert S // SLC_LEN <= LANES and tk % SLC_LEN == 0
        in_specs += [pl.BlockSpec((1, 1, tq, LANES), lambda b, g, i: (b, g, i, 0))]
        args += [sel]
    return pl.pallas_call(
        functools.partial(_gqa_body, tq=tq, tk=tk, rep=rep, mode=mode),
        grid=(B, G, S // tq),
        in_specs=in_specs,
        out_specs=pl.BlockSpec((1, tq, w), lambda b, g, i: (b, i, g)),
        out_shape=jax.ShapeDtypeStruct((B, S, G * w), BF16),
        scratch_shapes=scratch,
        compiler_params=_cparams("parallel", "parallel", "arbitrary"),
        name="nsa_" + mode,
    )(*args)


def _nsa_gate_body(gl_ref, oc_ref, os_ref, ow_ref, o_ref, *, n_heads):
    ng = 3 * n_heads
    gl = gl_ref[:, :ng].astype(F32)
    gate = 1.0 / (1.0 + jnp.exp(-gl))
    src = lax.broadcasted_iota(jnp.int32, (ng, n_heads * HEAD_DIM), 0)
    head = lax.broadcasted_iota(jnp.int32, (ng, n_heads * HEAD_DIM), 1) // HEAD_DIM
    out = None
    for c, ref in enumerate((oc_ref, os_ref, ow_ref)):
        spread01 = jnp.where(src == head * 3 + c, 1.0, 0.0).astype(BF16)
        term = _split_dot(gate, spread01) * ref[...].astype(F32)
        out = term if out is None else out + term
    o_ref[...] = out.astype(o_ref.dtype)


def nsa_gate(gl, oc, os_, ow, n_heads, tm=256):
    M, W = oc.shape
    tm = _pick(M, tm)
    row = pl.BlockSpec((tm, W), lambda i: (i, 0))
    return pl.pallas_call(
        functools.partial(_nsa_gate_body, n_heads=n_heads),
        grid=(M // tm,),
        in_specs=[pl.BlockSpec((tm, gl.shape[1]), lambda i: (i, 0)), row, row, row],
        out_specs=row,
        out_shape=jax.ShapeDtypeStruct((M, W), BF16),
        compiler_params=_cparams("parallel"),
        name="nsa_gate",
    )(gl, oc, os_, ow)


def _xattn_body(h_ref, hb_ref, kv_ref, wq_ref, wo_ref, g_ref, b_ref, o32_ref, o16_ref, *, n_heads, scale, alpha):
    w = n_heads * HEAD_DIM
    q = jnp.dot(hb_ref[0], wq_ref[...], preferred_element_type=F32).astype(BF16)
    outs = []
    for hd in range(n_heads):
        sl = slice(hd * HEAD_DIM, (hd + 1) * HEAD_DIM)
        k = kv_ref[0, :, sl]
        v = kv_ref[0, :, w + hd * HEAD_DIM:w + (hd + 1) * HEAD_DIM]
        s = lax.dot_general(q[:, sl], k, _NT, preferred_element_type=F32) * scale
        p = jnp.exp(s - jnp.max(s, axis=-1, keepdims=True))
        p = p / jnp.sum(p, axis=-1, keepdims=True)
        outs.append(jnp.dot(p.astype(BF16), v, preferred_element_type=F32).astype(BF16))
    mix = jnp.dot(jnp.concatenate(outs, axis=1), wo_ref[...], preferred_element_type=F32)
    out = _layer_norm(alpha * h_ref[0] + mix, g_ref[...], b_ref[...])
    o32_ref[0] = out
    o16_ref[0] = out.astype(BF16)


def mem_attention_block(h, hb, kv, w_q, w_o, g, b, n_heads, alpha, tq=256):
    B, S, D = h.shape
    n_mem = kv.shape[1]
    w = n_heads * HEAD_DIM
    tq = _pick(S, tq)
    row = pl.BlockSpec((1, tq, D), lambda bb, i: (bb, i, 0))
    vec = pl.BlockSpec((1, D), lambda bb, i: (0, 0))
    return pl.pallas_call(
        functools.partial(_xattn_body, n_heads=n_heads, scale=HEAD_DIM ** -0.5, alpha=alpha),
        grid=(B, S // tq),
        in_specs=[row, row,
                  pl.BlockSpec((1, n_mem, 2 * w), lambda bb, i: (bb, 0, 0)),
                  pl.BlockSpec((D, w), lambda bb, i: (0, 0)),
                  pl.BlockSpec((w, D), lambda bb, i: (0, 0)), vec, vec],
        out_specs=[row, row],
        out_shape=[jax.ShapeDtypeStruct((B, S, D), F32), jax.ShapeDtypeStruct((B, S, D), BF16)],
        compiler_params=_cparams("parallel", "parallel"),
        name="mem_attention_block",
    )(h, hb, kv, w_q, w_o, g.reshape(1, D).astype(F32), b.reshape(1, D).astype(F32))


def _ffn_up_body(x_ref, wa_ref, wu_ref, cw_ref, o_ref, wab_ref, wub_ref, tail_ref, *, tiles_per_seq):
    i = pl.program_id(1)

    @pl.when(i == 0)
    def _():
        wab_ref[...] = wa_ref[0].astype(BF16)
        wub_ref[...] = wu_ref[0].astype(BF16)

    x = x_ref[...]
    a = jnp.dot(x, wab_ref[...], preferred_element_type=F32)
    tm = a.shape[0]
    first = (i % tiles_per_seq) == 0
    prev = jnp.where(first, 0.0, tail_ref[...])
    tail_ref[...] = a[tm - 8:]
    rowi = lax.broadcasted_iota(jnp.int32, a.shape, 0)
    a1 = jnp.where(rowi >= 1, pltpu.roll(a, 1, 0), prev[7:8])
    a2 = jnp.where(rowi >= 2, pltpu.roll(a, 2, 0), jnp.where(rowi == 1, prev[7:8], prev[6:7]))
    cw = cw_ref[...]
    gate = _gelu(cw[2:3] * a + cw[1:2] * a1 + cw[0:1] * a2)
    u = jnp.dot(x, wub_ref[...], preferred_element_type=F32)
    o_ref[...] = (gate * u).astype(o_ref.dtype)


def ffn_up_glu(x, w_up, layer, conv_w, seq_len, tm=1024, tn=256):
    M, K = x.shape
    Fd = w_up.shape[2] // 2
    tm, tn = _pick(seq_len, tm), _pick(Fd, tn)
    nj = Fd // tn
    return pl.pallas_call(
        functools.partial(_ffn_up_body, tiles_per_seq=seq_len // tm),
        grid=(nj, M // tm),
        in_specs=[pl.BlockSpec((tm, K), lambda j, i: (i, 0)),
                  pl.BlockSpec((1, K, tn), lambda j, i: (layer, 0, j)),
                  pl.BlockSpec((1, K, tn), lambda j, i: (layer, 0, j + nj)),
                  pl.BlockSpec((CONV_W, tn), lambda j, i: (0, j))],
        out_specs=pl.BlockSpec((tm, tn), lambda j, i: (i, j)),
        out_shape=jax.ShapeDtypeStruct((M, Fd), BF16),
        scratch_shapes=[pltpu.VMEM((K, tn), BF16), pltpu.VMEM((K, tn), BF16), pltpu.VMEM((8, tn), F32)],
        compiler_params=_cparams("parallel", "arbitrary"),
        name="ffn_up_glu",
    )(x, w_up, w_up, conv_w.astype(F32))


def _rope_tables(S):
    half = HEAD_DIM // 2
    inv_freq = ROPE_THETA ** (-jnp.arange(half, dtype=F32) / half)
    ang = jnp.arange(S, dtype=F32)[:, None] * inv_freq[None, :]
    cos, sin = jnp.cos(ang), jnp.sin(ang)
    return jnp.concatenate([cos, cos], axis=-1), jnp.concatenate([-sin, sin], axis=-1)


def _hgrn_sb_mixer(hb, B, S, w_in_all, lb_raw, norm_w, w_out, e):
    width = w_out.shape[0]
    a_heads = width // (2 * HEAD_DIM)
    b_heads = a_heads
    proj = matmul_f32w(hb, w_in_all, e).reshape(B, S, -1)
    o_a = hgrn2(proj, lb_raw, norm_w, a_heads, e)
    o_b = stick_breaking(proj, 4 * a_heads, 4 * a_heads + b_heads, 4 * a_heads + 2 * b_heads, b_heads)
    o = jnp.concatenate([o_a, o_b], axis=-1).reshape(B * S, width)
    return matmul(o, w_out.astype(BF16))


def _nsa_mixer(hb, B, S, w_in_all, o, cmp_pos, cmp_w1, cmp_w2, w_out, cos, sin):
    G = NSA_KV_HEADS
    q_w = w_out.shape[0]
    n_heads = q_w // HEAD_DIM
    rep = n_heads // G
    kv_w = G * HEAD_DIM
    main_w = q_w + 6 * kv_w
    proj = matmul_f32w(hb, w_in_all, o, n_cols=main_w).reshape(B, S, main_w)
    gl = matmul_f32w(hb, w_in_all, o, col0=main_w, n_cols=LANES, tn=LANES)

    log2_scale = HEAD_DIM ** -0.5 * LOG2E
    q_rot = rope(proj, 0, n_heads, cos, sin, mult=log2_scale)
    ks_rot = rope(proj, q_w + 2 * kv_w, G, cos, sin)
    kw_rot = rope(proj, q_w + 4 * kv_w, G, cos, sin)

    n16 = S // CMP_STRIDE
    kvc_in = proj[:, :, q_w:q_w + 2 * kv_w].reshape(B, S, 2, G, HEAD_DIM)
    x16 = kvc_in.transpose(2, 0, 3, 1, 4).reshape(2, B, G, n16, CMP_STRIDE * HEAD_DIM)
    pe = jnp.broadcast_to(cmp_pos.reshape(2, 1, CMP_LEN * HEAD_DIM), (2, 8, CMP_LEN * HEAD_DIM)).astype(BF16)
    w1 = cmp_w1.reshape(2, CMP_LEN * HEAD_DIM, HEAD_DIM).astype(BF16)
    kvc = compress(x16, pe, w1, cmp_w2.astype(BF16), log2_scale)

    o_c, imp = cmp_attention(proj, kvc[0], kvc[1], rep)
    sel = select_blocks(imp)
    col = lambda off: (q_w + off * kv_w) // HEAD_DIM
    o_s = gqa_attention(q_rot, ks_rot, proj, col(3), rep, "sel", sel=sel, tq=256, tk=512)
    o_w = gqa_attention(q_rot, kw_rot, proj, col(5), rep, "win", tq=256, tk=256)
    o = nsa_gate(gl, o_c.reshape(B * S, q_w), o_s.reshape(B * S, q_w), o_w.reshape(B * S, q_w), n_heads)
    return matmul(o, w_out.astype(BF16))


def kernel(x, mem, ab_w_in, hgrn_lb, hgrn_norm_w, ab_w_out, nsa_w_in, nsa_cmp_pos, nsa_cmp_w1,
           nsa_cmp_w2, nsa_w_out, xa_w_q, xa_w_kv, xa_w_o, ffn_w_up, ffn_conv, ffn_w_down, ln_g, ln_b):
    B, S, D = x.shape
    depth = ln_g.shape[0]
    alpha = (2 * depth) ** 0.25
    n_mem = mem.shape[1]
    cos, sin = _rope_tables(S)
    h = x.reshape(B * S, D).astype(F32)
    hb = h.astype(BF16)
    memb = mem.reshape(B * n_mem, D).astype(BF16)
    for layer in range(depth):
        if layer % 2 == 0:
            e = layer // 2
            mix = _hgrn_sb_mixer(hb, B, S, ab_w_in, hgrn_lb, hgrn_norm_w[e], ab_w_out[e], e)
        else:
            o = layer // 2
            mix = _nsa_mixer(hb, B, S, nsa_w_in, o, nsa_cmp_pos[o], nsa_cmp_w1[o], nsa_cmp_w2[o],
                             nsa_w_out[o], cos, sin)
        h, hb = add_layer_norm(h, mix, ln_g[layer, 0], ln_b[layer, 0], alpha)

        xkv = matmul(memb, xa_w_kv[layer].astype(BF16)).reshape(B, n_mem, -1)
        h3, hb3 = mem_attention_block(h.reshape(B, S, D), hb.reshape(B, S, D), xkv, xa_w_q[layer].astype(BF16),
                                      xa_w_o[layer].astype(BF16), ln_g[layer, 1], ln_b[layer, 1], XA_HEADS, alpha)
        h, hb = h3.reshape(B * S, D), hb3.reshape(B * S, D)

        gated = ffn_up_glu(hb, ffn_w_up, layer, ffn_conv[layer], S)
        h, hb = add_layer_norm(h, matmul(gated, ffn_w_down[layer].astype(BF16), tm=512, tk=gated.shape[1]),
                               ln_g[layer, 2], ln_b[layer, 2], alpha)
    return h.reshape(B, S, D).astype(x.dtype)
```

```python
import functools

import jax
import jax.numpy as jnp
from jax import lax
from jax.experimental import pallas as pl
from jax.experimental.pallas import tpu as pltpu

F32 = jnp.float32
BF16 = jnp.bfloat16

HEAD_DIM = 128
LANES = 128
HGRN_SUB = 16
HGRN_SAFE_LOG_DECAY = -60.0
NSA_KV_HEADS = 4
CMP_LEN = 32
CMP_STRIDE = 16
SLC_LEN = 64
SLC_TOP = 16
WINDOW = 512
XA_HEADS = 4
CONV_W = 3
ROPE_THETA = 10000.0
LN_EPS = 1e-5
RMS_EPS = 1e-6
NEG_INF = -1e30
FORCE_SCORE = 1e9
EXP_ZERO_BELOW = -104.0
SOFTPLUS_CLAMP = 80.0
LOG2E = 1.4426950408889634
VMEM_LIMIT = 52 * 1024 * 1024

_NT = (((1,), (1,)), ((), ()))
_TN = (((0,), (0,)), ((), ()))


def _cparams(*sem):
    return pltpu.CompilerParams(dimension_semantics=sem, vmem_limit_bytes=VMEM_LIMIT)


def _split_dot(a, b01):
    hi = a.astype(BF16)
    lo = (a - hi.astype(F32)).astype(BF16)
    return (jnp.dot(hi, b01, preferred_element_type=F32)
            + jnp.dot(lo, b01, preferred_element_type=F32))


def _mm_body(x_ref, w_ref, o_ref, *scratch, nk):
    prod = jnp.dot(x_ref[...], w_ref[...], preferred_element_type=F32)
    if nk == 1:
        o_ref[...] = prod.astype(o_ref.dtype)
        return
    acc_ref, = scratch
    k = pl.program_id(2)

    @pl.when(k == 0)
    def _():
        acc_ref[...] = prod

    @pl.when(k > 0)
    def _():
        acc_ref[...] += prod

    @pl.when(k == nk - 1)
    def _():
        o_ref[...] = acc_ref[...].astype(o_ref.dtype)


def _pick(n, pref):
    if n <= pref:
        return n
    t = pref
    while t >= LANES:
        if n % t == 0:
            return t
        t -= LANES
    return n


def matmul(x, w, out_dtype=BF16, tm=1024, tn=512, tk=4096, col0=0, n_cols=None):
    M, K = x.shape
    N = w.shape[1] if n_cols is None else n_cols
    tm, tn, tk = _pick(M, tm), _pick(N, tn), _pick(K, tk)
    assert col0 % tn == 0
    cb = col0 // tn
    nk = K // tk
    scratch = [] if nk == 1 else [pltpu.VMEM((tm, tn), F32)]
    return pl.pallas_call(
        functools.partial(_mm_body, nk=nk),
        grid=(M // tm, N // tn, nk),
        in_specs=[pl.BlockSpec((tm, tk), lambda i, j, k: (i, k)),
                  pl.BlockSpec((tk, tn), lambda i, j, k: (k, j + cb))],
        out_specs=pl.BlockSpec((tm, tn), lambda i, j, k: (i, j)),
        out_shape=jax.ShapeDtypeStruct((M, N), out_dtype),
        scratch_shapes=scratch,
        compiler_params=_cparams("parallel", "parallel", "arbitrary"),
        name="matmul",
    )(x, w)


def _mm_f32w_body(x_ref, w_ref, o_ref, wb_ref):
    @pl.when(pl.program_id(1) == 0)
    def _():
        wb_ref[...] = w_ref[0].astype(BF16)

    o_ref[...] = jnp.dot(x_ref[...], wb_ref[...], preferred_element_type=F32).astype(o_ref.dtype)


def matmul_f32w(x, w, layer, out_dtype=BF16, tm=1024, tn=512, col0=0, n_cols=None):
    M, K = x.shape
    N = w.shape[2] if n_cols is None else n_cols
    tm, tn = _pick(M, tm), _pick(N, tn)
    assert col0 % tn == 0
    cb = col0 // tn
    return pl.pallas_call(
        _mm_f32w_body,
        grid=(N // tn, M // tm),
        in_specs=[pl.BlockSpec((tm, K), lambda j, i: (i, 0)),
                  pl.BlockSpec((1, K, tn), lambda j, i: (layer, 0, j + cb))],
        out_specs=pl.BlockSpec((tm, tn), lambda j, i: (i, j)),
        out_shape=jax.ShapeDtypeStruct((M, N), out_dtype),
        scratch_shapes=[pltpu.VMEM((K, tn), BF16)],
        compiler_params=_cparams("parallel", "arbitrary"),
        name="matmul_f32w",
    )(x, w)


def _layer_norm(y, g, b):
    mu = jnp.mean(y, axis=-1, keepdims=True)
    d = y - mu
    var = jnp.mean(d * d, axis=-1, keepdims=True)
    return d * lax.rsqrt(var + LN_EPS) * g + b


def _add_ln_body(h_ref, m_ref, g_ref, b_ref, o32_ref, o16_ref, *, alpha):
    out = _layer_norm(alpha * h_ref[...] + m_ref[...].astype(F32), g_ref[...], b_ref[...])
    o32_ref[...] = out
    o16_ref[...] = out.astype(BF16)


def add_layer_norm(h, mix, g, b, alpha, tm=256):
    M, D = h.shape
    tm = _pick(M, tm)
    row = pl.BlockSpec((tm, D), lambda i: (i, 0))
    vec = pl.BlockSpec((1, D), lambda i: (0, 0))
    return pl.pallas_call(
        functools.partial(_add_ln_body, alpha=alpha),
        grid=(M // tm,),
        in_specs=[row, row, vec, vec],
        out_specs=[row, row],
        out_shape=[jax.ShapeDtypeStruct((M, D), F32), jax.ShapeDtypeStruct((M, D), BF16)],
        compiler_params=_cparams("parallel"),
        name="add_layer_norm",
    )(h, mix, g.reshape(1, D).astype(F32), b.reshape(1, D).astype(F32))


def _hgrn_body(q_ref, f_ref, i_ref, g_ref, lb_ref, nw_ref, o_ref, st_ref, *, ts, layer_idx, nh):
    C = HGRN_SUB
    nsub = ts // C
    heads = [slice(h * HEAD_DIM, (h + 1) * HEAD_DIM) for h in range(nh)]

    @pl.when(pl.program_id(2) == 0)
    def _():
        st_ref[...] = jnp.zeros_like(st_ref)

    lbr = lb_ref[...]
    ex = jnp.exp(lbr - jnp.max(lbr, axis=0, keepdims=True))
    sm = ex / jnp.sum(ex, axis=0, keepdims=True)
    lb = jnp.sum(sm[:layer_idx + 1], axis=0, keepdims=True)

    q = q_ref[0].astype(F32)
    z = f_ref[0].astype(F32)
    v = i_ref[0]
    e = jnp.exp(-jnp.abs(z))
    r = 1.0 / (1.0 + e)
    pos = z >= 0
    sig = jnp.where(pos, r, e * r)
    nsig = jnp.where(pos, e * r, r)
    logf = jnp.log(lb + (1.0 - lb) * sig)
    k = (1.0 - lb) * nsig

    hi = logf.astype(BF16)
    lo = (logf - hi.astype(F32)).astype(BF16)
    row = lax.broadcasted_iota(jnp.int32, (ts, ts), 0)
    col = lax.broadcasted_iota(jnp.int32, (ts, ts), 1)

    def cumdot(m01):
        return jnp.dot(m01, hi, preferred_element_type=F32) + jnp.dot(m01, lo, preferred_element_type=F32)

    def whole_tile(sts):
        b = cumdot(jnp.where(row >= col, 1.0, 0.0).astype(BF16))
        bl = b[ts - 1:ts]
        qd = (q * jnp.exp(b)).astype(BF16)
        kinv = (k * jnp.exp(-b)).astype(BF16)
        kd = (k * jnp.exp(bl - b)).astype(BF16)
        dec = jnp.exp(bl)
        outs, new = [], []
        for h, sl in enumerate(heads):
            dmat = lax.dot_general(qd[:, sl], kinv[:, sl], _NT, preferred_element_type=F32)
            dmat = jnp.where(row >= col, dmat, 0.0)
            o = jnp.dot(dmat.astype(BF16), v[:, sl], preferred_element_type=F32)
            outs.append(o + lax.dot_general(qd[:, sl], sts[h].astype(BF16), _NT, preferred_element_type=F32))
            new.append(sts[h] * dec[:, sl] + lax.dot_general(v[:, sl], kd[:, sl], _TN, preferred_element_type=F32))
        return tuple(outs), tuple(new)

    def sub_chunks(sts):
        same = (row // C) == (col // C)
        b = cumdot(jnp.where(same & (row >= col), 1.0, 0.0).astype(BF16))
        bl = cumdot(jnp.where(same, 1.0, 0.0).astype(BF16))
        qd = (q * jnp.exp(b)).astype(BF16)
        kd = (k * jnp.exp(bl - b)).astype(BF16)
        dec = jnp.exp(bl)
        tri = (lax.broadcasted_iota(jnp.int32, (C, C, HEAD_DIM), 0)
               >= lax.broadcasted_iota(jnp.int32, (C, C, HEAD_DIM), 1))
        outs, new = [], []
        for h, hs in enumerate(heads):
            st = sts[h]
            parts = []
            for n in range(nsub):
                sl = slice(n * C, (n + 1) * C)
                bn, qn, kn = b[sl, hs], q[sl, hs], k[sl, hs]
                diff = bn[:, None, :] - bn[None, :, :]
                ee = jnp.exp(jnp.where(tri, diff, NEG_INF))
                dmat = jnp.sum(qn[:, None, :] * (kn[None, :, :] * ee), axis=-1)
                o_n = jnp.dot(dmat.astype(BF16), v[sl, hs], preferred_element_type=F32)
                o_n = o_n + lax.dot_general(qd[sl, hs], st.astype(BF16), _NT, preferred_element_type=F32)
                upd = lax.dot_general(v[sl, hs], kd[sl, hs], _TN, preferred_element_type=F32)
                st = st * dec[n * C:n * C + 1, hs] + upd
                parts.append(o_n)
            outs.append(jnp.concatenate(parts, axis=0))
            new.append(st)
        return tuple(outs), tuple(new)

    tile_decay = jnp.min(jnp.sum(logf, axis=0, keepdims=True))
    outs, sts = lax.cond(tile_decay > HGRN_SAFE_LOG_DECAY, whole_tile, sub_chunks,
                         tuple(st_ref[h] for h in range(nh)))
    gt = g_ref[0].astype(F32)
    gate = gt / (1.0 + jnp.exp(-gt))
    for h, sl in enumerate(heads):
        st_ref[h] = sts[h]
        o = outs[h]
        o = o * lax.rsqrt(jnp.mean(o * o, axis=-1, keepdims=True) + RMS_EPS) * nw_ref[...] * gate[:, sl]
        o_ref[0, :, sl] = o.astype(o_ref.dtype)


def hgrn2(proj, lb_raw, norm_w, n_heads, layer_idx, ts=128, nh=8):
    B, S, _ = proj.shape
    ts = _pick(S, ts)
    nh = min(nh, n_heads)
    assert n_heads % nh == 0
    H = n_heads // nh
    L = lb_raw.shape[0]
    w = nh * HEAD_DIM

    def col(off):
        return pl.BlockSpec((1, ts, w), lambda b, h, s: (b, s, off * H + h))

    return pl.pallas_call(
        functools.partial(_hgrn_body, ts=ts, layer_idx=layer_idx, nh=nh),
        grid=(B, H, S // ts),
        in_specs=[col(0), col(1), col(2), col(3),
                  pl.BlockSpec((L, w), lambda b, h, s: (0, h)),
                  pl.BlockSpec((1, HEAD_DIM), lambda b, h, s: (0, 0))],
        out_specs=pl.BlockSpec((1, ts, w), lambda b, h, s: (b, s, h)),
        out_shape=jax.ShapeDtypeStruct((B, S, n_heads * HEAD_DIM), BF16),
        scratch_shapes=[pltpu.VMEM((nh, HEAD_DIM, HEAD_DIM), F32)],
        compiler_params=_cparams("parallel", "parallel", "arbitrary"),
        name="hgrn2",
    )(proj, proj, proj, proj, lb_raw.astype(F32), norm_w.reshape(1, HEAD_DIM).astype(F32))


def _sb_block(q, k, v, carry, after01, scale, mask):
    z = lax.dot_general(q, k, _NT, preferred_element_type=F32) * scale
    sp = jnp.maximum(z, jnp.log(1.0 + jnp.exp(jnp.minimum(z, SOFTPLUS_CLAMP))))
    spm = sp if mask is None else jnp.where(mask, sp, 0.0)
    rev = _split_dot(spm, after01)
    w = jnp.exp(z - sp - rev - carry)
    if mask is not None:
        w = jnp.where(mask, w, 0.0)
    contrib = jnp.dot(w.astype(BF16), v, preferred_element_type=F32)
    return contrib, carry + rev[:, 0:1] + spm[:, 0:1]


def _sb_body(q_ref, k_ref, v_ref, o_ref, *, tq, first, scale, nh):
    qi = pl.program_id(2)
    heads = [slice(h * HEAD_DIM, (h + 1) * HEAD_DIM) for h in range(nh)]
    qs = [q_ref[0, :, sl] for sl in heads]

    def after01(width):
        return jnp.where(lax.broadcasted_iota(jnp.int32, (width, width), 0)
                         > lax.broadcasted_iota(jnp.int32, (width, width), 1), 1.0, 0.0).astype(BF16)

    def blocks(k0, width, after, carries, mask):
        res = [_sb_block(qs[h], k_ref[0, pl.ds(k0, width), heads[h]], v_ref[0, pl.ds(k0, width), heads[h]],
                         carries[h], after, scale, mask) for h in range(nh)]
        return tuple(r[0] for r in res), tuple(r[1] for r in res)

    k0 = pl.multiple_of(jnp.maximum((qi + 1) * tq - first, 0), tq)
    t = qi * tq + lax.broadcasted_iota(jnp.int32, (tq, first), 0)
    kpos = k0 + lax.broadcasted_iota(jnp.int32, (tq, first), 1)
    accs, carries = blocks(k0, first, after01(first), (jnp.zeros((tq, 1), F32),) * nh, kpos < t)

    def cond(c):
        j, _, carries = c
        low = functools.reduce(jnp.minimum, [jnp.min(cr) for cr in carries])
        return jnp.logical_and(j >= 0, low < -EXP_ZERO_BELOW)

    after_tq = after01(tq)

    def body(c):
        j, accs, carries = c
        contribs, carries = blocks(pl.multiple_of(j * tq, tq), tq, after_tq, carries, None)
        return j - 1, tuple(a + cb for a, cb in zip(accs, contribs)), carries

    _, accs, _ = lax.while_loop(cond, body, (k0 // tq - 1, accs, carries))
    for h in range(nh):
        o_ref[0, :, heads[h]] = accs[h].astype(o_ref.dtype)


def stick_breaking(proj, col_q, col_k, col_v, n_heads, tq=256, nh=4):
    B, S, _ = proj.shape
    tq = _pick(S, tq)
    first = min(2 * tq, S)
    assert n_heads % nh == 0 and col_q % nh == 0 and col_k % nh == 0 and col_v % nh == 0
    w = nh * HEAD_DIM
    full = lambda off: pl.BlockSpec((1, S, w), lambda b, h, i: (b, 0, off // nh + h))
    return pl.pallas_call(
        functools.partial(_sb_body, tq=tq, first=first, scale=HEAD_DIM ** -0.5, nh=nh),
        grid=(B, n_heads // nh, S // tq),
        in_specs=[pl.BlockSpec((1, tq, w), lambda b, h, i: (b, i, col_q // nh + h)),
                  full(col_k), full(col_v)],
        out_specs=pl.BlockSpec((1, tq, w), lambda b, h, i: (b, i, h)),
        out_shape=jax.ShapeDtypeStruct((B, S, n_heads * HEAD_DIM), BF16),
        compiler_params=_cparams("parallel", "parallel", "arbitrary"),
        name="stick_breaking",
    )(proj, proj, proj)


def _rope_body(x_ref, cos_ref, sin_ref, o_ref, *, n_heads):
    cos = cos_ref[...]
    sin = sin_ref[...]
    for h in range(n_heads):
        sl = slice(h * HEAD_DIM, (h + 1) * HEAD_DIM)
        t = x_ref[0, :, sl].astype(F32)
        o_ref[0, :, sl] = (t * cos + pltpu.roll(t, HEAD_DIM // 2, 1) * sin).astype(o_ref.dtype)


def rope(x, col0, n_heads, cos, sin, ts=256):
    B, S, _ = x.shape
    ts = _pick(S, ts)
    w = n_heads * HEAD_DIM
    assert col0 % w == 0
    cb = col0 // w
    tab = pl.BlockSpec((ts, HEAD_DIM), lambda b, s: (s, 0))
    return pl.pallas_call(
        functools.partial(_rope_body, n_heads=n_heads),
        grid=(B, S // ts),
        in_specs=[pl.BlockSpec((1, ts, w), lambda b, s: (b, s, cb)), tab, tab],
        out_specs=pl.BlockSpec((1, ts, w), lambda b, s: (b, s, 0)),
        out_shape=jax.ShapeDtypeStruct((B, S, w), BF16),
        compiler_params=_cparams("parallel", "parallel"),
        name="rope",
    )(x, cos, sin)


def _gelu(x):
    return 0.5 * x * (1.0 + lax.erf(x * (2.0 ** -0.5)))


def _compress_body(x_ref, pe_ref, w1_ref, w2_ref, o_ref, *, batch, k_mult):
    x = x_ref[0, 0]
    half = x.shape[1]
    n16 = x.shape[0]
    y1 = jnp.dot(x, w1_ref[0, :half], preferred_element_type=F32)
    y2 = jnp.dot(x, w1_ref[0, half:], preferred_element_type=F32)
    bias = jnp.dot(pe_ref[0], w1_ref[0], preferred_element_type=F32)[0:1]
    hid = _gelu(y1 + pltpu.roll(y2, n16 - 1, 0) + bias)
    out = jnp.dot(hid.astype(BF16), w2_ref[0], preferred_element_type=F32)
    mult = jnp.where(pl.program_id(0) < batch, k_mult, 1.0)
    o_ref[0, 0] = (out * mult).astype(o_ref.dtype)


def compress(x16, pe, w1, w2, k_mult):
    two, B, G, n16, wide = x16.shape
    x16 = x16.reshape(two * B, G, n16, wide)
    out = pl.pallas_call(
        functools.partial(_compress_body, batch=B, k_mult=k_mult),
        grid=(two * B, G),
        in_specs=[pl.BlockSpec((1, 1, n16, wide), lambda i, g: (i, g, 0, 0)),
                  pl.BlockSpec((1, 8, 2 * wide), lambda i, g: (i // B, 0, 0)),
                  pl.BlockSpec((1, 2 * wide, HEAD_DIM), lambda i, g: (i // B, 0, 0)),
                  pl.BlockSpec((1, HEAD_DIM, HEAD_DIM), lambda i, g: (i // B, 0, 0))],
        out_specs=pl.BlockSpec((1, 1, n16, HEAD_DIM), lambda i, g: (i, g, 0, 0)),
        out_shape=jax.ShapeDtypeStruct((two * B, G, n16, HEAD_DIM), BF16),
        compiler_params=_cparams("parallel", "parallel"),
        name="nsa_compress",
    )(x16, pe, w1, w2)
    return out.reshape(two, B, G, n16, HEAD_DIM)


def _stack_heads(q_ref, rep):
    return jnp.concatenate([q_ref[0, :, r * HEAD_DIM:(r + 1) * HEAD_DIM] for r in range(rep)], axis=0)


def _cmp_body(q_ref, kc_ref, vc_ref, o_ref, imp_ref, *, tq, rep):
    q0 = pl.program_id(2) * tq
    n16 = kc_ref.shape[2]
    cols = [slice(r * tq, (r + 1) * tq) for r in range(rep)]
    q2 = _stack_heads(q_ref, rep)

    def attend(nb):
        st = lax.dot_general(kc_ref[0, 0, :nb], q2, _NT, preferred_element_type=F32)
        n = lax.broadcasted_iota(jnp.int32, (nb, tq), 0)
        t = q0 + lax.broadcasted_iota(jnp.int32, (nb, tq), 1)
        bias = jnp.where(n * CMP_STRIDE + (CMP_LEN - 1) <= t, 0.0, NEG_INF)
        ps = []
        psum = jnp.zeros((nb, tq), F32)
        for r in range(rep):
            s = st[:, cols[r]] + bias
            m = jnp.maximum(jnp.max(s, axis=0, keepdims=True), 0.1 * NEG_INF)
            p = jnp.exp2(s - m)
            den = jnp.sum(p, axis=0, keepdims=True)
            pn = p * (1.0 / jnp.where(den > 0.0, den, 1.0))
            ps.append(pn.astype(BF16))
            psum = psum + pn
        ot = lax.dot_general(vc_ref[0, 0, :nb], jnp.concatenate(ps, axis=1), _TN,
                             preferred_element_type=F32)
        for r in range(rep):
            o_ref[0, :, r * HEAD_DIM:(r + 1) * HEAD_DIM] = ot[:, cols[r]].T.astype(o_ref.dtype)

        cj = lax.broadcasted_iota(jnp.int32, (LANES, nb), 0) * SLC_LEN
        cn = lax.broadcasted_iota(jnp.int32, (LANES, nb), 1) * CMP_STRIDE
        ov01 = jnp.where((cn < cj + SLC_LEN) & (cn + CMP_LEN > cj), 1.0, 0.0).astype(BF16)
        hi = psum.astype(BF16)
        lo = (psum - hi.astype(F32)).astype(BF16)
        imp_t = jnp.dot(ov01, hi, preferred_element_type=F32) + jnp.dot(ov01, lo, preferred_element_type=F32)
        imp_ref[0, 0] = imp_t.T

    n_chunks = max(n16 // LANES, 1)
    chunk = n16 // n_chunks
    last_valid = (q0 + tq - CMP_LEN) // CMP_STRIDE
    need = jnp.clip(last_valid // chunk + 1, 1, n_chunks)
    for c in range(1, n_chunks + 1):
        pl.when(need == c)(functools.partial(attend, c * chunk))


def cmp_attention(q, kc, vc, rep, tq=256):
    B, S, _ = q.shape
    G, n16 = kc.shape[1], kc.shape[2]
    tq = _pick(S, tq)
    w = rep * HEAD_DIM
    kv = pl.BlockSpec((1, 1, n16, HEAD_DIM), lambda b, g, i: (b, g, 0, 0))
    return pl.pallas_call(
        functools.partial(_cmp_body, tq=tq, rep=rep),
        grid=(B, G, S // tq),
        in_specs=[pl.BlockSpec((1, tq, w), lambda b, g, i: (b, i, g)), kv, kv],
        out_specs=[pl.BlockSpec((1, tq, w), lambda b, g, i: (b, i, g)),
                   pl.BlockSpec((1, 1, tq, LANES), lambda b, g, i: (b, g, i, 0))],
        out_shape=[jax.ShapeDtypeStruct((B, S, G * w), BF16),
                   jax.ShapeDtypeStruct((B, G, S, LANES), F32)],
        compiler_params=_cparams("parallel", "parallel", "parallel"),
        name="nsa_cmp",
    )(q, kc, vc)


def _topk_body(imp_ref, sel_ref, *, tq, n_slc):
    q0 = pl.program_id(2) * tq
    imp = imp_ref[0, 0]
    tt = q0 + lax.broadcasted_iota(jnp.int32, (tq, LANES), 0)
    j = lax.broadcasted_iota(jnp.int32, (tq, LANES), 1)
    cur = tt // SLC_LEN
    forced = (j == 0) | (j == cur) | (j == cur - 1)
    allowed = j * SLC_LEN <= tt
    score = jnp.where(forced, FORCE_SCORE, jnp.where(allowed, imp, -1.0))
    score = jnp.where(j < n_slc, score, -jnp.inf)
    jf = j.astype(F32)
    sel = jnp.zeros((tq, LANES), F32)
    for _ in range(min(SLC_TOP, n_slc)):
        m = jnp.max(score, axis=-1, keepdims=True)
        first = jnp.min(jnp.where(score == m, jf, float(LANES)), axis=-1, keepdims=True)
        pick = jf == first
        sel = jnp.where(pick, 1.0, sel)
        score = jnp.where(pick, -jnp.inf, score)
    sel_ref[0, 0] = sel.astype(sel_ref.dtype)


def select_blocks(imp, tq=1024):
    B, G, S, _ = imp.shape
    n_slc = S // SLC_LEN
    assert n_slc <= LANES
    tq = _pick(S, tq)
    spec = pl.BlockSpec((1, 1, tq, LANES), lambda b, g, i: (b, g, i, 0))
    return pl.pallas_call(
        functools.partial(_topk_body, tq=tq, n_slc=n_slc),
        grid=(B, G, S // tq),
        in_specs=[spec],
        out_specs=spec,
        out_shape=jax.ShapeDtypeStruct((B, G, S, LANES), BF16),
        compiler_params=_cparams("parallel", "parallel", "parallel"),
        name="nsa_topk",
    )(imp)


def _gqa_body(*refs, tq, tk, rep, mode, mult):
    q0 = pl.program_id(2) * tq
    q_ref, cos_ref, sin_ref, k_ref, v_ref = refs[:5]
    cos = cos_ref[...] * mult
    sin = sin_ref[...] * mult
    rot = []
    for r in range(rep):
        x = q_ref[0, :, r * HEAD_DIM:(r + 1) * HEAD_DIM].astype(F32)
        rot.append((x * cos + pltpu.roll(x, HEAD_DIM // 2, 1) * sin).astype(BF16))
    q2 = jnp.concatenate(rot, axis=0)
    if mode == "sel":
        sel_ref, o_ref, s_ref, acc_ref = refs[5:]
        key_blk = lax.broadcasted_iota(jnp.int32, (tk, LANES), 0) // SLC_LEN
        lane_blk = lax.broadcasted_iota(jnp.int32, (tk, LANES), 1)
        unpicked = ((sel_ref[0, 0].astype(F32) - 1.0) * (-NEG_INF)).astype(BF16)
        q2 = jnp.concatenate([q2, jnp.concatenate([unpicked] * rep, axis=0)], axis=1)
    else:
        o_ref, s_ref, acc_ref = refs[5:]
    kpos = lax.broadcasted_iota(jnp.int32, (tk, tq), 0)
    t = q0 + lax.broadcasted_iota(jnp.int32, (tk, tq), 1)
    cols = [slice(r * tq, (r + 1) * tq) for r in range(rep)]

    def put_scores(slot, kj):
        k0 = pl.multiple_of(kj * tk, tk)
        keys = k_ref[0, pl.ds(k0, tk), :]
        if mode == "sel":
            onehot = jnp.where(key_blk + kj * (tk // SLC_LEN) == lane_blk, 1.0, 0.0).astype(BF16)
            keys = jnp.concatenate([keys, onehot], axis=1)
        s_ref[slot] = lax.dot_general(keys, q2, _NT, preferred_element_type=F32)

    def tile(slot, kj, m, l, diagonal):
        k0 = pl.multiple_of(kj * tk, tk)
        kp = k0 + kpos
        if mode == "sel":
            bias = jnp.where(kp <= t, 0.0, NEG_INF) if diagonal else None
        else:
            ok = kp > t - WINDOW
            if diagonal:
                ok = ok & (kp <= t)
            bias = jnp.where(ok, 0.0, NEG_INF)
        ps, m_new, l_new, scale = [], [], [], []
        for r in range(rep):
            s = s_ref[slot, :, cols[r]]
            if bias is None:
                mr = jnp.maximum(m[r], jnp.max(s, axis=0, keepdims=True))
                p = jnp.exp2(s - mr)
            else:
                mr = jnp.maximum(m[r], jnp.max(s + bias, axis=0, keepdims=True))
                p = jnp.exp2((s - mr) + bias)
            a = jnp.exp2(m[r] - mr)
            ps.append(p.astype(BF16))
            m_new.append(mr)
            l_new.append(a * l[r] + jnp.sum(p, axis=0, keepdims=True))
            scale.append(a)
        pt = jnp.concatenate(ps, axis=1)
        pv = lax.dot_general(v_ref[0, pl.ds(k0, tk), :], pt, _TN, preferred_element_type=F32)
        acc_ref[...] = jnp.concatenate(scale, axis=1) * acc_ref[...] + pv
        return tuple(m_new), tuple(l_new)

    kd = q0 // tk
    lo = 0 if mode == "sel" else jnp.maximum(q0 - (WINDOW - 1), 0) // tk
    put_scores(0, kd)
    put_scores(1, lo)
    acc_ref[...] = jnp.zeros_like(acc_ref)
    stats = tile(0, kd, (jnp.full((1, tq), NEG_INF, F32),) * rep, (jnp.zeros((1, tq), F32),) * rep, True)

    def pair(i, stats):
        kj = lo + 2 * i
        put_scores(0, kj + 1)
        stats = tile(1, kj, *stats, False)
        put_scores(1, kj + 2)
        return tile(0, kj + 1, *stats, False)

    n_off = kd - lo
    stats = lax.fori_loop(0, n_off // 2, pair, stats)
    m, l = lax.cond(n_off % 2 == 1, lambda st: tile(1, kd - 1, *st, False), lambda st: st, stats)
    for r in range(rep):
        out = acc_ref[:, cols[r]] * (1.0 / l[r])
        o_ref[0, :, r * HEAD_DIM:(r + 1) * HEAD_DIM] = out.T.astype(o_ref.dtype)


def gqa_attention(q, cos, sin, mult, k, v, v_col0, rep, mode, sel=None, tq=128, tk=512):
    B, S, _ = q.shape
    G = k.shape[2] // HEAD_DIM
    tq, tk = _pick(S, tq), _pick(S, tk)
    assert tk % tq == 0
    w = rep * HEAD_DIM
    tab = pl.BlockSpec((tq, HEAD_DIM), lambda b, g, i: (i, 0))
    in_specs = [pl.BlockSpec((1, tq, w), lambda b, g, i: (b, i, g)), tab, tab,
                pl.BlockSpec((1, S, HEAD_DIM), lambda b, g, i: (b, 0, g)),
                pl.BlockSpec((1, S, HEAD_DIM), lambda b, g, i: (b, 0, v_col0 + g))]
    args = [q, cos, sin, k, v]
    scratch = [pltpu.VMEM((2, tk, rep * tq), F32), pltpu.VMEM((HEAD_DIM, rep * tq), F32)]
    if mode == "sel":
        assert S // SLC_LEN <= LANES and tk % SLC_LEN == 0
        in_specs += [pl.BlockSpec((1, 1, tq, LANES), lambda b, g, i: (b, g, i, 0))]
        args += [sel]
    return pl.pallas_call(
        functools.partial(_gqa_body, tq=tq, tk=tk, rep=rep, mode=mode, mult=mult),
        grid=(B, G, S // tq),
        in_specs=in_specs,
        out_specs=pl.BlockSpec((1, tq, w), lambda b, g, i: (b, i, g)),
        out_shape=jax.ShapeDtypeStruct((B, S, G * w), BF16),
        scratch_shapes=scratch,
        compiler_params=_cparams("parallel", "parallel", "arbitrary"),
        name="nsa_" + mode,
    )(*args)


def _nsa_gate_body(gl_ref, oc_ref, os_ref, ow_ref, o_ref, *, n_heads):
    ng = 3 * n_heads
    gl = gl_ref[:, :ng].astype(F32)
    gate = 1.0 / (1.0 + jnp.exp(-gl))
    src = lax.broadcasted_iota(jnp.int32, (ng, n_heads * HEAD_DIM), 0)
    head = lax.broadcasted_iota(jnp.int32, (ng, n_heads * HEAD_DIM), 1) // HEAD_DIM
    out = None
    for c, ref in enumerate((oc_ref, os_ref, ow_ref)):
        spread01 = jnp.where(src == head * 3 + c, 1.0, 0.0).astype(BF16)
        term = _split_dot(gate, spread01) * ref[...].astype(F32)
        out = term if out is None else out + term
    o_ref[...] = out.astype(o_ref.dtype)


def nsa_gate(gl, oc, os_, ow, n_heads, tm=256):
    M, W = oc.shape
    tm = _pick(M, tm)
    row = pl.BlockSpec((tm, W), lambda i: (i, 0))
    return pl.pallas_call(
        functools.partial(_nsa_gate_body, n_heads=n_heads),
        grid=(M // tm,),
        in_specs=[pl.BlockSpec((tm, gl.shape[1]), lambda i: (i, 0)), row, row, row],
        out_specs=row,
        out_shape=jax.ShapeDtypeStruct((M, W), BF16),
        compiler_params=_cparams("parallel"),
        name="nsa_gate",
    )(gl, oc, os_, ow)


def _xattn_body(h_ref, hb_ref, kv_ref, wq_ref, wo_ref, g_ref, b_ref, o32_ref, o16_ref, *, n_heads, scale, alpha):
    w = n_heads * HEAD_DIM
    q = jnp.dot(hb_ref[0], wq_ref[...], preferred_element_type=F32).astype(BF16)
    outs = []
    for hd in range(n_heads):
        sl = slice(hd * HEAD_DIM, (hd + 1) * HEAD_DIM)
        k = kv_ref[0, :, sl]
        v = kv_ref[0, :, w + hd * HEAD_DIM:w + (hd + 1) * HEAD_DIM]
        s = lax.dot_general(q[:, sl], k, _NT, preferred_element_type=F32) * scale
        p = jnp.exp(s - jnp.max(s, axis=-1, keepdims=True))
        p = p / jnp.sum(p, axis=-1, keepdims=True)
        outs.append(jnp.dot(p.astype(BF16), v, preferred_element_type=F32).astype(BF16))
    mix = jnp.dot(jnp.concatenate(outs, axis=1), wo_ref[...], preferred_element_type=F32)
    out = _layer_norm(alpha * h_ref[0] + mix, g_ref[...], b_ref[...])
    o32_ref[0] = out
    o16_ref[0] = out.astype(BF16)


def mem_attention_block(h, hb, kv, w_q, w_o, g, b, n_heads, alpha, tq=256):
    B, S, D = h.shape
    n_mem = kv.shape[1]
    w = n_heads * HEAD_DIM
    tq = _pick(S, tq)
    row = pl.BlockSpec((1, tq, D), lambda bb, i: (bb, i, 0))
    vec = pl.BlockSpec((1, D), lambda bb, i: (0, 0))
    return pl.pallas_call(
        functools.partial(_xattn_body, n_heads=n_heads, scale=HEAD_DIM ** -0.5, alpha=alpha),
        grid=(B, S // tq),
        in_specs=[row, row,
                  pl.BlockSpec((1, n_mem, 2 * w), lambda bb, i: (bb, 0, 0)),
                  pl.BlockSpec((D, w), lambda bb, i: (0, 0)),
                  pl.BlockSpec((w, D), lambda bb, i: (0, 0)), vec, vec],
        out_specs=[row, row],
        out_shape=[jax.ShapeDtypeStruct((B, S, D), F32), jax.ShapeDtypeStruct((B, S, D), BF16)],
        compiler_params=_cparams("parallel", "parallel"),
        name="mem_attention_block",
    )(h, hb, kv, w_q, w_o, g.reshape(1, D).astype(F32), b.reshape(1, D).astype(F32))


def _ffn_up_body(x_ref, wa_ref, wu_ref, cw_ref, o_ref, wab_ref, wub_ref, tail_ref, *, tiles_per_seq):
    i = pl.program_id(1)

    @pl.when(i == 0)
    def _():
        wab_ref[...] = wa_ref[0].astype(BF16)
        wub_ref[...] = wu_ref[0].astype(BF16)

    x = x_ref[...]
    a = jnp.dot(x, wab_ref[...], preferred_element_type=F32)
    tm = a.shape[0]
    first = (i % tiles_per_seq) == 0
    prev = jnp.where(first, 0.0, tail_ref[...])
    tail_ref[...] = a[tm - 8:]
    rowi = lax.broadcasted_iota(jnp.int32, a.shape, 0)
    a1 = jnp.where(rowi >= 1, pltpu.roll(a, 1, 0), prev[7:8])
    a2 = jnp.where(rowi >= 2, pltpu.roll(a, 2, 0), jnp.where(rowi == 1, prev[7:8], prev[6:7]))
    cw = cw_ref[...]
    gate = _gelu(cw[2:3] * a + cw[1:2] * a1 + cw[0:1] * a2)
    u = jnp.dot(x, wub_ref[...], preferred_element_type=F32)
    o_ref[...] = (gate * u).astype(o_ref.dtype)


def ffn_up_glu(x, w_up, layer, conv_w, seq_len, tm=1024, tn=256):
    M, K = x.shape
    Fd = w_up.shape[2] // 2
    tm, tn = _pick(seq_len, tm), _pick(Fd, tn)
    nj = Fd // tn
    return pl.pallas_call(
        functools.partial(_ffn_up_body, tiles_per_seq=seq_len // tm),
        grid=(nj, M // tm),
        in_specs=[pl.BlockSpec((tm, K), lambda j, i: (i, 0)),
                  pl.BlockSpec((1, K, tn), lambda j, i: (layer, 0, j)),
                  pl.BlockSpec((1, K, tn), lambda j, i: (layer, 0, j + nj)),
                  pl.BlockSpec((CONV_W, tn), lambda j, i: (0, j))],
        out_specs=pl.BlockSpec((tm, tn), lambda j, i: (i, j)),
        out_shape=jax.ShapeDtypeStruct((M, Fd), BF16),
        scratch_shapes=[pltpu.VMEM((K, tn), BF16), pltpu.VMEM((K, tn), BF16), pltpu.VMEM((8, tn), F32)],
        compiler_params=_cparams("parallel", "arbitrary"),
        name="ffn_up_glu",
    )(x, w_up, w_up, conv_w.astype(F32))


def _rope_tables(S):
    half = HEAD_DIM // 2
    inv_freq = ROPE_THETA ** (-jnp.arange(half, dtype=F32) / half)
    ang = jnp.arange(S, dtype=F32)[:, None] * inv_freq[None, :]
    cos, sin = jnp.cos(ang), jnp.sin(ang)
    return jnp.concatenate([cos, cos], axis=-1), jnp.concatenate([-sin, sin], axis=-1)


def _hgrn_sb_mixer(hb, B, S, w_in_all, lb_raw, norm_w, w_out, e):
    width = w_out.shape[0]
    a_heads = width // (2 * HEAD_DIM)
    b_heads = a_heads
    proj = matmul_f32w(hb, w_in_all, e).reshape(B, S, -1)
    o_a = hgrn2(proj, lb_raw, norm_w, a_heads, e)
    o_b = stick_breaking(proj, 4 * a_heads, 4 * a_heads + b_heads, 4 * a_heads + 2 * b_heads, b_heads)
    o = jnp.concatenate([o_a, o_b], axis=-1).reshape(B * S, width)
    return matmul(o, w_out.astype(BF16))


def _nsa_mixer(hb, B, S, w_in_all, o, cmp_pos, cmp_w1, cmp_w2, w_out, cos, sin):
    G = NSA_KV_HEADS
    q_w = w_out.shape[0]
    n_heads = q_w // HEAD_DIM
    rep = n_heads // G
    kv_w = G * HEAD_DIM
    main_w = q_w + 6 * kv_w
    proj = matmul_f32w(hb, w_in_all, o, n_cols=main_w).reshape(B, S, main_w)
    gl = matmul_f32w(hb, w_in_all, o, col0=main_w, n_cols=LANES, tn=LANES)

    log2_scale = HEAD_DIM ** -0.5 * LOG2E
    ks_rot = rope(proj, q_w + 2 * kv_w, G, cos, sin)
    kw_rot = rope(proj, q_w + 4 * kv_w, G, cos, sin)

    n16 = S // CMP_STRIDE
    kvc_in = proj[:, :, q_w:q_w + 2 * kv_w].reshape(B, S, 2, G, HEAD_DIM)
    x16 = kvc_in.transpose(2, 0, 3, 1, 4).reshape(2, B, G, n16, CMP_STRIDE * HEAD_DIM)
    pe = jnp.broadcast_to(cmp_pos.reshape(2, 1, CMP_LEN * HEAD_DIM), (2, 8, CMP_LEN * HEAD_DIM)).astype(BF16)
    w1 = cmp_w1.reshape(2, CMP_LEN * HEAD_DIM, HEAD_DIM).astype(BF16)
    kvc = compress(x16, pe, w1, cmp_w2.astype(BF16), log2_scale)

    o_c, imp = cmp_attention(proj, kvc[0], kvc[1], rep)
    sel = select_blocks(imp)
    col = lambda off: (q_w + off * kv_w) // HEAD_DIM
    o_s = gqa_attention(proj, cos, sin, log2_scale, ks_rot, proj, col(3), rep, "sel", sel=sel, tq=256, tk=512)
    o_w = gqa_attention(proj, cos, sin, log2_scale, kw_rot, proj, col(5), rep, "win", tq=256, tk=256)
    o = nsa_gate(gl, o_c.reshape(B * S, q_w), o_s.reshape(B * S, q_w), o_w.reshape(B * S, q_w), n_heads)
    return matmul(o, w_out.astype(BF16))


def kernel(x, mem, ab_w_in, hgrn_lb, hgrn_norm_w, ab_w_out, nsa_w_in, nsa_cmp_pos, nsa_cmp_w1,
           nsa_cmp_w2, nsa_w_out, xa_w_q, xa_w_kv, xa_w_o, ffn_w_up, ffn_conv, ffn_w_down, ln_g, ln_b):
    B, S, D = x.shape
    depth = ln_g.shape[0]
    alpha = (2 * depth) ** 0.25
    n_mem = mem.shape[1]
    cos, sin = _rope_tables(S)
    h = x.reshape(B * S, D).astype(F32)
    hb = h.astype(BF16)
    memb = mem.reshape(B * n_mem, D).astype(BF16)
    for layer in range(depth):
        if layer % 2 == 0:
            e = layer // 2
            mix = _hgrn_sb_mixer(hb, B, S, ab_w_in, hgrn_lb, hgrn_norm_w[e], ab_w_out[e], e)
        else:
            o = layer // 2
            mix = _nsa_mixer(hb, B, S, nsa_w_in, o, nsa_cmp_pos[o], nsa_cmp_w1[o], nsa_cmp_w2[o],
                             nsa_w_out[o], cos, sin)
        h, hb = add_layer_norm(h, mix, ln_g[layer, 0], ln_b[layer, 0], alpha)

        xkv = matmul(memb, xa_w_kv[layer].astype(BF16)).reshape(B, n_mem, -1)
        h3, hb3 = mem_attention_block(h.reshape(B, S, D), hb.reshape(B, S, D), xkv, xa_w_q[layer].astype(BF16),
                                      xa_w_o[layer].astype(BF16), ln_g[layer, 1], ln_b[layer, 1], XA_HEADS, alpha)
        h, hb = h3.reshape(B * S, D), hb3.reshape(B * S, D)

        gated = ffn_up_glu(hb, ffn_w_up, layer, ffn_conv[layer], S)
        h, hb = add_layer_norm(h, matmul(gated, ffn_w_down[layer].astype(BF16), tm=512, tk=gated.shape[1]),
                               ln_g[layer, 2], ln_b[layer, 2], alpha)
    return h.reshape(B, S, D).astype(x.dtype)
```

```python
import functools

import jax
import jax.numpy as jnp
from jax import lax
from jax.experimental import pallas as pl
from jax.experimental.pallas import tpu as pltpu

F32 = jnp.float32
BF16 = jnp.bfloat16

HEAD_DIM = 128
LANES = 128
HGRN_SUB = 16
HGRN_SAFE_LOG_DECAY = -60.0
NSA_KV_HEADS = 4
CMP_LEN = 32
CMP_STRIDE = 16
SLC_LEN = 64
SLC_TOP = 16
WINDOW = 512
XA_HEADS = 4
CONV_W = 3
ROPE_THETA = 10000.0
LN_EPS = 1e-5
RMS_EPS = 1e-6
NEG_INF = -1e30
FORCE_SCORE = 1e9
EXP_ZERO_BELOW = -104.0
SOFTPLUS_CLAMP = 80.0
LOG2E = 1.4426950408889634
VMEM_LIMIT = 52 * 1024 * 1024

_NT = (((1,), (1,)), ((), ()))
_TN = (((0,), (0,)), ((), ()))


def _cparams(*sem):
    return pltpu.CompilerParams(dimension_semantics=sem, vmem_limit_bytes=VMEM_LIMIT)


def _split_dot(a, b01):
    hi = a.astype(BF16)
    lo = (a - hi.astype(F32)).astype(BF16)
    return (jnp.dot(hi, b01, preferred_element_type=F32)
            + jnp.dot(lo, b01, preferred_element_type=F32))


def _mm_body(x_ref, w_ref, o_ref, *scratch, nk):
    prod = jnp.dot(x_ref[...], w_ref[...], preferred_element_type=F32)
    if nk == 1:
        o_ref[...] = prod.astype(o_ref.dtype)
        return
    acc_ref, = scratch
    k = pl.program_id(2)

    @pl.when(k == 0)
    def _():
        acc_ref[...] = prod

    @pl.when(k > 0)
    def _():
        acc_ref[...] += prod

    @pl.when(k == nk - 1)
    def _():
        o_ref[...] = acc_ref[...].astype(o_ref.dtype)


def _pick(n, pref):
    if n <= pref:
        return n
    t = pref
    while t >= LANES:
        if n % t == 0:
            return t
        t -= LANES
    return n


def matmul(x, w, out_dtype=BF16, tm=1024, tn=512, tk=4096, col0=0, n_cols=None):
    M, K = x.shape
    N = w.shape[1] if n_cols is None else n_cols
    tm, tn, tk = _pick(M, tm), _pick(N, tn), _pick(K, tk)
    assert col0 % tn == 0
    cb = col0 // tn
    nk = K // tk
    scratch = [] if nk == 1 else [pltpu.VMEM((tm, tn), F32)]
    return pl.pallas_call(
        functools.partial(_mm_body, nk=nk),
        grid=(M // tm, N // tn, nk),
        in_specs=[pl.BlockSpec((tm, tk), lambda i, j, k: (i, k)),
                  pl.BlockSpec((tk, tn), lambda i, j, k: (k, j + cb))],
        out_specs=pl.BlockSpec((tm, tn), lambda i, j, k: (i, j)),
        out_shape=jax.ShapeDtypeStruct((M, N), out_dtype),
        scratch_shapes=scratch,
        compiler_params=_cparams("parallel", "parallel", "arbitrary"),
        name="matmul",
    )(x, w)


def _mm_pair_body(xa_ref, xb_ref, wa_ref, wb_ref, o_ref):
    acc = jnp.dot(xa_ref[...], wa_ref[...], preferred_element_type=F32)
    acc = acc + jnp.dot(xb_ref[...], wb_ref[...], preferred_element_type=F32)
    o_ref[...] = acc.astype(o_ref.dtype)


def matmul_pair(xa, xb, w, out_dtype=BF16, tm=1024, tn=512):
    M, Kh = xa.shape
    assert xb.shape == xa.shape and w.shape[0] == 2 * Kh
    N = w.shape[1]
    tm, tn = _pick(M, tm), _pick(N, tn)
    xs = pl.BlockSpec((tm, Kh), lambda i, j: (i, 0))
    return pl.pallas_call(
        _mm_pair_body,
        grid=(M // tm, N // tn),
        in_specs=[xs, xs, pl.BlockSpec((Kh, tn), lambda i, j: (0, j)), pl.BlockSpec((Kh, tn), lambda i, j: (1, j))],
        out_specs=pl.BlockSpec((tm, tn), lambda i, j: (i, j)),
        out_shape=jax.ShapeDtypeStruct((M, N), out_dtype),
        compiler_params=_cparams("parallel", "parallel"),
        name="matmul_pair",
    )(xa, xb, w, w)


def _mm_f32w_body(x_ref, w_ref, o_ref, wb_ref):
    @pl.when(pl.program_id(1) == 0)
    def _():
        wb_ref[...] = w_ref[0].astype(BF16)

    o_ref[...] = jnp.dot(x_ref[...], wb_ref[...], preferred_element_type=F32).astype(o_ref.dtype)


def matmul_f32w(x, w, layer, out_dtype=BF16, tm=1024, tn=512, col0=0, n_cols=None):
    M, K = x.shape
    N = w.shape[2] if n_cols is None else n_cols
    tm, tn = _pick(M, tm), _pick(N, tn)
    assert col0 % tn == 0
    cb = col0 // tn
    return pl.pallas_call(
        _mm_f32w_body,
        grid=(N // tn, M // tm),
        in_specs=[pl.BlockSpec((tm, K), lambda j, i: (i, 0)),
                  pl.BlockSpec((1, K, tn), lambda j, i: (layer, 0, j + cb))],
        out_specs=pl.BlockSpec((tm, tn), lambda j, i: (i, j)),
        out_shape=jax.ShapeDtypeStruct((M, N), out_dtype),
        scratch_shapes=[pltpu.VMEM((K, tn), BF16)],
        compiler_params=_cparams("parallel", "arbitrary"),
        name="matmul_f32w",
    )(x, w)


def _layer_norm(y, g, b):
    mu = jnp.mean(y, axis=-1, keepdims=True)
    d = y - mu
    var = jnp.mean(d * d, axis=-1, keepdims=True)
    return d * lax.rsqrt(var + LN_EPS) * g + b


def _add_ln_body(h_ref, m_ref, g_ref, b_ref, o32_ref, o16_ref, *, alpha):
    out = _layer_norm(alpha * h_ref[...] + m_ref[...].astype(F32), g_ref[...], b_ref[...])
    o32_ref[...] = out
    o16_ref[...] = out.astype(BF16)


def add_layer_norm(h, mix, g, b, alpha, tm=256):
    M, D = h.shape
    tm = _pick(M, tm)
    row = pl.BlockSpec((tm, D), lambda i: (i, 0))
    vec = pl.BlockSpec((1, D), lambda i: (0, 0))
    return pl.pallas_call(
        functools.partial(_add_ln_body, alpha=alpha),
        grid=(M // tm,),
        in_specs=[row, row, vec, vec],
        out_specs=[row, row],
        out_shape=[jax.ShapeDtypeStruct((M, D), F32), jax.ShapeDtypeStruct((M, D), BF16)],
        compiler_params=_cparams("parallel"),
        name="add_layer_norm",
    )(h, mix, g.reshape(1, D).astype(F32), b.reshape(1, D).astype(F32))


def _hgrn_body(q_ref, f_ref, i_ref, g_ref, lb_ref, nw_ref, o_ref, st_ref, *, ts, layer_idx, nh):
    C = HGRN_SUB
    nsub = ts // C
    heads = [slice(h * HEAD_DIM, (h + 1) * HEAD_DIM) for h in range(nh)]

    @pl.when(pl.program_id(2) == 0)
    def _():
        st_ref[...] = jnp.zeros_like(st_ref)

    lbr = lb_ref[...]
    ex = jnp.exp(lbr - jnp.max(lbr, axis=0, keepdims=True))
    sm = ex / jnp.sum(ex, axis=0, keepdims=True)
    lb = jnp.sum(sm[:layer_idx + 1], axis=0, keepdims=True)

    q = q_ref[0].astype(F32)
    z = f_ref[0].astype(F32)
    v = i_ref[0]
    e = jnp.exp(-jnp.abs(z))
    r = 1.0 / (1.0 + e)
    pos = z >= 0
    sig = jnp.where(pos, r, e * r)
    nsig = jnp.where(pos, e * r, r)
    logf = jnp.log(lb + (1.0 - lb) * sig)
    k = (1.0 - lb) * nsig

    hi = logf.astype(BF16)
    lo = (logf - hi.astype(F32)).astype(BF16)
    row = lax.broadcasted_iota(jnp.int32, (ts, ts), 0)
    col = lax.broadcasted_iota(jnp.int32, (ts, ts), 1)

    def cumdot(m01):
        return jnp.dot(m01, hi, preferred_element_type=F32) + jnp.dot(m01, lo, preferred_element_type=F32)

    def whole_tile(sts):
        b = cumdot(jnp.where(row >= col, 1.0, 0.0).astype(BF16))
        bl = b[ts - 1:ts]
        qd = (q * jnp.exp(b)).astype(BF16)
        kinv = (k * jnp.exp(-b)).astype(BF16)
        kd = (k * jnp.exp(bl - b)).astype(BF16)
        dec = jnp.exp(bl)
        outs, new = [], []
        for h, sl in enumerate(heads):
            dmat = lax.dot_general(qd[:, sl], kinv[:, sl], _NT, preferred_element_type=F32)
            dmat = jnp.where(row >= col, dmat, 0.0)
            o = jnp.dot(dmat.astype(BF16), v[:, sl], preferred_element_type=F32)
            outs.append(o + lax.dot_general(qd[:, sl], sts[h].astype(BF16), _NT, preferred_element_type=F32))
            new.append(sts[h] * dec[:, sl] + lax.dot_general(v[:, sl], kd[:, sl], _TN, preferred_element_type=F32))
        return tuple(outs), tuple(new)

    def sub_chunks(sts):
        same = (row // C) == (col // C)
        b = cumdot(jnp.where(same & (row >= col), 1.0, 0.0).astype(BF16))
        bl = cumdot(jnp.where(same, 1.0, 0.0).astype(BF16))
        qd = (q * jnp.exp(b)).astype(BF16)
        kd = (k * jnp.exp(bl - b)).astype(BF16)
        dec = jnp.exp(bl)
        tri = (lax.broadcasted_iota(jnp.int32, (C, C, HEAD_DIM), 0)
               >= lax.broadcasted_iota(jnp.int32, (C, C, HEAD_DIM), 1))
        outs, new = [], []
        for h, hs in enumerate(heads):
            st = sts[h]
            parts = []
            for n in range(nsub):
                sl = slice(n * C, (n + 1) * C)
                bn, qn, kn = b[sl, hs], q[sl, hs], k[sl, hs]
                diff = bn[:, None, :] - bn[None, :, :]
                ee = jnp.exp(jnp.where(tri, diff, NEG_INF))
                dmat = jnp.sum(qn[:, None, :] * (kn[None, :, :] * ee), axis=-1)
                o_n = jnp.dot(dmat.astype(BF16), v[sl, hs], preferred_element_type=F32)
                o_n = o_n + lax.dot_general(qd[sl, hs], st.astype(BF16), _NT, preferred_element_type=F32)
                upd = lax.dot_general(v[sl, hs], kd[sl, hs], _TN, preferred_element_type=F32)
                st = st * dec[n * C:n * C + 1, hs] + upd
                parts.append(o_n)
            outs.append(jnp.concatenate(parts, axis=0))
            new.append(st)
        return tuple(outs), tuple(new)

    tile_decay = jnp.min(jnp.sum(logf, axis=0, keepdims=True))
    outs, sts = lax.cond(tile_decay > HGRN_SAFE_LOG_DECAY, whole_tile, sub_chunks,
                         tuple(st_ref[h] for h in range(nh)))
    gt = g_ref[0].astype(F32)
    gate = gt / (1.0 + jnp.exp(-gt))
    for h, sl in enumerate(heads):
        st_ref[h] = sts[h]
        o = outs[h]
        o = o * lax.rsqrt(jnp.mean(o * o, axis=-1, keepdims=True) + RMS_EPS) * nw_ref[...] * gate[:, sl]
        o_ref[0, :, sl] = o.astype(o_ref.dtype)


def hgrn2(proj, lb_raw, norm_w, n_heads, layer_idx, ts=128, nh=8):
    B, S, _ = proj.shape
    ts = _pick(S, ts)
    nh = min(nh, n_heads)
    assert n_heads % nh == 0
    H = n_heads // nh
    L = lb_raw.shape[0]
    w = nh * HEAD_DIM

    def col(off):
        return pl.BlockSpec((1, ts, w), lambda b, h, s: (b, s, off * H + h))

    return pl.pallas_call(
        functools.partial(_hgrn_body, ts=ts, layer_idx=layer_idx, nh=nh),
        grid=(B, H, S // ts),
        in_specs=[col(0), col(1), col(2), col(3),
                  pl.BlockSpec((L, w), lambda b, h, s: (0, h)),
                  pl.BlockSpec((1, HEAD_DIM), lambda b, h, s: (0, 0))],
        out_specs=pl.BlockSpec((1, ts, w), lambda b, h, s: (b, s, h)),
        out_shape=jax.ShapeDtypeStruct((B, S, n_heads * HEAD_DIM), BF16),
        scratch_shapes=[pltpu.VMEM((nh, HEAD_DIM, HEAD_DIM), F32)],
        compiler_params=_cparams("parallel", "parallel", "arbitrary"),
        name="hgrn2",
    )(proj, proj, proj, proj, lb_raw.astype(F32), norm_w.reshape(1, HEAD_DIM).astype(F32))


def _sb_block(q, k, v, carry, after01, scale, mask):
    z = lax.dot_general(q, k, _NT, preferred_element_type=F32) * scale
    sp = jnp.maximum(z, jnp.log(1.0 + jnp.exp(jnp.minimum(z, SOFTPLUS_CLAMP))))
    spm = sp if mask is None else jnp.where(mask, sp, 0.0)
    sub = after01.shape[0]
    parts, tail = [], 0.0
    for c in reversed(range(spm.shape[1] // sub)):
        blk = spm[:, c * sub:(c + 1) * sub]
        part = _split_dot(blk, after01) + tail
        parts.insert(0, part)
        tail = part[:, 0:1] + blk[:, 0:1]
    rev = parts[0] if len(parts) == 1 else jnp.concatenate(parts, axis=1)
    w = jnp.exp(z - sp - rev - carry)
    if mask is not None:
        w = jnp.where(mask, w, 0.0)
    contrib = jnp.dot(w.astype(BF16), v, preferred_element_type=F32)
    return contrib, carry + tail


def _sb_body(q_ref, k_ref, v_ref, o_ref, *, tq, first, scale, nh):
    qi = pl.program_id(2)
    heads = [slice(h * HEAD_DIM, (h + 1) * HEAD_DIM) for h in range(nh)]
    qs = [q_ref[0, :, sl] for sl in heads]

    def after01(width):
        return jnp.where(lax.broadcasted_iota(jnp.int32, (width, width), 0)
                         > lax.broadcasted_iota(jnp.int32, (width, width), 1), 1.0, 0.0).astype(BF16)

    def blocks(k0, width, after, carries, mask):
        res = [_sb_block(qs[h], k_ref[0, pl.ds(k0, width), heads[h]], v_ref[0, pl.ds(k0, width), heads[h]],
                         carries[h], after, scale, mask) for h in range(nh)]
        return tuple(r[0] for r in res), tuple(r[1] for r in res)

    k0 = pl.multiple_of(jnp.maximum((qi + 1) * tq - first, 0), tq)
    t = qi * tq + lax.broadcasted_iota(jnp.int32, (tq, first), 0)
    kpos = k0 + lax.broadcasted_iota(jnp.int32, (tq, first), 1)
    after_tq = after01(tq)
    accs, carries = blocks(k0, first, after_tq, (jnp.zeros((tq, 1), F32),) * nh, kpos < t)

    def cond(c):
        j, _, carries = c
        low = functools.reduce(jnp.minimum, [jnp.min(cr) for cr in carries])
        return jnp.logical_and(j >= 0, low < -EXP_ZERO_BELOW)

    def body(c):
        j, accs, carries = c
        contribs, carries = blocks(pl.multiple_of(j * tq, tq), tq, after_tq, carries, None)
        return j - 1, tuple(a + cb for a, cb in zip(accs, contribs)), carries

    _, accs, _ = lax.while_loop(cond, body, (k0 // tq - 1, accs, carries))
    for h in range(nh):
        o_ref[0, :, heads[h]] = accs[h].astype(o_ref.dtype)


def stick_breaking(proj, col_q, col_k, col_v, n_heads, tq=256, nh=4):
    B, S, _ = proj.shape
    tq = _pick(S, tq)
    first = min(2 * tq, S)
    assert n_heads % nh == 0 and col_q % nh == 0 and col_k % nh == 0 and col_v % nh == 0
    w = nh * HEAD_DIM
    full = lambda off: pl.BlockSpec((1, S, w), lambda b, h, i: (b, 0, off // nh + h))
    return pl.pallas_call(
        functools.partial(_sb_body, tq=tq, first=first, scale=HEAD_DIM ** -0.5, nh=nh),
        grid=(B, n_heads // nh, S // tq),
        in_specs=[pl.BlockSpec((1, tq, w), lambda b, h, i: (b, i, col_q // nh + h)),
                  full(col_k), full(col_v)],
        out_specs=pl.BlockSpec((1, tq, w), lambda b, h, i: (b, i, h)),
        out_shape=jax.ShapeDtypeStruct((B, S, n_heads * HEAD_DIM), BF16),
        compiler_params=_cparams("parallel", "parallel", "arbitrary"),
        name="stick_breaking",
    )(proj, proj, proj)


def _rope_body(x_ref, cos_ref, sin_ref, o_ref, *, n_heads):
    cos = cos_ref[...]
    sin = sin_ref[...]
    for h in range(n_heads):
        sl = slice(h * HEAD_DIM, (h + 1) * HEAD_DIM)
        t = x_ref[0, :, sl].astype(F32)
        o_ref[0, :, sl] = (t * cos + pltpu.roll(t, HEAD_DIM // 2, 1) * sin).astype(o_ref.dtype)


def rope(x, col0, n_heads, cos, sin, ts=256):
    B, S, _ = x.shape
    ts = _pick(S, ts)
    w = n_heads * HEAD_DIM
    assert col0 % w == 0
    cb = col0 // w
    tab = pl.BlockSpec((ts, HEAD_DIM), lambda b, s: (s, 0))
    return pl.pallas_call(
        functools.partial(_rope_body, n_heads=n_heads),
        grid=(B, S // ts),
        in_specs=[pl.BlockSpec((1, ts, w), lambda b, s: (b, s, cb)), tab, tab],
        out_specs=pl.BlockSpec((1, ts, w), lambda b, s: (b, s, 0)),
        out_shape=jax.ShapeDtypeStruct((B, S, w), BF16),
        compiler_params=_cparams("parallel", "parallel"),
        name="rope",
    )(x, cos, sin)


def _gelu(x):
    return 0.5 * x * (1.0 + lax.erf(x * (2.0 ** -0.5)))


def _compress_body(x_ref, pe_ref, w1_ref, w2_ref, o_ref, *, batch, k_mult):
    x = x_ref[0, 0]
    half = x.shape[1]
    n16 = x.shape[0]
    y1 = jnp.dot(x, w1_ref[0, :half], preferred_element_type=F32)
    y2 = jnp.dot(x, w1_ref[0, half:], preferred_element_type=F32)
    bias = jnp.dot(pe_ref[0], w1_ref[0], preferred_element_type=F32)[0:1]
    hid = _gelu(y1 + pltpu.roll(y2, n16 - 1, 0) + bias)
    out = jnp.dot(hid.astype(BF16), w2_ref[0], preferred_element_type=F32)
    mult = jnp.where(pl.program_id(0) < batch, k_mult, 1.0)
    o_ref[0, 0] = (out * mult).astype(o_ref.dtype)


def compress(x16, pe, w1, w2, k_mult):
    two, B, G, n16, wide = x16.shape
    x16 = x16.reshape(two * B, G, n16, wide)
    out = pl.pallas_call(
        functools.partial(_compress_body, batch=B, k_mult=k_mult),
        grid=(two * B, G),
        in_specs=[pl.BlockSpec((1, 1, n16, wide), lambda i, g: (i, g, 0, 0)),
                  pl.BlockSpec((1, 8, 2 * wide), lambda i, g: (i // B, 0, 0)),
                  pl.BlockSpec((1, 2 * wide, HEAD_DIM), lambda i, g: (i // B, 0, 0)),
                  pl.BlockSpec((1, HEAD_DIM, HEAD_DIM), lambda i, g: (i // B, 0, 0))],
        out_specs=pl.BlockSpec((1, 1, n16, HEAD_DIM), lambda i, g: (i, g, 0, 0)),
        out_shape=jax.ShapeDtypeStruct((two * B, G, n16, HEAD_DIM), BF16),
        compiler_params=_cparams("parallel", "parallel"),
        name="nsa_compress",
    )(x16, pe, w1, w2)
    return out.reshape(two, B, G, n16, HEAD_DIM)


def _stack_heads(q_ref, rep):
    return jnp.concatenate([q_ref[0, :, r * HEAD_DIM:(r + 1) * HEAD_DIM] for r in range(rep)], axis=0)


def _cmp_body(q_ref, kc_ref, vc_ref, o_ref, imp_ref, *, tq, rep):
    q0 = pl.program_id(2) * tq
    n16 = kc_ref.shape[2]
    cols = [slice(r * tq, (r + 1) * tq) for r in range(rep)]
    q2 = _stack_heads(q_ref, rep)

    def attend(nb):
        st = lax.dot_general(kc_ref[0, 0, :nb], q2, _NT, preferred_element_type=F32)
        n = lax.broadcasted_iota(jnp.int32, (nb, tq), 0)
        t = q0 + lax.broadcasted_iota(jnp.int32, (nb, tq), 1)
        bias = jnp.where(n * CMP_STRIDE + (CMP_LEN - 1) <= t, 0.0, NEG_INF)
        ps = []
        psum = jnp.zeros((nb, tq), F32)
        for r in range(rep):
            s = st[:, cols[r]] + bias
            m = jnp.maximum(jnp.max(s, axis=0, keepdims=True), 0.1 * NEG_INF)
            p = jnp.exp2(s - m)
            den = jnp.sum(p, axis=0, keepdims=True)
            pn = p * (1.0 / jnp.where(den > 0.0, den, 1.0))
            ps.append(pn.astype(BF16))
            psum = psum + pn
        ot = lax.dot_general(vc_ref[0, 0, :nb], jnp.concatenate(ps, axis=1), _TN,
                             preferred_element_type=F32)
        for r in range(rep):
            o_ref[0, :, r * HEAD_DIM:(r + 1) * HEAD_DIM] = ot[:, cols[r]].T.astype(o_ref.dtype)

        cj = lax.broadcasted_iota(jnp.int32, (LANES, nb), 0) * SLC_LEN
        cn = lax.broadcasted_iota(jnp.int32, (LANES, nb), 1) * CMP_STRIDE
        ov01 = jnp.where((cn < cj + SLC_LEN) & (cn + CMP_LEN > cj), 1.0, 0.0).astype(BF16)
        hi = psum.astype(BF16)
        lo = (psum - hi.astype(F32)).astype(BF16)
        imp_t = jnp.dot(ov01, hi, preferred_element_type=F32) + jnp.dot(ov01, lo, preferred_element_type=F32)
        imp_ref[0, 0] = imp_t.T

    n_chunks = max(n16 // LANES, 1)
    chunk = n16 // n_chunks
    last_valid = (q0 + tq - CMP_LEN) // CMP_STRIDE
    need = jnp.clip(last_valid // chunk + 1, 1, n_chunks)
    for c in range(1, n_chunks + 1):
        pl.when(need == c)(functools.partial(attend, c * chunk))


def cmp_attention(q, kc, vc, rep, tq=256):
    B, S, _ = q.shape
    G, n16 = kc.shape[1], kc.shape[2]
    tq = _pick(S, tq)
    w = rep * HEAD_DIM
    kv = pl.BlockSpec((1, 1, n16, HEAD_DIM), lambda b, g, i: (b, g, 0, 0))
    return pl.pallas_call(
        functools.partial(_cmp_body, tq=tq, rep=rep),
        grid=(B, G, S // tq),
        in_specs=[pl.BlockSpec((1, tq, w), lambda b, g, i: (b, i, g)), kv, kv],
        out_specs=[pl.BlockSpec((1, tq, w), lambda b, g, i: (b, i, g)),
                   pl.BlockSpec((1, 1, tq, LANES), lambda b, g, i: (b, g, i, 0))],
        out_shape=[jax.ShapeDtypeStruct((B, S, G * w), BF16),
                   jax.ShapeDtypeStruct((B, G, S, LANES), F32)],
        compiler_params=_cparams("parallel", "parallel", "parallel"),
        name="nsa_cmp",
    )(q, kc, vc)


def _topk_body(imp_ref, sel_ref, *, tq, n_slc):
    q0 = pl.program_id(2) * tq
    imp = imp_ref[0, 0]
    tt = q0 + lax.broadcasted_iota(jnp.int32, (tq, LANES), 0)
    j = lax.broadcasted_iota(jnp.int32, (tq, LANES), 1)
    cur = tt // SLC_LEN
    forced = (j == 0) | (j == cur) | (j == cur - 1)
    allowed = j * SLC_LEN <= tt
    score = jnp.where(forced, FORCE_SCORE, jnp.where(allowed, imp, -1.0))
    score = jnp.where(j < n_slc, score, -jnp.inf)
    jf = j.astype(F32)
    sel = jnp.zeros((tq, LANES), F32)
    for _ in range(min(SLC_TOP, n_slc)):
        m = jnp.max(score, axis=-1, keepdims=True)
        first = jnp.min(jnp.where(score == m, jf, float(LANES)), axis=-1, keepdims=True)
        pick = jf == first
        sel = jnp.where(pick, 1.0, sel)
        score = jnp.where(pick, -jnp.inf, score)
    sel_ref[0, 0] = sel.astype(sel_ref.dtype)


def select_blocks(imp, tq=1024):
    B, G, S, _ = imp.shape
    n_slc = S // SLC_LEN
    assert n_slc <= LANES
    tq = _pick(S, tq)
    spec = pl.BlockSpec((1, 1, tq, LANES), lambda b, g, i: (b, g, i, 0))
    return pl.pallas_call(
        functools.partial(_topk_body, tq=tq, n_slc=n_slc),
        grid=(B, G, S // tq),
        in_specs=[spec],
        out_specs=spec,
        out_shape=jax.ShapeDtypeStruct((B, G, S, LANES), BF16),
        compiler_params=_cparams("parallel", "parallel", "parallel"),
        name="nsa_topk",
    )(imp)


def _gqa_body(*refs, tq, tk, rep, mode, mult):
    q0 = pl.program_id(2) * tq
    q_ref, cos_ref, sin_ref, k_ref, v_ref = refs[:5]
    cos = cos_ref[...] * mult
    sin = sin_ref[...] * mult
    rot = []
    for r in range(rep):
        x = q_ref[0, :, r * HEAD_DIM:(r + 1) * HEAD_DIM].astype(F32)
        rot.append((x * cos + pltpu.roll(x, HEAD_DIM // 2, 1) * sin).astype(BF16))
    q2 = jnp.concatenate(rot, axis=0)
    if mode == "sel":
        sel_ref, o_ref, s_ref, acc_ref = refs[5:]
        key_blk = lax.broadcasted_iota(jnp.int32, (tk, LANES), 0) // SLC_LEN
        lane_blk = lax.broadcasted_iota(jnp.int32, (tk, LANES), 1)
        unpicked = ((sel_ref[0, 0].astype(F32) - 1.0) * (-NEG_INF)).astype(BF16)
        q2 = jnp.concatenate([q2, jnp.concatenate([unpicked] * rep, axis=0)], axis=1)
    else:
        o_ref, s_ref, acc_ref = refs[5:]
    kpos = lax.broadcasted_iota(jnp.int32, (tk, tq), 0)
    t = q0 + lax.broadcasted_iota(jnp.int32, (tk, tq), 1)
    cols = [slice(r * tq, (r + 1) * tq) for r in range(rep)]

    def put_scores(slot, kj):
        k0 = pl.multiple_of(kj * tk, tk)
        keys = k_ref[0, pl.ds(k0, tk), :]
        if mode == "sel":
            onehot = jnp.where(key_blk + kj * (tk // SLC_LEN) == lane_blk, 1.0, 0.0).astype(BF16)
            keys = jnp.concatenate([keys, onehot], axis=1)
        s_ref[slot] = lax.dot_general(keys, q2, _NT, preferred_element_type=F32)

    def tile(slot, kj, m, l, diagonal):
        k0 = pl.multiple_of(kj * tk, tk)
        kp = k0 + kpos
        if mode == "sel":
            bias = jnp.where(kp <= t, 0.0, NEG_INF) if diagonal else None
        else:
            ok = kp > t - WINDOW
            if diagonal:
                ok = ok & (kp <= t)
            bias = jnp.where(ok, 0.0, NEG_INF)
        ps, m_new, l_new, scale = [], [], [], []
        for r in range(rep):
            s = s_ref[slot, :, cols[r]]
            if bias is None:
                mr = jnp.maximum(m[r], jnp.max(s, axis=0, keepdims=True))
                p = jnp.exp2(s - mr)
            else:
                mr = jnp.maximum(m[r], jnp.max(s + bias, axis=0, keepdims=True))
                p = jnp.exp2((s - mr) + bias)
            a = jnp.exp2(m[r] - mr)
            ps.append(p.astype(BF16))
            m_new.append(mr)
            l_new.append(a * l[r] + jnp.sum(p, axis=0, keepdims=True))
            scale.append(a)
        pt = jnp.concatenate(ps, axis=1)
        pv = lax.dot_general(v_ref[0, pl.ds(k0, tk), :], pt, _TN, preferred_element_type=F32)
        acc_ref[...] = jnp.concatenate(scale, axis=1) * acc_ref[...] + pv
        return tuple(m_new), tuple(l_new)

    kd = q0 // tk
    lo = 0 if mode == "sel" else jnp.maximum(q0 - (WINDOW - 1), 0) // tk
    put_scores(0, kd)
    put_scores(1, lo)
    acc_ref[...] = jnp.zeros_like(acc_ref)
    stats = tile(0, kd, (jnp.full((1, tq), NEG_INF, F32),) * rep, (jnp.zeros((1, tq), F32),) * rep, True)

    def pair(i, stats):
        kj = lo + 2 * i
        put_scores(0, kj + 1)
        stats = tile(1, kj, *stats, False)
        put_scores(1, kj + 2)
        return tile(0, kj + 1, *stats, False)

    n_off = kd - lo
    stats = lax.fori_loop(0, n_off // 2, pair, stats)
    m, l = lax.cond(n_off % 2 == 1, lambda st: tile(1, kd - 1, *st, False), lambda st: st, stats)
    for r in range(rep):
        out = acc_ref[:, cols[r]] * (1.0 / l[r])
        o_ref[0, :, r * HEAD_DIM:(r + 1) * HEAD_DIM] = out.T.astype(o_ref.dtype)


def gqa_attention(q, cos, sin, mult, k, v, v_col0, rep, mode, sel=None, tq=128, tk=512):
    B, S, _ = q.shape
    G = k.shape[2] // HEAD_DIM
    tq, tk = _pick(S, tq), _pick(S, tk)
    assert tk % tq == 0
    w = rep * HEAD_DIM
    tab = pl.BlockSpec((tq, HEAD_DIM), lambda b, g, i: (i, 0))
    in_specs = [pl.BlockSpec((1, tq, w), lambda b, g, i: (b, i, g)), tab, tab,
                pl.BlockSpec((1, S, HEAD_DIM), lambda b, g, i: (b, 0, g)),
                pl.BlockSpec((1, S, HEAD_DIM), lambda b, g, i: (b, 0, v_col0 + g))]
    args = [q, cos, sin, k, v]
    scratch = [pltpu.VMEM((2, tk, rep * tq), F32), pltpu.VMEM((HEAD_DIM, rep * tq), F32)]
    if mode == "sel":
        assert S // SLC_LEN <= LANES and tk % SLC_LEN == 0
        in_specs += [pl.BlockSpec((1, 1, tq, LANES), lambda b, g, i: (b, g, i, 0))]
        args += [sel]
    return pl.pallas_call(
        functools.partial(_gqa_body, tq=tq, tk=tk, rep=rep, mode=mode, mult=mult),
        grid=(B, G, S // tq),
        in_specs=in_specs,
        out_specs=pl.BlockSpec((1, tq, w), lambda b, g, i: (b, i, g)),
        out_shape=jax.ShapeDtypeStruct((B, S, G * w), BF16),
        scratch_shapes=scratch,
        compiler_params=_cparams("parallel", "parallel", "arbitrary"),
        name="nsa_" + mode,
    )(*args)


def _nsa_gate_body(gl_ref, oc_ref, os_ref, ow_ref, o_ref, *, n_heads):
    ng = 3 * n_heads
    gl = gl_ref[:, :ng].astype(F32)
    gate = 1.0 / (1.0 + jnp.exp(-gl))
    src = lax.broadcasted_iota(jnp.int32, (ng, n_heads * HEAD_DIM), 0)
    head = lax.broadcasted_iota(jnp.int32, (ng, n_heads * HEAD_DIM), 1) // HEAD_DIM
    out = None
    for c, ref in enumerate((oc_ref, os_ref, ow_ref)):
        spread01 = jnp.where(src == head * 3 + c, 1.0, 0.0).astype(BF16)
        term = _split_dot(gate, spread01) * ref[...].astype(F32)
        out = term if out is None else out + term
    o_ref[...] = out.astype(o_ref.dtype)


def nsa_gate(gl, oc, os_, ow, n_heads, tm=256):
    M, W = oc.shape
    tm = _pick(M, tm)
    row = pl.BlockSpec((tm, W), lambda i: (i, 0))
    return pl.pallas_call(
        functools.partial(_nsa_gate_body, n_heads=n_heads),
        grid=(M // tm,),
        in_specs=[pl.BlockSpec((tm, gl.shape[1]), lambda i: (i, 0)), row, row, row],
        out_specs=row,
        out_shape=jax.ShapeDtypeStruct((M, W), BF16),
        compiler_params=_cparams("parallel"),
        name="nsa_gate",
    )(gl, oc, os_, ow)


def _xattn_body(h_ref, hb_ref, kv_ref, wq_ref, wo_ref, g_ref, b_ref, o32_ref, o16_ref, *, n_heads, scale, alpha):
    w = n_heads * HEAD_DIM
    q = jnp.dot(hb_ref[0], wq_ref[...], preferred_element_type=F32).astype(BF16)
    outs = []
    for hd in range(n_heads):
        sl = slice(hd * HEAD_DIM, (hd + 1) * HEAD_DIM)
        k = kv_ref[0, :, sl]
        v = kv_ref[0, :, w + hd * HEAD_DIM:w + (hd + 1) * HEAD_DIM]
        s = lax.dot_general(q[:, sl], k, _NT, preferred_element_type=F32) * scale
        p = jnp.exp(s - jnp.max(s, axis=-1, keepdims=True))
        p = p / jnp.sum(p, axis=-1, keepdims=True)
        outs.append(jnp.dot(p.astype(BF16), v, preferred_element_type=F32).astype(BF16))
    mix = jnp.dot(jnp.concatenate(outs, axis=1), wo_ref[...], preferred_element_type=F32)
    out = _layer_norm(alpha * h_ref[0] + mix, g_ref[...], b_ref[...])
    o32_ref[0] = out
    o16_ref[0] = out.astype(BF16)


def mem_attention_block(h, hb, kv, w_q, w_o, g, b, n_heads, alpha, tq=256):
    B, S, D = h.shape
    n_mem = kv.shape[1]
    w = n_heads * HEAD_DIM
    tq = _pick(S, tq)
    row = pl.BlockSpec((1, tq, D), lambda bb, i: (bb, i, 0))
    vec = pl.BlockSpec((1, D), lambda bb, i: (0, 0))
    return pl.pallas_call(
        functools.partial(_xattn_body, n_heads=n_heads, scale=HEAD_DIM ** -0.5, alpha=alpha),
        grid=(B, S // tq),
        in_specs=[row, row,
                  pl.BlockSpec((1, n_mem, 2 * w), lambda bb, i: (bb, 0, 0)),
                  pl.BlockSpec((D, w), lambda bb, i: (0, 0)),
                  pl.BlockSpec((w, D), lambda bb, i: (0, 0)), vec, vec],
        out_specs=[row, row],
        out_shape=[jax.ShapeDtypeStruct((B, S, D), F32), jax.ShapeDtypeStruct((B, S, D), BF16)],
        compiler_params=_cparams("parallel", "parallel"),
        name="mem_attention_block",
    )(h, hb, kv, w_q, w_o, g.reshape(1, D).astype(F32), b.reshape(1, D).astype(F32))


def _ffn_up_body(x_ref, wa_ref, wu_ref, cw_ref, o_ref, wab_ref, wub_ref, tail_ref, *, tiles_per_seq):
    i = pl.program_id(1)

    @pl.when(i == 0)
    def _():
        wab_ref[...] = wa_ref[0].astype(BF16)
        wub_ref[...] = wu_ref[0].astype(BF16)

    x = x_ref[...]
    a = jnp.dot(x, wab_ref[...], preferred_element_type=F32)
    tm = a.shape[0]
    first = (i % tiles_per_seq) == 0
    prev = jnp.where(first, 0.0, tail_ref[...])
    tail_ref[...] = a[tm - 8:]
    rowi = lax.broadcasted_iota(jnp.int32, a.shape, 0)
    a1 = jnp.where(rowi >= 1, pltpu.roll(a, 1, 0), prev[7:8])
    a2 = jnp.where(rowi >= 2, pltpu.roll(a, 2, 0), jnp.where(rowi == 1, prev[7:8], prev[6:7]))
    cw = cw_ref[...]
    gate = _gelu(cw[2:3] * a + cw[1:2] * a1 + cw[0:1] * a2)
    u = jnp.dot(x, wub_ref[...], preferred_element_type=F32)
    o_ref[...] = (gate * u).astype(o_ref.dtype)


def ffn_up_glu(x, w_up, layer, conv_w, seq_len, tm=1024, tn=256):
    M, K = x.shape
    Fd = w_up.shape[2] // 2
    tm, tn = _pick(seq_len, tm), _pick(Fd, tn)
    nj = Fd // tn
    return pl.pallas_call(
        functools.partial(_ffn_up_body, tiles_per_seq=seq_len // tm),
        grid=(nj, M // tm),
        in_specs=[pl.BlockSpec((tm, K), lambda j, i: (i, 0)),
                  pl.BlockSpec((1, K, tn), lambda j, i: (layer, 0, j)),
                  pl.BlockSpec((1, K, tn), lambda j, i: (layer, 0, j + nj)),
                  pl.BlockSpec((CONV_W, tn), lambda j, i: (0, j))],
        out_specs=pl.BlockSpec((tm, tn), lambda j, i: (i, j)),
        out_shape=jax.ShapeDtypeStruct((M, Fd), BF16),
        scratch_shapes=[pltpu.VMEM((K, tn), BF16), pltpu.VMEM((K, tn), BF16), pltpu.VMEM((8, tn), F32)],
        compiler_params=_cparams("parallel", "arbitrary"),
        name="ffn_up_glu",
    )(x, w_up, w_up, conv_w.astype(F32))


def _rope_tables(S):
    half = HEAD_DIM // 2
    inv_freq = ROPE_THETA ** (-jnp.arange(half, dtype=F32) / half)
    ang = jnp.arange(S, dtype=F32)[:, None] * inv_freq[None, :]
    cos, sin = jnp.cos(ang), jnp.sin(ang)
    return jnp.concatenate([cos, cos], axis=-1), jnp.concatenate([-sin, sin], axis=-1)


def _hgrn_sb_mixer(hb, B, S, w_in_all, lb_raw, norm_w, w_out, e):
    width = w_out.shape[0]
    a_heads = width // (2 * HEAD_DIM)
    b_heads = a_heads
    proj = matmul_f32w(hb, w_in_all, e).reshape(B, S, -1)
    o_a = hgrn2(proj, lb_raw, norm_w, a_heads, e)
    o_b = stick_breaking(proj, 4 * a_heads, 4 * a_heads + b_heads, 4 * a_heads + 2 * b_heads, b_heads)
    return matmul_pair(o_a.reshape(B * S, -1), o_b.reshape(B * S, -1), w_out.astype(BF16))


def _nsa_mixer(hb, B, S, w_in_all, o, cmp_pos, cmp_w1, cmp_w2, w_out, cos, sin):
    G = NSA_KV_HEADS
    q_w = w_out.shape[0]
    n_heads = q_w // HEAD_DIM
    rep = n_heads // G
    kv_w = G * HEAD_DIM
    main_w = q_w + 6 * kv_w
    proj = matmul_f32w(hb, w_in_all, o, n_cols=main_w).reshape(B, S, main_w)
    gl = matmul_f32w(hb, w_in_all, o, col0=main_w, n_cols=LANES, tn=LANES)

    log2_scale = HEAD_DIM ** -0.5 * LOG2E
    ks_rot = rope(proj, q_w + 2 * kv_w, G, cos, sin)
    kw_rot = rope(proj, q_w + 4 * kv_w, G, cos, sin)

    n16 = S // CMP_STRIDE
    kvc_in = proj[:, :, q_w:q_w + 2 * kv_w].reshape(B, S, 2, G, HEAD_DIM)
    x16 = kvc_in.transpose(2, 0, 3, 1, 4).reshape(2, B, G, n16, CMP_STRIDE * HEAD_DIM)
    pe = jnp.broadcast_to(cmp_pos.reshape(2, 1, CMP_LEN * HEAD_DIM), (2, 8, CMP_LEN * HEAD_DIM)).astype(BF16)
    w1 = cmp_w1.reshape(2, CMP_LEN * HEAD_DIM, HEAD_DIM).astype(BF16)
    kvc = compress(x16, pe, w1, cmp_w2.astype(BF16), log2_scale)

    o_c, imp = cmp_attention(proj, kvc[0], kvc[1], rep)
    sel = select_blocks(imp)
    col = lambda off: (q_w + off * kv_w) // HEAD_DIM
    o_s = gqa_attention(proj, cos, sin, log2_scale, ks_rot, proj, col(3), rep, "sel", sel=sel, tq=256, tk=512)
    o_w = gqa_attention(proj, cos, sin, log2_scale, kw_rot, proj, col(5), rep, "win", tq=256, tk=256)
    o = nsa_gate(gl, o_c.reshape(B * S, q_w), o_s.reshape(B * S, q_w), o_w.reshape(B * S, q_w), n_heads)
    return matmul(o, w_out.astype(BF16))


def kernel(x, mem, ab_w_in, hgrn_lb, hgrn_norm_w, ab_w_out, nsa_w_in, nsa_cmp_pos, nsa_cmp_w1,
           nsa_cmp_w2, nsa_w_out, xa_w_q, xa_w_kv, xa_w_o, ffn_w_up, ffn_conv, ffn_w_down, ln_g, ln_b):
    B, S, D = x.shape
    depth = ln_g.shape[0]
    alpha = (2 * depth) ** 0.25
    n_mem = mem.shape[1]
    cos, sin = _rope_tables(S)
    h = x.reshape(B * S, D).astype(F32)
    hb = h.astype(BF16)
    memb = mem.reshape(B * n_mem, D).astype(BF16)
    for layer in range(depth):
        if layer % 2 == 0:
            e = layer // 2
            mix = _hgrn_sb_mixer(hb, B, S, ab_w_in, hgrn_lb, hgrn_norm_w[e], ab_w_out[e], e)
        else:
            o = layer // 2
            mix = _nsa_mixer(hb, B, S, nsa_w_in, o, nsa_cmp_pos[o], nsa_cmp_w1[o], nsa_cmp_w2[o],
                             nsa_w_out[o], cos, sin)
        h, hb = add_layer_norm(h, mix, ln_g[layer, 0], ln_b[layer, 0], alpha)

        xkv = matmul(memb, xa_w_kv[layer].astype(BF16)).reshape(B, n_mem, -1)
        h3, hb3 = mem_attention_block(h.reshape(B, S, D), hb.reshape(B, S, D), xkv, xa_w_q[layer].astype(BF16),
                                      xa_w_o[layer].astype(BF16), ln_g[layer, 1], ln_b[layer, 1], XA_HEADS, alpha)
        h, hb = h3.reshape(B * S, D), hb3.reshape(B * S, D)

        gated = ffn_up_glu(hb, ffn_w_up, layer, ffn_conv[layer], S)
        h, hb = add_layer_norm(h, matmul(gated, ffn_w_down[layer].astype(BF16), tm=512, tk=gated.shape[1]),
                               ln_g[layer, 2], ln_b[layer, 2], alpha)
    return h.reshape(B, S, D).astype(x.dtype)
```

```python
import functools

import jax
import jax.numpy as jnp
from jax import lax
from jax.experimental import pallas as pl
from jax.experimental.pallas import tpu as pltpu

F32 = jnp.float32
BF16 = jnp.bfloat16

HEAD_DIM = 128
LANES = 128
HGRN_SUB = 16
HGRN_SAFE_LOG_DECAY = -60.0
NSA_KV_HEADS = 4
CMP_LEN = 32
CMP_STRIDE = 16
SLC_LEN = 64
SLC_TOP = 16
WINDOW = 512
XA_HEADS = 4
CONV_W = 3
ROPE_THETA = 10000.0
LN_EPS = 1e-5
RMS_EPS = 1e-6
NEG_INF = -1e30
FORCE_SCORE = 1e9
EXP_ZERO_BELOW = -104.0
SOFTPLUS_CLAMP = 80.0
LOG2E = 1.4426950408889634
VMEM_LIMIT = 52 * 1024 * 1024

_NT = (((1,), (1,)), ((), ()))
_TN = (((0,), (0,)), ((), ()))


def _cparams(*sem):
    return pltpu.CompilerParams(dimension_semantics=sem, vmem_limit_bytes=VMEM_LIMIT)


def _split_dot(a, b01):
    hi = a.astype(BF16)
    lo = (a - hi.astype(F32)).astype(BF16)
    return (jnp.dot(hi, b01, preferred_element_type=F32)
            + jnp.dot(lo, b01, preferred_element_type=F32))


def _mm_body(x_ref, w_ref, o_ref, *scratch, nk):
    prod = jnp.dot(x_ref[...], w_ref[...], preferred_element_type=F32)
    if nk == 1:
        o_ref[...] = prod.astype(o_ref.dtype)
        return
    acc_ref, = scratch
    k = pl.program_id(2)

    @pl.when(k == 0)
    def _():
        acc_ref[...] = prod

    @pl.when(k > 0)
    def _():
        acc_ref[...] += prod

    @pl.when(k == nk - 1)
    def _():
        o_ref[...] = acc_ref[...].astype(o_ref.dtype)


def _pick(n, pref):
    if n <= pref:
        return n
    t = pref
    while t >= LANES:
        if n % t == 0:
            return t
        t -= LANES
    return n


def matmul(x, w, out_dtype=BF16, tm=1024, tn=512, tk=4096):
    M, K = x.shape
    N = w.shape[1]
    tm, tn, tk = _pick(M, tm), _pick(N, tn), _pick(K, tk)
    nk = K // tk
    scratch = [] if nk == 1 else [pltpu.VMEM((tm, tn), F32)]
    return pl.pallas_call(
        functools.partial(_mm_body, nk=nk),
        grid=(M // tm, N // tn, nk),
        in_specs=[pl.BlockSpec((tm, tk), lambda i, j, k: (i, k)),
                  pl.BlockSpec((tk, tn), lambda i, j, k: (k, j))],
        out_specs=pl.BlockSpec((tm, tn), lambda i, j, k: (i, j)),
        out_shape=jax.ShapeDtypeStruct((M, N), out_dtype),
        scratch_shapes=scratch,
        compiler_params=_cparams("parallel", "parallel", "arbitrary"),
        name="matmul",
    )(x, w)


def _mm_pair_body(xa_ref, xb_ref, wa_ref, wb_ref, o_ref):
    acc = jnp.dot(xa_ref[...], wa_ref[...], preferred_element_type=F32)
    acc = acc + jnp.dot(xb_ref[...], wb_ref[...], preferred_element_type=F32)
    o_ref[...] = acc.astype(o_ref.dtype)


def matmul_pair(xa, xb, w, out_dtype=BF16, tm=1024, tn=512):
    M, Kh = xa.shape
    assert xb.shape == xa.shape and w.shape[0] == 2 * Kh
    N = w.shape[1]
    tm, tn = _pick(M, tm), _pick(N, tn)
    xs = pl.BlockSpec((tm, Kh), lambda i, j: (i, 0))
    return pl.pallas_call(
        _mm_pair_body,
        grid=(M // tm, N // tn),
        in_specs=[xs, xs, pl.BlockSpec((Kh, tn), lambda i, j: (0, j)), pl.BlockSpec((Kh, tn), lambda i, j: (1, j))],
        out_specs=pl.BlockSpec((tm, tn), lambda i, j: (i, j)),
        out_shape=jax.ShapeDtypeStruct((M, N), out_dtype),
        compiler_params=_cparams("parallel", "parallel"),
        name="matmul_pair",
    )(xa, xb, w, w)


def _mm_f32w_body(x_ref, w_ref, o_ref, wb_ref):
    @pl.when(pl.program_id(1) == 0)
    def _():
        wb_ref[...] = w_ref[0].astype(BF16)

    o_ref[...] = jnp.dot(x_ref[...], wb_ref[...], preferred_element_type=F32).astype(o_ref.dtype)


def matmul_f32w(x, w, layer, out_dtype=BF16, tm=1024, tn=512, col0=0, n_cols=None):
    M, K = x.shape
    N = w.shape[2] if n_cols is None else n_cols
    tm, tn = _pick(M, tm), _pick(N, tn)
    assert col0 % tn == 0
    cb = col0 // tn
    return pl.pallas_call(
        _mm_f32w_body,
        grid=(N // tn, M // tm),
        in_specs=[pl.BlockSpec((tm, K), lambda j, i: (i, 0)),
                  pl.BlockSpec((1, K, tn), lambda j, i: (layer, 0, j + cb))],
        out_specs=pl.BlockSpec((tm, tn), lambda j, i: (i, j)),
        out_shape=jax.ShapeDtypeStruct((M, N), out_dtype),
        scratch_shapes=[pltpu.VMEM((K, tn), BF16)],
        compiler_params=_cparams("parallel", "arbitrary"),
        name="matmul_f32w",
    )(x, w)


def _layer_norm(y, g, b):
    mu = jnp.mean(y, axis=-1, keepdims=True)
    d = y - mu
    var = jnp.mean(d * d, axis=-1, keepdims=True)
    return d * lax.rsqrt(var + LN_EPS) * g + b


def _add_ln_body(h_ref, m_ref, g_ref, b_ref, o32_ref, o16_ref, *, alpha):
    out = _layer_norm(alpha * h_ref[...] + m_ref[...].astype(F32), g_ref[...], b_ref[...])
    o32_ref[...] = out
    o16_ref[...] = out.astype(BF16)


def add_layer_norm(h, mix, g, b, alpha, tm=256):
    M, D = h.shape
    tm = _pick(M, tm)
    row = pl.BlockSpec((tm, D), lambda i: (i, 0))
    vec = pl.BlockSpec((1, D), lambda i: (0, 0))
    return pl.pallas_call(
        functools.partial(_add_ln_body, alpha=alpha),
        grid=(M // tm,),
        in_specs=[row, row, vec, vec],
        out_specs=[row, row],
        out_shape=[jax.ShapeDtypeStruct((M, D), F32), jax.ShapeDtypeStruct((M, D), BF16)],
        compiler_params=_cparams("parallel"),
        name="add_layer_norm",
    )(h, mix, g.reshape(1, D).astype(F32), b.reshape(1, D).astype(F32))


def _hgrn_body(q_ref, f_ref, i_ref, g_ref, lb_ref, nw_ref, o_ref, st_ref, *, ts, layer_idx, nh):
    C = HGRN_SUB
    nsub = ts // C
    heads = [slice(h * HEAD_DIM, (h + 1) * HEAD_DIM) for h in range(nh)]

    @pl.when(pl.program_id(2) == 0)
    def _():
        st_ref[...] = jnp.zeros_like(st_ref)

    lbr = lb_ref[...]
    ex = jnp.exp(lbr - jnp.max(lbr, axis=0, keepdims=True))
    sm = ex / jnp.sum(ex, axis=0, keepdims=True)
    lb = jnp.sum(sm[:layer_idx + 1], axis=0, keepdims=True)

    q = q_ref[0].astype(F32)
    z = f_ref[0].astype(F32)
    v = i_ref[0]
    e = jnp.exp(-jnp.abs(z))
    r = 1.0 / (1.0 + e)
    pos = z >= 0
    sig = jnp.where(pos, r, e * r)
    nsig = jnp.where(pos, e * r, r)
    logf = jnp.log(lb + (1.0 - lb) * sig)
    k = (1.0 - lb) * nsig

    hi = logf.astype(BF16)
    lo = (logf - hi.astype(F32)).astype(BF16)
    row = lax.broadcasted_iota(jnp.int32, (ts, ts), 0)
    col = lax.broadcasted_iota(jnp.int32, (ts, ts), 1)

    def cumdot(m01):
        return jnp.dot(m01, hi, preferred_element_type=F32) + jnp.dot(m01, lo, preferred_element_type=F32)

    def whole_tile(sts):
        b = cumdot(jnp.where(row >= col, 1.0, 0.0).astype(BF16))
        bl = b[ts - 1:ts]
        qd = (q * jnp.exp(b)).astype(BF16)
        kinv = (k * jnp.exp(-b)).astype(BF16)
        kd = (k * jnp.exp(bl - b)).astype(BF16)
        dec = jnp.exp(bl)
        outs, new = [], []
        for h, sl in enumerate(heads):
            dmat = lax.dot_general(qd[:, sl], kinv[:, sl], _NT, preferred_element_type=F32)
            dmat = jnp.where(row >= col, dmat, 0.0)
            o = jnp.dot(dmat.astype(BF16), v[:, sl], preferred_element_type=F32)
            outs.append(o + lax.dot_general(qd[:, sl], sts[h].astype(BF16), _NT, preferred_element_type=F32))
            new.append(sts[h] * dec[:, sl] + lax.dot_general(v[:, sl], kd[:, sl], _TN, preferred_element_type=F32))
        return tuple(outs), tuple(new)

    def sub_chunks(sts):
        same = (row // C) == (col // C)
        b = cumdot(jnp.where(same & (row >= col), 1.0, 0.0).astype(BF16))
        bl = cumdot(jnp.where(same, 1.0, 0.0).astype(BF16))
        qd = (q * jnp.exp(b)).astype(BF16)
        kd = (k * jnp.exp(bl - b)).astype(BF16)
        dec = jnp.exp(bl)
        tri = (lax.broadcasted_iota(jnp.int32, (C, C, HEAD_DIM), 0)
               >= lax.broadcasted_iota(jnp.int32, (C, C, HEAD_DIM), 1))
        outs, new = [], []
        for h, hs in enumerate(heads):
            st = sts[h]
            parts = []
            for n in range(nsub):
                sl = slice(n * C, (n + 1) * C)
                bn, qn, kn = b[sl, hs], q[sl, hs], k[sl, hs]
                diff = bn[:, None, :] - bn[None, :, :]
                ee = jnp.exp(jnp.where(tri, diff, NEG_INF))
                dmat = jnp.sum(qn[:, None, :] * (kn[None, :, :] * ee), axis=-1)
                o_n = jnp.dot(dmat.astype(BF16), v[sl, hs], preferred_element_type=F32)
                o_n = o_n + lax.dot_general(qd[sl, hs], st.astype(BF16), _NT, preferred_element_type=F32)
                upd = lax.dot_general(v[sl, hs], kd[sl, hs], _TN, preferred_element_type=F32)
                st = st * dec[n * C:n * C + 1, hs] + upd
                parts.append(o_n)
            outs.append(jnp.concatenate(parts, axis=0))
            new.append(st)
        return tuple(outs), tuple(new)

    tile_decay = jnp.min(jnp.sum(logf, axis=0, keepdims=True))
    outs, sts = lax.cond(tile_decay > HGRN_SAFE_LOG_DECAY, whole_tile, sub_chunks,
                         tuple(st_ref[h] for h in range(nh)))
    gt = g_ref[0].astype(F32)
    gate = gt / (1.0 + jnp.exp(-gt))
    for h, sl in enumerate(heads):
        st_ref[h] = sts[h]
        o = outs[h]
        o = o * lax.rsqrt(jnp.mean(o * o, axis=-1, keepdims=True) + RMS_EPS) * nw_ref[...] * gate[:, sl]
        o_ref[0, :, sl] = o.astype(o_ref.dtype)


def hgrn2(proj, lb_raw, norm_w, n_heads, layer_idx, ts=128, nh=16):
    B, S, _ = proj.shape
    ts = _pick(S, ts)
    nh = min(nh, n_heads)
    assert n_heads % nh == 0
    H = n_heads // nh
    L = lb_raw.shape[0]
    w = nh * HEAD_DIM

    def col(off):
        return pl.BlockSpec((1, ts, w), lambda b, h, s: (b, s, off * H + h))

    return pl.pallas_call(
        functools.partial(_hgrn_body, ts=ts, layer_idx=layer_idx, nh=nh),
        grid=(B, H, S // ts),
        in_specs=[col(0), col(1), col(2), col(3),
                  pl.BlockSpec((L, w), lambda b, h, s: (0, h)),
                  pl.BlockSpec((1, HEAD_DIM), lambda b, h, s: (0, 0))],
        out_specs=pl.BlockSpec((1, ts, w), lambda b, h, s: (b, s, h)),
        out_shape=jax.ShapeDtypeStruct((B, S, n_heads * HEAD_DIM), BF16),
        scratch_shapes=[pltpu.VMEM((nh, HEAD_DIM, HEAD_DIM), F32)],
        compiler_params=_cparams("parallel", "parallel", "arbitrary"),
        name="hgrn2",
    )(proj, proj, proj, proj, lb_raw.astype(F32), norm_w.reshape(1, HEAD_DIM).astype(F32))


def _sb_block(q, k, v, carry, after01, scale, mask):
    z = lax.dot_general(q, k, _NT, preferred_element_type=F32) * scale
    sp = jnp.maximum(z, jnp.log(1.0 + jnp.exp(jnp.minimum(z, SOFTPLUS_CLAMP))))
    spm = sp if mask is None else jnp.where(mask, sp, 0.0)
    sub = after01.shape[0]
    parts, tail = [], 0.0
    for c in reversed(range(spm.shape[1] // sub)):
        blk = spm[:, c * sub:(c + 1) * sub]
        part = _split_dot(blk, after01) + tail
        parts.insert(0, part)
        tail = part[:, 0:1] + blk[:, 0:1]
    rev = parts[0] if len(parts) == 1 else jnp.concatenate(parts, axis=1)
    w = jnp.exp(z - sp - rev - carry)
    if mask is not None:
        w = jnp.where(mask, w, 0.0)
    contrib = jnp.dot(w.astype(BF16), v, preferred_element_type=F32)
    return contrib, carry + tail


def _sb_body(q_ref, k_ref, v_ref, o_ref, *, tq, first, scale, nh):
    qi = pl.program_id(2)
    heads = [slice(h * HEAD_DIM, (h + 1) * HEAD_DIM) for h in range(nh)]
    qs = [q_ref[0, :, sl] for sl in heads]

    def after01(width):
        return jnp.where(lax.broadcasted_iota(jnp.int32, (width, width), 0)
                         > lax.broadcasted_iota(jnp.int32, (width, width), 1), 1.0, 0.0).astype(BF16)

    def blocks(k0, width, after, carries, mask):
        res = [_sb_block(qs[h], k_ref[0, pl.ds(k0, width), heads[h]], v_ref[0, pl.ds(k0, width), heads[h]],
                         carries[h], after, scale, mask) for h in range(nh)]
        return tuple(r[0] for r in res), tuple(r[1] for r in res)

    k0 = pl.multiple_of(jnp.maximum((qi + 1) * tq - first, 0), tq)
    t = qi * tq + lax.broadcasted_iota(jnp.int32, (tq, first), 0)
    kpos = k0 + lax.broadcasted_iota(jnp.int32, (tq, first), 1)
    after_tq = after01(tq)
    accs, carries = blocks(k0, first, after_tq, (jnp.zeros((tq, 1), F32),) * nh, kpos < t)

    def cond(c):
        j, _, carries = c
        low = functools.reduce(jnp.minimum, [jnp.min(cr) for cr in carries])
        return jnp.logical_and(j >= 0, low < -EXP_ZERO_BELOW)

    def body(c):
        j, accs, carries = c
        contribs, carries = blocks(pl.multiple_of(j * tq, tq), tq, after_tq, carries, None)
        return j - 1, tuple(a + cb for a, cb in zip(accs, contribs)), carries

    _, accs, _ = lax.while_loop(cond, body, (k0 // tq - 1, accs, carries))
    for h in range(nh):
        o_ref[0, :, heads[h]] = accs[h].astype(o_ref.dtype)


def stick_breaking(proj, col_q, col_k, col_v, n_heads, tq=256, nh=4):
    B, S, _ = proj.shape
    tq = _pick(S, tq)
    first = min(2 * tq, S)
    assert n_heads % nh == 0 and col_q % nh == 0 and col_k % nh == 0 and col_v % nh == 0
    w = nh * HEAD_DIM
    full = lambda off: pl.BlockSpec((1, S, w), lambda b, h, i: (b, 0, off // nh + h))
    return pl.pallas_call(
        functools.partial(_sb_body, tq=tq, first=first, scale=HEAD_DIM ** -0.5, nh=nh),
        grid=(B, n_heads // nh, S // tq),
        in_specs=[pl.BlockSpec((1, tq, w), lambda b, h, i: (b, i, col_q // nh + h)),
                  full(col_k), full(col_v)],
        out_specs=pl.BlockSpec((1, tq, w), lambda b, h, i: (b, i, h)),
        out_shape=jax.ShapeDtypeStruct((B, S, n_heads * HEAD_DIM), BF16),
        compiler_params=_cparams("parallel", "parallel", "arbitrary"),
        name="stick_breaking",
    )(proj, proj, proj)


def _rope_body(x_ref, cos_ref, sin_ref, o_ref, *, n_heads):
    cos = cos_ref[...]
    sin = sin_ref[...]
    for h in range(n_heads):
        sl = slice(h * HEAD_DIM, (h + 1) * HEAD_DIM)
        t = x_ref[0, :, sl].astype(F32)
        o_ref[0, :, sl] = (t * cos + pltpu.roll(t, HEAD_DIM // 2, 1) * sin).astype(o_ref.dtype)


def rope(x, col0, n_heads, cos, sin, ts=256):
    B, S, _ = x.shape
    ts = _pick(S, ts)
    w = n_heads * HEAD_DIM
    assert col0 % w == 0
    cb = col0 // w
    tab = pl.BlockSpec((ts, HEAD_DIM), lambda b, s: (s, 0))
    return pl.pallas_call(
        functools.partial(_rope_body, n_heads=n_heads),
        grid=(B, S // ts),
        in_specs=[pl.BlockSpec((1, ts, w), lambda b, s: (b, s, cb)), tab, tab],
        out_specs=pl.BlockSpec((1, ts, w), lambda b, s: (b, s, 0)),
        out_shape=jax.ShapeDtypeStruct((B, S, w), BF16),
        compiler_params=_cparams("parallel", "parallel"),
        name="rope",
    )(x, cos, sin)


def _gelu(x):
    return 0.5 * x * (1.0 + lax.erf(x * (2.0 ** -0.5)))


def _compress_body(x_ref, pe_ref, w1_ref, w2_ref, o_ref, *, batch, k_mult):
    x = x_ref[0, 0]
    half = x.shape[1]
    n16 = x.shape[0]
    y1 = jnp.dot(x, w1_ref[0, :half], preferred_element_type=F32)
    y2 = jnp.dot(x, w1_ref[0, half:], preferred_element_type=F32)
    bias = jnp.dot(pe_ref[0], w1_ref[0], preferred_element_type=F32)[0:1]
    hid = _gelu(y1 + pltpu.roll(y2, n16 - 1, 0) + bias)
    out = jnp.dot(hid.astype(BF16), w2_ref[0], preferred_element_type=F32)
    mult = jnp.where(pl.program_id(0) < batch, k_mult, 1.0)
    o_ref[0, 0] = (out * mult).astype(o_ref.dtype)


def compress(x16, pe, w1, w2, k_mult):
    two, B, G, n16, wide = x16.shape
    x16 = x16.reshape(two * B, G, n16, wide)
    out = pl.pallas_call(
        functools.partial(_compress_body, batch=B, k_mult=k_mult),
        grid=(two * B, G),
        in_specs=[pl.BlockSpec((1, 1, n16, wide), lambda i, g: (i, g, 0, 0)),
                  pl.BlockSpec((1, 8, 2 * wide), lambda i, g: (i // B, 0, 0)),
                  pl.BlockSpec((1, 2 * wide, HEAD_DIM), lambda i, g: (i // B, 0, 0)),
                  pl.BlockSpec((1, HEAD_DIM, HEAD_DIM), lambda i, g: (i // B, 0, 0))],
        out_specs=pl.BlockSpec((1, 1, n16, HEAD_DIM), lambda i, g: (i, g, 0, 0)),
        out_shape=jax.ShapeDtypeStruct((two * B, G, n16, HEAD_DIM), BF16),
        compiler_params=_cparams("parallel", "parallel"),
        name="nsa_compress",
    )(x16, pe, w1, w2)
    return out.reshape(two, B, G, n16, HEAD_DIM)


def _stack_heads(q_ref, rep):
    return jnp.concatenate([q_ref[0, :, r * HEAD_DIM:(r + 1) * HEAD_DIM] for r in range(rep)], axis=0)


def _cmp_body(q_ref, kc_ref, vc_ref, o_ref, imp_ref, *, tq, rep):
    q0 = pl.program_id(2) * tq
    n16 = kc_ref.shape[2]
    cols = [slice(r * tq, (r + 1) * tq) for r in range(rep)]
    q2 = _stack_heads(q_ref, rep)

    def attend(nb):
        st = lax.dot_general(kc_ref[0, 0, :nb], q2, _NT, preferred_element_type=F32)
        n = lax.broadcasted_iota(jnp.int32, (nb, tq), 0)
        t = q0 + lax.broadcasted_iota(jnp.int32, (nb, tq), 1)
        bias = jnp.where(n * CMP_STRIDE + (CMP_LEN - 1) <= t, 0.0, NEG_INF)
        ps = []
        psum = jnp.zeros((nb, tq), F32)
        for r in range(rep):
            s = st[:, cols[r]] + bias
            m = jnp.maximum(jnp.max(s, axis=0, keepdims=True), 0.1 * NEG_INF)
            p = jnp.exp2(s - m)
            den = jnp.sum(p, axis=0, keepdims=True)
            pn = p * (1.0 / jnp.where(den > 0.0, den, 1.0))
            ps.append(pn.astype(BF16))
            psum = psum + pn
        ot = lax.dot_general(vc_ref[0, 0, :nb], jnp.concatenate(ps, axis=1), _TN,
                             preferred_element_type=F32)
        for r in range(rep):
            o_ref[0, :, r * HEAD_DIM:(r + 1) * HEAD_DIM] = ot[:, cols[r]].T.astype(o_ref.dtype)

        cj = lax.broadcasted_iota(jnp.int32, (LANES, nb), 0) * SLC_LEN
        cn = lax.broadcasted_iota(jnp.int32, (LANES, nb), 1) * CMP_STRIDE
        ov01 = jnp.where((cn < cj + SLC_LEN) & (cn + CMP_LEN > cj), 1.0, 0.0).astype(BF16)
        hi = psum.astype(BF16)
        lo = (psum - hi.astype(F32)).astype(BF16)
        imp_t = jnp.dot(ov01, hi, preferred_element_type=F32) + jnp.dot(ov01, lo, preferred_element_type=F32)
        imp_ref[0, 0] = imp_t.T

    n_chunks = max(n16 // LANES, 1)
    chunk = n16 // n_chunks
    last_valid = (q0 + tq - CMP_LEN) // CMP_STRIDE
    need = jnp.clip(last_valid // chunk + 1, 1, n_chunks)
    for c in range(1, n_chunks + 1):
        pl.when(need == c)(functools.partial(attend, c * chunk))


def cmp_attention(q, kc, vc, rep, tq=256):
    B, S, _ = q.shape
    G, n16 = kc.shape[1], kc.shape[2]
    tq = _pick(S, tq)
    w = rep * HEAD_DIM
    kv = pl.BlockSpec((1, 1, n16, HEAD_DIM), lambda b, g, i: (b, g, 0, 0))
    return pl.pallas_call(
        functools.partial(_cmp_body, tq=tq, rep=rep),
        grid=(B, G, S // tq),
        in_specs=[pl.BlockSpec((1, tq, w), lambda b, g, i: (b, i, g)), kv, kv],
        out_specs=[pl.BlockSpec((1, tq, w), lambda b, g, i: (b, i, g)),
                   pl.BlockSpec((1, 1, tq, LANES), lambda b, g, i: (b, g, i, 0))],
        out_shape=[jax.ShapeDtypeStruct((B, S, G * w), BF16),
                   jax.ShapeDtypeStruct((B, G, S, LANES), F32)],
        compiler_params=_cparams("parallel", "parallel", "parallel"),
        name="nsa_cmp",
    )(q, kc, vc)


def _topk_body(imp_ref, sel_ref, *, tq, n_slc):
    q0 = pl.program_id(2) * tq
    imp = imp_ref[0, 0]
    tt = q0 + lax.broadcasted_iota(jnp.int32, (tq, LANES), 0)
    j = lax.broadcasted_iota(jnp.int32, (tq, LANES), 1)
    cur = tt // SLC_LEN
    forced = (j == 0) | (j == cur) | (j == cur - 1)
    allowed = j * SLC_LEN <= tt
    score = jnp.where(forced, FORCE_SCORE, jnp.where(allowed, imp, -1.0))
    score = jnp.where(j < n_slc, score, -jnp.inf)
    jf = j.astype(F32)
    sel = jnp.zeros((tq, LANES), F32)
    for _ in range(min(SLC_TOP, n_slc)):
        m = jnp.max(score, axis=-1, keepdims=True)
        first = jnp.min(jnp.where(score == m, jf, float(LANES)), axis=-1, keepdims=True)
        pick = jf == first
        sel = jnp.where(pick, 1.0, sel)
        score = jnp.where(pick, -jnp.inf, score)
    sel_ref[0, 0] = sel.astype(sel_ref.dtype)


def select_blocks(imp, tq=1024):
    B, G, S, _ = imp.shape
    n_slc = S // SLC_LEN
    assert n_slc <= LANES
    tq = _pick(S, tq)
    spec = pl.BlockSpec((1, 1, tq, LANES), lambda b, g, i: (b, g, i, 0))
    return pl.pallas_call(
        functools.partial(_topk_body, tq=tq, n_slc=n_slc),
        grid=(B, G, S // tq),
        in_specs=[spec],
        out_specs=spec,
        out_shape=jax.ShapeDtypeStruct((B, G, S, LANES), BF16),
        compiler_params=_cparams("parallel", "parallel", "parallel"),
        name="nsa_topk",
    )(imp)


def _gqa_body(*refs, tq, tk, rep, mode, mult):
    q0 = pl.program_id(2) * tq
    q_ref, cos_ref, sin_ref, k_ref, v_ref = refs[:5]
    cos = cos_ref[...] * mult
    sin = sin_ref[...] * mult
    rot = []
    for r in range(rep):
        x = q_ref[0, :, r * HEAD_DIM:(r + 1) * HEAD_DIM].astype(F32)
        rot.append((x * cos + pltpu.roll(x, HEAD_DIM // 2, 1) * sin).astype(BF16))
    q2 = jnp.concatenate(rot, axis=0)
    if mode == "sel":
        sel_ref, o_ref, s_ref, acc_ref = refs[5:]
        key_blk = lax.broadcasted_iota(jnp.int32, (tk, LANES), 0) // SLC_LEN
        lane_blk = lax.broadcasted_iota(jnp.int32, (tk, LANES), 1)
        unpicked = ((sel_ref[0, 0].astype(F32) - 1.0) * (-NEG_INF)).astype(BF16)
        q2 = jnp.concatenate([q2, jnp.concatenate([unpicked] * rep, axis=0)], axis=1)
    else:
        o_ref, s_ref, acc_ref = refs[5:]
    kpos = lax.broadcasted_iota(jnp.int32, (tk, tq), 0)
    t = q0 + lax.broadcasted_iota(jnp.int32, (tk, tq), 1)
    cols = [slice(r * tq, (r + 1) * tq) for r in range(rep)]

    def put_scores(slot, kj):
        k0 = pl.multiple_of(kj * tk, tk)
        keys = k_ref[0, pl.ds(k0, tk), :]
        if mode == "sel":
            onehot = jnp.where(key_blk + kj * (tk // SLC_LEN) == lane_blk, 1.0, 0.0).astype(BF16)
            keys = jnp.concatenate([keys, onehot], axis=1)
        s_ref[slot] = lax.dot_general(keys, q2, _NT, preferred_element_type=F32)

    def tile(slot, kj, m, l, diagonal):
        k0 = pl.multiple_of(kj * tk, tk)
        kp = k0 + kpos
        if mode == "sel":
            bias = jnp.where(kp <= t, 0.0, NEG_INF) if diagonal else None
        else:
            ok = kp > t - WINDOW
            if diagonal:
                ok = ok & (kp <= t)
            bias = jnp.where(ok, 0.0, NEG_INF)
        ps, m_new, l_new, scale = [], [], [], []
        for r in range(rep):
            s = s_ref[slot, :, cols[r]]
            if bias is None:
                mr = jnp.maximum(m[r], jnp.max(s, axis=0, keepdims=True))
                p = jnp.exp2(s - mr)
            else:
                mr = jnp.maximum(m[r], jnp.max(s + bias, axis=0, keepdims=True))
                p = jnp.exp2((s - mr) + bias)
            a = jnp.exp2(m[r] - mr)
            ps.append(p.astype(BF16))
            m_new.append(mr)
            l_new.append(a * l[r] + jnp.sum(p, axis=0, keepdims=True))
            scale.append(a)
        pt = jnp.concatenate(ps, axis=1)
        pv = lax.dot_general(v_ref[0, pl.ds(k0, tk), :], pt, _TN, preferred_element_type=F32)
        acc_ref[...] = jnp.concatenate(scale, axis=1) * acc_ref[...] + pv
        return tuple(m_new), tuple(l_new)

    kd = q0 // tk
    lo = 0 if mode == "sel" else jnp.maximum(q0 - (WINDOW - 1), 0) // tk
    put_scores(0, kd)
    put_scores(1, lo)
    acc_ref[...] = jnp.zeros_like(acc_ref)
    stats = tile(0, kd, (jnp.full((1, tq), NEG_INF, F32),) * rep, (jnp.zeros((1, tq), F32),) * rep, True)

    def pair(i, stats):
        kj = lo + 2 * i
        put_scores(0, kj + 1)
        stats = tile(1, kj, *stats, False)
        put_scores(1, kj + 2)
        return tile(0, kj + 1, *stats, False)

    n_off = kd - lo
    stats = lax.fori_loop(0, n_off // 2, pair, stats)
    m, l = lax.cond(n_off % 2 == 1, lambda st: tile(1, kd - 1, *st, False), lambda st: st, stats)
    for r in range(rep):
        out = acc_ref[:, cols[r]] * (1.0 / l[r])
        o_ref[0, :, r * HEAD_DIM:(r + 1) * HEAD_DIM] = out.T.astype(o_ref.dtype)


def gqa_attention(q, cos, sin, mult, k, v, v_col0, rep, mode, sel=None, tq=128, tk=512):
    B, S, _ = q.shape
    G = k.shape[2] // HEAD_DIM
    tq, tk = _pick(S, tq), _pick(S, tk)
    assert tk % tq == 0
    w = rep * HEAD_DIM
    tab = pl.BlockSpec((tq, HEAD_DIM), lambda b, g, i: (i, 0))
    in_specs = [pl.BlockSpec((1, tq, w), lambda b, g, i: (b, i, g)), tab, tab,
                pl.BlockSpec((1, S, HEAD_DIM), lambda b, g, i: (b, 0, g)),
                pl.BlockSpec((1, S, HEAD_DIM), lambda b, g, i: (b, 0, v_col0 + g))]
    args = [q, cos, sin, k, v]
    scratch = [pltpu.VMEM((2, tk, rep * tq), F32), pltpu.VMEM((HEAD_DIM, rep * tq), F32)]
    if mode == "sel":
        assert S // SLC_LEN <= LANES and tk % SLC_LEN == 0
        in_specs += [pl.BlockSpec((1, 1, tq, LANES), lambda b, g, i: (b, g, i, 0))]
        args += [sel]
    return pl.pallas_call(
        functools.partial(_gqa_body, tq=tq, tk=tk, rep=rep, mode=mode, mult=mult),
        grid=(B, G, S // tq),
        in_specs=in_specs,
        out_specs=pl.BlockSpec((1, tq, w), lambda b, g, i: (b, i, g)),
        out_shape=jax.ShapeDtypeStruct((B, S, G * w), BF16),
        scratch_shapes=scratch,
        compiler_params=_cparams("parallel", "parallel", "arbitrary"),
        name="nsa_" + mode,
    )(*args)


def _nsa_gate_body(gl_ref, oc_ref, os_ref, ow_ref, o_ref, *, n_heads):
    ng = 3 * n_heads
    gl = gl_ref[:, :ng].astype(F32)
    gate = 1.0 / (1.0 + jnp.exp(-gl))
    src = lax.broadcasted_iota(jnp.int32, (ng, n_heads * HEAD_DIM), 0)
    head = lax.broadcasted_iota(jnp.int32, (ng, n_heads * HEAD_DIM), 1) // HEAD_DIM
    out = None
    for c, ref in enumerate((oc_ref, os_ref, ow_ref)):
        spread01 = jnp.where(src == head * 3 + c, 1.0, 0.0).astype(BF16)
        term = _split_dot(gate, spread01) * ref[...].astype(F32)
        out = term if out is None else out + term
    o_ref[...] = out.astype(o_ref.dtype)


def nsa_gate(gl, oc, os_, ow, n_heads, tm=256):
    M, W = oc.shape
    tm = _pick(M, tm)
    row = pl.BlockSpec((tm, W), lambda i: (i, 0))
    return pl.pallas_call(
        functools.partial(_nsa_gate_body, n_heads=n_heads),
        grid=(M // tm,),
        in_specs=[pl.BlockSpec((tm, gl.shape[1]), lambda i: (i, 0)), row, row, row],
        out_specs=row,
        out_shape=jax.ShapeDtypeStruct((M, W), BF16),
        compiler_params=_cparams("parallel"),
        name="nsa_gate",
    )(gl, oc, os_, ow)


def _xattn_body(h_ref, hb_ref, kv_ref, wq_ref, wo_ref, g_ref, b_ref, o32_ref, o16_ref, *, n_heads, scale, alpha):
    w = n_heads * HEAD_DIM
    q = jnp.dot(hb_ref[0], wq_ref[...], preferred_element_type=F32).astype(BF16)
    outs = []
    for hd in range(n_heads):
        sl = slice(hd * HEAD_DIM, (hd + 1) * HEAD_DIM)
        k = kv_ref[0, :, sl]
        v = kv_ref[0, :, w + hd * HEAD_DIM:w + (hd + 1) * HEAD_DIM]
        s = lax.dot_general(q[:, sl], k, _NT, preferred_element_type=F32) * scale
        p = jnp.exp(s - jnp.max(s, axis=-1, keepdims=True))
        p = p / jnp.sum(p, axis=-1, keepdims=True)
        outs.append(jnp.dot(p.astype(BF16), v, preferred_element_type=F32).astype(BF16))
    mix = jnp.dot(jnp.concatenate(outs, axis=1), wo_ref[...], preferred_element_type=F32)
    out = _layer_norm(alpha * h_ref[0] + mix, g_ref[...], b_ref[...])
    o32_ref[0] = out
    o16_ref[0] = out.astype(BF16)


def mem_attention_block(h, hb, kv, w_q, w_o, g, b, n_heads, alpha, tq=256):
    B, S, D = h.shape
    n_mem = kv.shape[1]
    w = n_heads * HEAD_DIM
    tq = _pick(S, tq)
    row = pl.BlockSpec((1, tq, D), lambda bb, i: (bb, i, 0))
    vec = pl.BlockSpec((1, D), lambda bb, i: (0, 0))
    return pl.pallas_call(
        functools.partial(_xattn_body, n_heads=n_heads, scale=HEAD_DIM ** -0.5, alpha=alpha),
        grid=(B, S // tq),
        in_specs=[row, row,
                  pl.BlockSpec((1, n_mem, 2 * w), lambda bb, i: (bb, 0, 0)),
                  pl.BlockSpec((D, w), lambda bb, i: (0, 0)),
                  pl.BlockSpec((w, D), lambda bb, i: (0, 0)), vec, vec],
        out_specs=[row, row],
        out_shape=[jax.ShapeDtypeStruct((B, S, D), F32), jax.ShapeDtypeStruct((B, S, D), BF16)],
        compiler_params=_cparams("parallel", "parallel"),
        name="mem_attention_block",
    )(h, hb, kv, w_q, w_o, g.reshape(1, D).astype(F32), b.reshape(1, D).astype(F32))


def _ffn_up_body(x_ref, wa_ref, wu_ref, cw_ref, o_ref, wab_ref, wub_ref, tail_ref, *, tiles_per_seq):
    i = pl.program_id(1)

    @pl.when(i == 0)
    def _():
        wab_ref[...] = wa_ref[0].astype(BF16)
        wub_ref[...] = wu_ref[0].astype(BF16)

    x = x_ref[...]
    a = jnp.dot(x, wab_ref[...], preferred_element_type=F32)
    tm = a.shape[0]
    first = (i % tiles_per_seq) == 0
    prev = jnp.where(first, 0.0, tail_ref[...])
    tail_ref[...] = a[tm - 8:]
    rowi = lax.broadcasted_iota(jnp.int32, a.shape, 0)
    a1 = jnp.where(rowi >= 1, pltpu.roll(a, 1, 0), prev[7:8])
    a2 = jnp.where(rowi >= 2, pltpu.roll(a, 2, 0), jnp.where(rowi == 1, prev[7:8], prev[6:7]))
    cw = cw_ref[...]
    gate = _gelu(cw[2:3] * a + cw[1:2] * a1 + cw[0:1] * a2)
    u = jnp.dot(x, wub_ref[...], preferred_element_type=F32)
    o_ref[...] = (gate * u).astype(o_ref.dtype)


def ffn_up_glu(x, w_up, layer, conv_w, seq_len, tm=1024, tn=256):
    M, K = x.shape
    Fd = w_up.shape[2] // 2
    tm, tn = _pick(seq_len, tm), _pick(Fd, tn)
    nj = Fd // tn
    return pl.pallas_call(
        functools.partial(_ffn_up_body, tiles_per_seq=seq_len // tm),
        grid=(nj, M // tm),
        in_specs=[pl.BlockSpec((tm, K), lambda j, i: (i, 0)),
                  pl.BlockSpec((1, K, tn), lambda j, i: (layer, 0, j)),
                  pl.BlockSpec((1, K, tn), lambda j, i: (layer, 0, j + nj)),
                  pl.BlockSpec((CONV_W, tn), lambda j, i: (0, j))],
        out_specs=pl.BlockSpec((tm, tn), lambda j, i: (i, j)),
        out_shape=jax.ShapeDtypeStruct((M, Fd), BF16),
        scratch_shapes=[pltpu.VMEM((K, tn), BF16), pltpu.VMEM((K, tn), BF16), pltpu.VMEM((8, tn), F32)],
        compiler_params=_cparams("parallel", "arbitrary"),
        name="ffn_up_glu",
    )(x, w_up, w_up, conv_w.astype(F32))


def _rope_tables(S):
    half = HEAD_DIM // 2
    inv_freq = ROPE_THETA ** (-jnp.arange(half, dtype=F32) / half)
    ang = jnp.arange(S, dtype=F32)[:, None] * inv_freq[None, :]
    cos, sin = jnp.cos(ang), jnp.sin(ang)
    return jnp.concatenate([cos, cos], axis=-1), jnp.concatenate([-sin, sin], axis=-1)


def _hgrn_sb_mixer(hb, B, S, w_in_all, lb_raw, norm_w, w_out, e):
    width = w_out.shape[0]
    a_heads = width // (2 * HEAD_DIM)
    b_heads = a_heads
    proj = matmul_f32w(hb, w_in_all, e).reshape(B, S, -1)
    o_a = hgrn2(proj, lb_raw, norm_w, a_heads, e)
    o_b = stick_breaking(proj, 4 * a_heads, 4 * a_heads + b_heads, 4 * a_heads + 2 * b_heads, b_heads)
    return matmul_pair(o_a.reshape(B * S, -1), o_b.reshape(B * S, -1), w_out.astype(BF16))


def _nsa_mixer(hb, B, S, w_in_all, o, cmp_pos, cmp_w1, cmp_w2, w_out, cos, sin):
    G = NSA_KV_HEADS
    q_w = w_out.shape[0]
    n_heads = q_w // HEAD_DIM
    rep = n_heads // G
    kv_w = G * HEAD_DIM
    main_w = q_w + 6 * kv_w
    proj = matmul_f32w(hb, w_in_all, o, n_cols=main_w).reshape(B, S, main_w)
    gl = matmul_f32w(hb, w_in_all, o, col0=main_w, n_cols=LANES, tn=LANES)

    log2_scale = HEAD_DIM ** -0.5 * LOG2E
    ks_rot = rope(proj, q_w + 2 * kv_w, G, cos, sin)
    kw_rot = rope(proj, q_w + 4 * kv_w, G, cos, sin)

    n16 = S // CMP_STRIDE
    kvc_in = proj[:, :, q_w:q_w + 2 * kv_w].reshape(B, S, 2, G, HEAD_DIM)
    x16 = kvc_in.transpose(2, 0, 3, 1, 4).reshape(2, B, G, n16, CMP_STRIDE * HEAD_DIM)
    pe = jnp.broadcast_to(cmp_pos.reshape(2, 1, CMP_LEN * HEAD_DIM), (2, 8, CMP_LEN * HEAD_DIM)).astype(BF16)
    w1 = cmp_w1.reshape(2, CMP_LEN * HEAD_DIM, HEAD_DIM).astype(BF16)
    kvc = compress(x16, pe, w1, cmp_w2.astype(BF16), log2_scale)

    o_c, imp = cmp_attention(proj, kvc[0], kvc[1], rep)
    sel = select_blocks(imp)
    col = lambda off: (q_w + off * kv_w) // HEAD_DIM
    o_s = gqa_attention(proj, cos, sin, log2_scale, ks_rot, proj, col(3), rep, "sel", sel=sel, tq=256, tk=512)
    o_w = gqa_attention(proj, cos, sin, log2_scale, kw_rot, proj, col(5), rep, "win", tq=256, tk=256)
    o = nsa_gate(gl, o_c.reshape(B * S, q_w), o_s.reshape(B * S, q_w), o_w.reshape(B * S, q_w), n_heads)
    return matmul(o, w_out.astype(BF16))


def kernel(x, mem, ab_w_in, hgrn_lb, hgrn_norm_w, ab_w_out, nsa_w_in, nsa_cmp_pos, nsa_cmp_w1,
           nsa_cmp_w2, nsa_w_out, xa_w_q, xa_w_kv, xa_w_o, ffn_w_up, ffn_conv, ffn_w_down, ln_g, ln_b):
    B, S, D = x.shape
    depth = ln_g.shape[0]
    alpha = (2 * depth) ** 0.25
    n_mem = mem.shape[1]
    cos, sin = _rope_tables(S)
    h = x.reshape(B * S, D).astype(F32)
    hb = h.astype(BF16)
    memb = mem.reshape(B * n_mem, D).astype(BF16)
    for layer in range(depth):
        if layer % 2 == 0:
            e = layer // 2
            mix = _hgrn_sb_mixer(hb, B, S, ab_w_in, hgrn_lb, hgrn_norm_w[e], ab_w_out[e], e)
        else:
            o = layer // 2
            mix = _nsa_mixer(hb, B, S, nsa_w_in, o, nsa_cmp_pos[o], nsa_cmp_w1[o], nsa_cmp_w2[o],
                             nsa_w_out[o], cos, sin)
        h, hb = add_layer_norm(h, mix, ln_g[layer, 0], ln_b[layer, 0], alpha)

        xkv = matmul(memb, xa_w_kv[layer].astype(BF16)).reshape(B, n_mem, -1)
        h3, hb3 = mem_attention_block(h.reshape(B, S, D), hb.reshape(B, S, D), xkv, xa_w_q[layer].astype(BF16),
                                      xa_w_o[layer].astype(BF16), ln_g[layer, 1], ln_b[layer, 1], XA_HEADS, alpha)
        h, hb = h3.reshape(B * S, D), hb3.reshape(B * S, D)

        gated = ffn_up_glu(hb, ffn_w_up, layer, ffn_conv[layer], S)
        h, hb = add_layer_norm(h, matmul(gated, ffn_w_down[layer].astype(BF16), tm=512, tk=gated.shape[1]),
                               ln_g[layer, 2], ln_b[layer, 2], alpha)
    return h.reshape(B, S, D).astype(x.dtype)
```

```python
import functools

import jax
import jax.numpy as jnp
from jax import lax
from jax.experimental import pallas as pl
from jax.experimental.pallas import tpu as pltpu

F32 = jnp.float32
BF16 = jnp.bfloat16

HEAD_DIM = 128
LANES = 128
HGRN_SUB = 16
HGRN_SAFE_LOG_DECAY = -60.0
NSA_KV_HEADS = 4
CMP_LEN = 32
CMP_STRIDE = 16
SLC_LEN = 64
SLC_TOP = 16
WINDOW = 512
XA_HEADS = 4
XA_ROW_CHUNK = 128
CONV_W = 3
ROPE_THETA = 10000.0
LN_EPS = 1e-5
RMS_EPS = 1e-6
NEG_INF = -1e30
FORCE_SCORE = 1e9
EXP_ZERO_BELOW = -104.0
SOFTPLUS_CLAMP = 80.0
LOG2E = 1.4426950408889634
VMEM_LIMIT = 52 * 1024 * 1024

_NT = (((1,), (1,)), ((), ()))
_TN = (((0,), (0,)), ((), ()))


def _cparams(*sem):
    return pltpu.CompilerParams(dimension_semantics=sem, vmem_limit_bytes=VMEM_LIMIT)


def _split_dot(a, b01):
    hi = a.astype(BF16)
    lo = (a - hi.astype(F32)).astype(BF16)
    return (jnp.dot(hi, b01, preferred_element_type=F32)
            + jnp.dot(lo, b01, preferred_element_type=F32))


def _mm_body(x_ref, w_ref, o_ref, *scratch, nk):
    prod = jnp.dot(x_ref[...], w_ref[...], preferred_element_type=F32)
    if nk == 1:
        o_ref[...] = prod.astype(o_ref.dtype)
        return
    acc_ref, = scratch
    k = pl.program_id(2)

    @pl.when(k == 0)
    def _():
        acc_ref[...] = prod

    @pl.when(k > 0)
    def _():
        acc_ref[...] += prod

    @pl.when(k == nk - 1)
    def _():
        o_ref[...] = acc_ref[...].astype(o_ref.dtype)


def _pick(n, pref):
    if n <= pref:
        return n
    t = pref
    while t >= LANES:
        if n % t == 0:
            return t
        t -= LANES
    return n


def matmul(x, w, out_dtype=BF16, tm=1024, tn=512, tk=4096):
    M, K = x.shape
    N = w.shape[1]
    tm, tn, tk = _pick(M, tm), _pick(N, tn), _pick(K, tk)
    nk = K // tk
    scratch = [] if nk == 1 else [pltpu.VMEM((tm, tn), F32)]
    return pl.pallas_call(
        functools.partial(_mm_body, nk=nk),
        grid=(M // tm, N // tn, nk),
        in_specs=[pl.BlockSpec((tm, tk), lambda i, j, k: (i, k)),
                  pl.BlockSpec((tk, tn), lambda i, j, k: (k, j))],
        out_specs=pl.BlockSpec((tm, tn), lambda i, j, k: (i, j)),
        out_shape=jax.ShapeDtypeStruct((M, N), out_dtype),
        scratch_shapes=scratch,
        compiler_params=_cparams("parallel", "parallel", "arbitrary"),
        name="matmul",
    )(x, w)


def _mm_pair_body(xa_ref, xb_ref, wa_ref, wb_ref, o_ref):
    acc = jnp.dot(xa_ref[...], wa_ref[...], preferred_element_type=F32)
    acc = acc + jnp.dot(xb_ref[...], wb_ref[...], preferred_element_type=F32)
    o_ref[...] = acc.astype(o_ref.dtype)


def matmul_pair(xa, xb, w, out_dtype=BF16, tm=1024, tn=512):
    M, Kh = xa.shape
    assert xb.shape == xa.shape and w.shape[0] == 2 * Kh
    N = w.shape[1]
    tm, tn = _pick(M, tm), _pick(N, tn)
    xs = pl.BlockSpec((tm, Kh), lambda i, j: (i, 0))
    return pl.pallas_call(
        _mm_pair_body,
        grid=(M // tm, N // tn),
        in_specs=[xs, xs, pl.BlockSpec((Kh, tn), lambda i, j: (0, j)), pl.BlockSpec((Kh, tn), lambda i, j: (1, j))],
        out_specs=pl.BlockSpec((tm, tn), lambda i, j: (i, j)),
        out_shape=jax.ShapeDtypeStruct((M, N), out_dtype),
        compiler_params=_cparams("parallel", "parallel"),
        name="matmul_pair",
    )(xa, xb, w, w)


def _mm_f32w_body(x_ref, w_ref, o_ref, wb_ref):
    @pl.when(pl.program_id(1) == 0)
    def _():
        wb_ref[...] = w_ref[0].astype(BF16)

    o_ref[...] = jnp.dot(x_ref[...], wb_ref[...], preferred_element_type=F32).astype(o_ref.dtype)


def matmul_f32w(x, w, layer, out_dtype=BF16, tm=1024, tn=512, col0=0, n_cols=None):
    M, K = x.shape
    N = w.shape[2] if n_cols is None else n_cols
    tm, tn = _pick(M, tm), _pick(N, tn)
    assert col0 % tn == 0
    cb = col0 // tn
    return pl.pallas_call(
        _mm_f32w_body,
        grid=(N // tn, M // tm),
        in_specs=[pl.BlockSpec((tm, K), lambda j, i: (i, 0)),
                  pl.BlockSpec((1, K, tn), lambda j, i: (layer, 0, j + cb))],
        out_specs=pl.BlockSpec((tm, tn), lambda j, i: (i, j)),
        out_shape=jax.ShapeDtypeStruct((M, N), out_dtype),
        scratch_shapes=[pltpu.VMEM((K, tn), BF16)],
        compiler_params=_cparams("parallel", "arbitrary"),
        name="matmul_f32w",
    )(x, w)


def _layer_norm(y, g, b):
    mu = jnp.mean(y, axis=-1, keepdims=True)
    d = y - mu
    var = jnp.mean(d * d, axis=-1, keepdims=True)
    return d * lax.rsqrt(var + LN_EPS) * g + b


def _add_ln_body(h_ref, m_ref, g_ref, b_ref, o32_ref, o16_ref, *, alpha):
    out = _layer_norm(alpha * h_ref[...] + m_ref[...].astype(F32), g_ref[...], b_ref[...])
    o32_ref[...] = out
    o16_ref[...] = out.astype(BF16)


def add_layer_norm(h, mix, g, b, alpha, tm=256):
    M, D = h.shape
    tm = _pick(M, tm)
    row = pl.BlockSpec((tm, D), lambda i: (i, 0))
    vec = pl.BlockSpec((1, D), lambda i: (0, 0))
    return pl.pallas_call(
        functools.partial(_add_ln_body, alpha=alpha),
        grid=(M // tm,),
        in_specs=[row, row, vec, vec],
        out_specs=[row, row],
        out_shape=[jax.ShapeDtypeStruct((M, D), F32), jax.ShapeDtypeStruct((M, D), BF16)],
        compiler_params=_cparams("parallel"),
        name="add_layer_norm",
    )(h, mix, g.reshape(1, D).astype(F32), b.reshape(1, D).astype(F32))


def _hgrn_body(q_ref, f_ref, i_ref, g_ref, lb_ref, nw_ref, o_ref, st_ref, *, ts, layer_idx, nh):
    C = HGRN_SUB
    nsub = ts // C
    heads = [slice(h * HEAD_DIM, (h + 1) * HEAD_DIM) for h in range(nh)]

    @pl.when(pl.program_id(2) == 0)
    def _():
        st_ref[...] = jnp.zeros_like(st_ref)

    lbr = lb_ref[...]
    ex = jnp.exp(lbr - jnp.max(lbr, axis=0, keepdims=True))
    sm = ex / jnp.sum(ex, axis=0, keepdims=True)
    lb = jnp.sum(sm[:layer_idx + 1], axis=0, keepdims=True)

    q = q_ref[0].astype(F32)
    z = f_ref[0].astype(F32)
    v = i_ref[0]
    e = jnp.exp(-jnp.abs(z))
    r = 1.0 / (1.0 + e)
    pos = z >= 0
    sig = jnp.where(pos, r, e * r)
    nsig = jnp.where(pos, e * r, r)
    logf = jnp.log(lb + (1.0 - lb) * sig)
    k = (1.0 - lb) * nsig

    hi = logf.astype(BF16)
    lo = (logf - hi.astype(F32)).astype(BF16)
    row = lax.broadcasted_iota(jnp.int32, (ts, ts), 0)
    col = lax.broadcasted_iota(jnp.int32, (ts, ts), 1)

    def cumdot(m01):
        return jnp.dot(m01, hi, preferred_element_type=F32) + jnp.dot(m01, lo, preferred_element_type=F32)

    def whole_tile(sts):
        b = cumdot(jnp.where(row >= col, 1.0, 0.0).astype(BF16))
        bl = b[ts - 1:ts]
        qd = (q * jnp.exp(b)).astype(BF16)
        kinv = (k * jnp.exp(-b)).astype(BF16)
        kd = (k * jnp.exp(bl - b)).astype(BF16)
        dec = jnp.exp(bl)
        outs, new = [], []
        for h, sl in enumerate(heads):
            dmat = lax.dot_general(qd[:, sl], kinv[:, sl], _NT, preferred_element_type=F32)
            dmat = jnp.where(row >= col, dmat, 0.0)
            o = jnp.dot(dmat.astype(BF16), v[:, sl], preferred_element_type=F32)
            outs.append(o + lax.dot_general(qd[:, sl], sts[h].astype(BF16), _NT, preferred_element_type=F32))
            new.append(sts[h] * dec[:, sl] + lax.dot_general(v[:, sl], kd[:, sl], _TN, preferred_element_type=F32))
        return tuple(outs), tuple(new)

    def sub_chunks(sts):
        same = (row // C) == (col // C)
        b = cumdot(jnp.where(same & (row >= col), 1.0, 0.0).astype(BF16))
        bl = cumdot(jnp.where(same, 1.0, 0.0).astype(BF16))
        qd = (q * jnp.exp(b)).astype(BF16)
        kd = (k * jnp.exp(bl - b)).astype(BF16)
        dec = jnp.exp(bl)
        tri = (lax.broadcasted_iota(jnp.int32, (C, C, HEAD_DIM), 0)
               >= lax.broadcasted_iota(jnp.int32, (C, C, HEAD_DIM), 1))
        outs, new = [], []
        for h, hs in enumerate(heads):
            st = sts[h]
            parts = []
            for n in range(nsub):
                sl = slice(n * C, (n + 1) * C)
                bn, qn, kn = b[sl, hs], q[sl, hs], k[sl, hs]
                diff = bn[:, None, :] - bn[None, :, :]
                ee = jnp.exp(jnp.where(tri, diff, NEG_INF))
                dmat = jnp.sum(qn[:, None, :] * (kn[None, :, :] * ee), axis=-1)
                o_n = jnp.dot(dmat.astype(BF16), v[sl, hs], preferred_element_type=F32)
                o_n = o_n + lax.dot_general(qd[sl, hs], st.astype(BF16), _NT, preferred_element_type=F32)
                upd = lax.dot_general(v[sl, hs], kd[sl, hs], _TN, preferred_element_type=F32)
                st = st * dec[n * C:n * C + 1, hs] + upd
                parts.append(o_n)
            outs.append(jnp.concatenate(parts, axis=0))
            new.append(st)
        return tuple(outs), tuple(new)

    tile_decay = jnp.min(jnp.sum(logf, axis=0, keepdims=True))
    outs, sts = lax.cond(tile_decay > HGRN_SAFE_LOG_DECAY, whole_tile, sub_chunks,
                         tuple(st_ref[h] for h in range(nh)))
    gt = g_ref[0].astype(F32)
    gate = gt / (1.0 + jnp.exp(-gt))
    for h, sl in enumerate(heads):
        st_ref[h] = sts[h]
        o = outs[h]
        o = o * lax.rsqrt(jnp.mean(o * o, axis=-1, keepdims=True) + RMS_EPS) * nw_ref[...] * gate[:, sl]
        o_ref[0, :, sl] = o.astype(o_ref.dtype)


def hgrn2(proj, lb_raw, norm_w, n_heads, layer_idx, ts=128, nh=16):
    B, S, _ = proj.shape
    ts = _pick(S, ts)
    nh = min(nh, n_heads)
    assert n_heads % nh == 0
    H = n_heads // nh
    L = lb_raw.shape[0]
    w = nh * HEAD_DIM

    def col(off):
        return pl.BlockSpec((1, ts, w), lambda b, h, s: (b, s, off * H + h))

    return pl.pallas_call(
        functools.partial(_hgrn_body, ts=ts, layer_idx=layer_idx, nh=nh),
        grid=(B, H, S // ts),
        in_specs=[col(0), col(1), col(2), col(3),
                  pl.BlockSpec((L, w), lambda b, h, s: (0, h)),
                  pl.BlockSpec((1, HEAD_DIM), lambda b, h, s: (0, 0))],
        out_specs=pl.BlockSpec((1, ts, w), lambda b, h, s: (b, s, h)),
        out_shape=jax.ShapeDtypeStruct((B, S, n_heads * HEAD_DIM), BF16),
        scratch_shapes=[pltpu.VMEM((nh, HEAD_DIM, HEAD_DIM), F32)],
        compiler_params=_cparams("parallel", "parallel", "arbitrary"),
        name="hgrn2",
    )(proj, proj, proj, proj, lb_raw.astype(F32), norm_w.reshape(1, HEAD_DIM).astype(F32))


def _sb_block(q, k, v, carry, after01, scale, mask):
    z = lax.dot_general(q, k, _NT, preferred_element_type=F32) * scale
    sp = jnp.maximum(z, jnp.log(1.0 + jnp.exp(jnp.minimum(z, SOFTPLUS_CLAMP))))
    spm = sp if mask is None else jnp.where(mask, sp, 0.0)
    sub = after01.shape[0]
    parts, tail = [], 0.0
    for c in reversed(range(spm.shape[1] // sub)):
        blk = spm[:, c * sub:(c + 1) * sub]
        part = _split_dot(blk, after01) + tail
        parts.insert(0, part)
        tail = part[:, 0:1] + blk[:, 0:1]
    rev = parts[0] if len(parts) == 1 else jnp.concatenate(parts, axis=1)
    w = jnp.exp(z - sp - rev - carry)
    if mask is not None:
        w = jnp.where(mask, w, 0.0)
    contrib = jnp.dot(w.astype(BF16), v, preferred_element_type=F32)
    return contrib, carry + tail


def _sb_body(q_ref, k_ref, v_ref, o_ref, *, tq, first, scale, nh):
    qi = pl.program_id(2)
    heads = [slice(h * HEAD_DIM, (h + 1) * HEAD_DIM) for h in range(nh)]
    qs = [q_ref[0, :, sl] for sl in heads]

    def after01(width):
        return jnp.where(lax.broadcasted_iota(jnp.int32, (width, width), 0)
                         > lax.broadcasted_iota(jnp.int32, (width, width), 1), 1.0, 0.0).astype(BF16)

    def blocks(k0, width, after, carries, mask):
        res = [_sb_block(qs[h], k_ref[0, pl.ds(k0, width), heads[h]], v_ref[0, pl.ds(k0, width), heads[h]],
                         carries[h], after, scale, mask) for h in range(nh)]
        return tuple(r[0] for r in res), tuple(r[1] for r in res)

    k0 = pl.multiple_of(jnp.maximum((qi + 1) * tq - first, 0), tq)
    t = qi * tq + lax.broadcasted_iota(jnp.int32, (tq, first), 0)
    kpos = k0 + lax.broadcasted_iota(jnp.int32, (tq, first), 1)
    after_tq = after01(tq)
    accs, carries = blocks(k0, first, after_tq, (jnp.zeros((tq, 1), F32),) * nh, kpos < t)

    def cond(c):
        j, _, carries = c
        low = functools.reduce(jnp.minimum, [jnp.min(cr) for cr in carries])
        return jnp.logical_and(j >= 0, low < -EXP_ZERO_BELOW)

    def body(c):
        j, accs, carries = c
        contribs, carries = blocks(pl.multiple_of(j * tq, tq), tq, after_tq, carries, None)
        return j - 1, tuple(a + cb for a, cb in zip(accs, contribs)), carries

    _, accs, _ = lax.while_loop(cond, body, (k0 // tq - 1, accs, carries))
    for h in range(nh):
        o_ref[0, :, heads[h]] = accs[h].astype(o_ref.dtype)


def stick_breaking(proj, col_q, col_k, col_v, n_heads, tq=256, nh=4):
    B, S, _ = proj.shape
    tq = _pick(S, tq)
    first = min(2 * tq, S)
    assert n_heads % nh == 0 and col_q % nh == 0 and col_k % nh == 0 and col_v % nh == 0
    w = nh * HEAD_DIM
    full = lambda off: pl.BlockSpec((1, S, w), lambda b, h, i: (b, 0, off // nh + h))
    return pl.pallas_call(
        functools.partial(_sb_body, tq=tq, first=first, scale=HEAD_DIM ** -0.5, nh=nh),
        grid=(B, n_heads // nh, S // tq),
        in_specs=[pl.BlockSpec((1, tq, w), lambda b, h, i: (b, i, col_q // nh + h)),
                  full(col_k), full(col_v)],
        out_specs=pl.BlockSpec((1, tq, w), lambda b, h, i: (b, i, h)),
        out_shape=jax.ShapeDtypeStruct((B, S, n_heads * HEAD_DIM), BF16),
        compiler_params=_cparams("parallel", "parallel", "arbitrary"),
        name="stick_breaking",
    )(proj, proj, proj)


def _rope_body(x_ref, cos_ref, sin_ref, o_ref, *, n_heads):
    cos = cos_ref[...]
    sin = sin_ref[...]
    for h in range(n_heads):
        sl = slice(h * HEAD_DIM, (h + 1) * HEAD_DIM)
        t = x_ref[0, :, sl].astype(F32)
        o_ref[0, :, sl] = (t * cos + pltpu.roll(t, HEAD_DIM // 2, 1) * sin).astype(o_ref.dtype)


def rope(x, col0, n_heads, cos, sin, ts=256):
    B, S, _ = x.shape
    ts = _pick(S, ts)
    w = n_heads * HEAD_DIM
    assert col0 % w == 0
    cb = col0 // w
    tab = pl.BlockSpec((ts, HEAD_DIM), lambda b, s: (s, 0))
    return pl.pallas_call(
        functools.partial(_rope_body, n_heads=n_heads),
        grid=(B, S // ts),
        in_specs=[pl.BlockSpec((1, ts, w), lambda b, s: (b, s, cb)), tab, tab],
        out_specs=pl.BlockSpec((1, ts, w), lambda b, s: (b, s, 0)),
        out_shape=jax.ShapeDtypeStruct((B, S, w), BF16),
        compiler_params=_cparams("parallel", "parallel"),
        name="rope",
    )(x, cos, sin)


def _gelu(x):
    return 0.5 * x * (1.0 + lax.erf(x * (2.0 ** -0.5)))


def _compress_body(x_ref, pe_ref, w1_ref, w2_ref, o_ref, *, batch, k_mult):
    x = x_ref[0, 0]
    half = x.shape[1]
    n16 = x.shape[0]
    y1 = jnp.dot(x, w1_ref[0, :half], preferred_element_type=F32)
    y2 = jnp.dot(x, w1_ref[0, half:], preferred_element_type=F32)
    bias = jnp.dot(pe_ref[0], w1_ref[0], preferred_element_type=F32)[0:1]
    hid = _gelu(y1 + pltpu.roll(y2, n16 - 1, 0) + bias)
    out = jnp.dot(hid.astype(BF16), w2_ref[0], preferred_element_type=F32)
    mult = jnp.where(pl.program_id(0) < batch, k_mult, 1.0)
    o_ref[0, 0] = (out * mult).astype(o_ref.dtype)


def compress(x16, pe, w1, w2, k_mult):
    two, B, G, n16, wide = x16.shape
    x16 = x16.reshape(two * B, G, n16, wide)
    out = pl.pallas_call(
        functools.partial(_compress_body, batch=B, k_mult=k_mult),
        grid=(two * B, G),
        in_specs=[pl.BlockSpec((1, 1, n16, wide), lambda i, g: (i, g, 0, 0)),
                  pl.BlockSpec((1, 8, 2 * wide), lambda i, g: (i // B, 0, 0)),
                  pl.BlockSpec((1, 2 * wide, HEAD_DIM), lambda i, g: (i // B, 0, 0)),
                  pl.BlockSpec((1, HEAD_DIM, HEAD_DIM), lambda i, g: (i // B, 0, 0))],
        out_specs=pl.BlockSpec((1, 1, n16, HEAD_DIM), lambda i, g: (i, g, 0, 0)),
        out_shape=jax.ShapeDtypeStruct((two * B, G, n16, HEAD_DIM), BF16),
        compiler_params=_cparams("parallel", "parallel"),
        name="nsa_compress",
    )(x16, pe, w1, w2)
    return out.reshape(two, B, G, n16, HEAD_DIM)


def _stack_heads(q_ref, rep):
    return jnp.concatenate([q_ref[0, :, r * HEAD_DIM:(r + 1) * HEAD_DIM] for r in range(rep)], axis=0)


def _cmp_body(q_ref, kc_ref, vc_ref, o_ref, imp_ref, *, tq, rep):
    q0 = pl.program_id(2) * tq
    n16 = kc_ref.shape[2]
    cols = [slice(r * tq, (r + 1) * tq) for r in range(rep)]
    q2 = _stack_heads(q_ref, rep)

    def attend(nb):
        st = lax.dot_general(kc_ref[0, 0, :nb], q2, _NT, preferred_element_type=F32)
        n = lax.broadcasted_iota(jnp.int32, (nb, tq), 0)
        t = q0 + lax.broadcasted_iota(jnp.int32, (nb, tq), 1)
        bias = jnp.where(n * CMP_STRIDE + (CMP_LEN - 1) <= t, 0.0, NEG_INF)
        ps = []
        psum = jnp.zeros((nb, tq), F32)
        for r in range(rep):
            s = st[:, cols[r]] + bias
            m = jnp.maximum(jnp.max(s, axis=0, keepdims=True), 0.1 * NEG_INF)
            p = jnp.exp2(s - m)
            den = jnp.sum(p, axis=0, keepdims=True)
            pn = p * (1.0 / jnp.where(den > 0.0, den, 1.0))
            ps.append(pn.astype(BF16))
            psum = psum + pn
        ot = lax.dot_general(vc_ref[0, 0, :nb], jnp.concatenate(ps, axis=1), _TN,
                             preferred_element_type=F32)
        for r in range(rep):
            o_ref[0, :, r * HEAD_DIM:(r + 1) * HEAD_DIM] = ot[:, cols[r]].T.astype(o_ref.dtype)

        cj = lax.broadcasted_iota(jnp.int32, (LANES, nb), 0) * SLC_LEN
        cn = lax.broadcasted_iota(jnp.int32, (LANES, nb), 1) * CMP_STRIDE
        ov01 = jnp.where((cn < cj + SLC_LEN) & (cn + CMP_LEN > cj), 1.0, 0.0).astype(BF16)
        hi = psum.astype(BF16)
        lo = (psum - hi.astype(F32)).astype(BF16)
        imp_t = jnp.dot(ov01, hi, preferred_element_type=F32) + jnp.dot(ov01, lo, preferred_element_type=F32)
        imp_ref[0, 0] = imp_t.T

    n_chunks = max(n16 // LANES, 1)
    chunk = n16 // n_chunks
    last_valid = (q0 + tq - CMP_LEN) // CMP_STRIDE
    need = jnp.clip(last_valid // chunk + 1, 1, n_chunks)
    for c in range(1, n_chunks + 1):
        pl.when(need == c)(functools.partial(attend, c * chunk))


def cmp_attention(q, kc, vc, rep, tq=256):
    B, S, _ = q.shape
    G, n16 = kc.shape[1], kc.shape[2]
    tq = _pick(S, tq)
    w = rep * HEAD_DIM
    kv = pl.BlockSpec((1, 1, n16, HEAD_DIM), lambda b, g, i: (b, g, 0, 0))
    return pl.pallas_call(
        functools.partial(_cmp_body, tq=tq, rep=rep),
        grid=(B, G, S // tq),
        in_specs=[pl.BlockSpec((1, tq, w), lambda b, g, i: (b, i, g)), kv, kv],
        out_specs=[pl.BlockSpec((1, tq, w), lambda b, g, i: (b, i, g)),
                   pl.BlockSpec((1, 1, tq, LANES), lambda b, g, i: (b, g, i, 0))],
        out_shape=[jax.ShapeDtypeStruct((B, S, G * w), BF16),
                   jax.ShapeDtypeStruct((B, G, S, LANES), F32)],
        compiler_params=_cparams("parallel", "parallel", "parallel"),
        name="nsa_cmp",
    )(q, kc, vc)


def _topk_body(imp_ref, sel_ref, *, tq, n_slc):
    q0 = pl.program_id(2) * tq
    imp = imp_ref[0, 0]
    tt = q0 + lax.broadcasted_iota(jnp.int32, (tq, LANES), 0)
    j = lax.broadcasted_iota(jnp.int32, (tq, LANES), 1)
    cur = tt // SLC_LEN
    forced = (j == 0) | (j == cur) | (j == cur - 1)
    allowed = j * SLC_LEN <= tt
    score = jnp.where(forced, FORCE_SCORE, jnp.where(allowed, imp, -1.0))
    score = jnp.where(j < n_slc, score, -jnp.inf)
    jf = j.astype(F32)
    sel = jnp.zeros((tq, LANES), F32)
    for _ in range(min(SLC_TOP, n_slc)):
        m = jnp.max(score, axis=-1, keepdims=True)
        first = jnp.min(jnp.where(score == m, jf, float(LANES)), axis=-1, keepdims=True)
        pick = jf == first
        sel = jnp.where(pick, 1.0, sel)
        score = jnp.where(pick, -jnp.inf, score)
    sel_ref[0, 0] = sel.astype(sel_ref.dtype)


def select_blocks(imp, tq=1024):
    B, G, S, _ = imp.shape
    n_slc = S // SLC_LEN
    assert n_slc <= LANES
    tq = _pick(S, tq)
    spec = pl.BlockSpec((1, 1, tq, LANES), lambda b, g, i: (b, g, i, 0))
    return pl.pallas_call(
        functools.partial(_topk_body, tq=tq, n_slc=n_slc),
        grid=(B, G, S // tq),
        in_specs=[spec],
        out_specs=spec,
        out_shape=jax.ShapeDtypeStruct((B, G, S, LANES), BF16),
        compiler_params=_cparams("parallel", "parallel", "parallel"),
        name="nsa_topk",
    )(imp)


def _gqa_body(*refs, tq, tk, rep, mode, mult):
    q0 = pl.program_id(2) * tq
    q_ref, cos_ref, sin_ref, k_ref, v_ref = refs[:5]
    cos = cos_ref[...] * mult
    sin = sin_ref[...] * mult
    rot = []
    for r in range(rep):
        x = q_ref[0, :, r * HEAD_DIM:(r + 1) * HEAD_DIM].astype(F32)
        rot.append((x * cos + pltpu.roll(x, HEAD_DIM // 2, 1) * sin).astype(BF16))
    q2 = jnp.concatenate(rot, axis=0)
    if mode == "sel":
        sel_ref, o_ref, s_ref, acc_ref = refs[5:]
        key_blk = lax.broadcasted_iota(jnp.int32, (tk, LANES), 0) // SLC_LEN
        lane_blk = lax.broadcasted_iota(jnp.int32, (tk, LANES), 1)
        unpicked = ((sel_ref[0, 0].astype(F32) - 1.0) * (-NEG_INF)).astype(BF16)
        q2 = jnp.concatenate([q2, jnp.concatenate([unpicked] * rep, axis=0)], axis=1)
    else:
        o_ref, s_ref, acc_ref = refs[5:]
    kpos = lax.broadcasted_iota(jnp.int32, (tk, tq), 0)
    t = q0 + lax.broadcasted_iota(jnp.int32, (tk, tq), 1)
    cols = [slice(r * tq, (r + 1) * tq) for r in range(rep)]

    def put_scores(slot, kj):
        k0 = pl.multiple_of(kj * tk, tk)
        keys = k_ref[0, pl.ds(k0, tk), :]
        if mode == "sel":
            onehot = jnp.where(key_blk + kj * (tk // SLC_LEN) == lane_blk, 1.0, 0.0).astype(BF16)
            keys = jnp.concatenate([keys, onehot], axis=1)
        s_ref[slot] = lax.dot_general(keys, q2, _NT, preferred_element_type=F32)

    def tile(slot, kj, m, l, diagonal):
        k0 = pl.multiple_of(kj * tk, tk)
        kp = k0 + kpos
        if mode == "sel":
            bias = jnp.where(kp <= t, 0.0, NEG_INF) if diagonal else None
        else:
            ok = kp > t - WINDOW
            if diagonal:
                ok = ok & (kp <= t)
            bias = jnp.where(ok, 0.0, NEG_INF)
        ps, m_new, l_new, scale = [], [], [], []
        for r in range(rep):
            s = s_ref[slot, :, cols[r]]
            if bias is None:
                mr = jnp.maximum(m[r], jnp.max(s, axis=0, keepdims=True))
                p = jnp.exp2(s - mr)
            else:
                mr = jnp.maximum(m[r], jnp.max(s + bias, axis=0, keepdims=True))
                p = jnp.exp2((s - mr) + bias)
            a = jnp.exp2(m[r] - mr)
            ps.append(p.astype(BF16))
            m_new.append(mr)
            l_new.append(a * l[r] + jnp.sum(p, axis=0, keepdims=True))
            scale.append(a)
        pt = jnp.concatenate(ps, axis=1)
        pv = lax.dot_general(v_ref[0, pl.ds(k0, tk), :], pt, _TN, preferred_element_type=F32)
        acc_ref[...] = jnp.concatenate(scale, axis=1) * acc_ref[...] + pv
        return tuple(m_new), tuple(l_new)

    kd = q0 // tk
    lo = 0 if mode == "sel" else jnp.maximum(q0 - (WINDOW - 1), 0) // tk
    put_scores(0, kd)
    put_scores(1, lo)
    acc_ref[...] = jnp.zeros_like(acc_ref)
    stats = tile(0, kd, (jnp.full((1, tq), NEG_INF, F32),) * rep, (jnp.zeros((1, tq), F32),) * rep, True)

    def pair(i, stats):
        kj = lo + 2 * i
        put_scores(0, kj + 1)
        stats = tile(1, kj, *stats, False)
        put_scores(1, kj + 2)
        return tile(0, kj + 1, *stats, False)

    n_off = kd - lo
    stats = lax.fori_loop(0, n_off // 2, pair, stats)
    m, l = lax.cond(n_off % 2 == 1, lambda st: tile(1, kd - 1, *st, False), lambda st: st, stats)
    for r in range(rep):
        out = acc_ref[:, cols[r]] * (1.0 / l[r])
        o_ref[0, :, r * HEAD_DIM:(r + 1) * HEAD_DIM] = out.T.astype(o_ref.dtype)


def gqa_attention(q, cos, sin, mult, k, v, v_col0, rep, mode, sel=None, tq=128, tk=512):
    B, S, _ = q.shape
    G = k.shape[2] // HEAD_DIM
    tq, tk = _pick(S, tq), _pick(S, tk)
    assert tk % tq == 0
    w = rep * HEAD_DIM
    tab = pl.BlockSpec((tq, HEAD_DIM), lambda b, g, i: (i, 0))
    in_specs = [pl.BlockSpec((1, tq, w), lambda b, g, i: (b, i, g)), tab, tab,
                pl.BlockSpec((1, S, HEAD_DIM), lambda b, g, i: (b, 0, g)),
                pl.BlockSpec((1, S, HEAD_DIM), lambda b, g, i: (b, 0, v_col0 + g))]
    args = [q, cos, sin, k, v]
    scratch = [pltpu.VMEM((2, tk, rep * tq), F32), pltpu.VMEM((HEAD_DIM, rep * tq), F32)]
    if mode == "sel":
        assert S // SLC_LEN <= LANES and tk % SLC_LEN == 0
        in_specs += [pl.BlockSpec((1, 1, tq, LANES), lambda b, g, i: (b, g, i, 0))]
        args += [sel]
    return pl.pallas_call(
        functools.partial(_gqa_body, tq=tq, tk=tk, rep=rep, mode=mode, mult=mult),
        grid=(B, G, S // tq),
        in_specs=in_specs,
        out_specs=pl.BlockSpec((1, tq, w), lambda b, g, i: (b, i, g)),
        out_shape=jax.ShapeDtypeStruct((B, S, G * w), BF16),
        scratch_shapes=scratch,
        compiler_params=_cparams("parallel", "parallel", "arbitrary"),
        name="nsa_" + mode,
    )(*args)


def _nsa_gate_body(gl_ref, oc_ref, os_ref, ow_ref, o_ref, *, n_heads):
    ng = 3 * n_heads
    gl = gl_ref[:, :ng].astype(F32)
    gate = 1.0 / (1.0 + jnp.exp(-gl))
    src = lax.broadcasted_iota(jnp.int32, (ng, n_heads * HEAD_DIM), 0)
    head = lax.broadcasted_iota(jnp.int32, (ng, n_heads * HEAD_DIM), 1) // HEAD_DIM
    out = None
    for c, ref in enumerate((oc_ref, os_ref, ow_ref)):
        spread01 = jnp.where(src == head * 3 + c, 1.0, 0.0).astype(BF16)
        term = _split_dot(gate, spread01) * ref[...].astype(F32)
        out = term if out is None else out + term
    o_ref[...] = out.astype(o_ref.dtype)


def nsa_gate(gl, oc, os_, ow, n_heads, tm=256):
    M, W = oc.shape
    tm = _pick(M, tm)
    row = pl.BlockSpec((tm, W), lambda i: (i, 0))
    return pl.pallas_call(
        functools.partial(_nsa_gate_body, n_heads=n_heads),
        grid=(M // tm,),
        in_specs=[pl.BlockSpec((tm, gl.shape[1]), lambda i: (i, 0)), row, row, row],
        out_specs=row,
        out_shape=jax.ShapeDtypeStruct((M, W), BF16),
        compiler_params=_cparams("parallel"),
        name="nsa_gate",
    )(gl, oc, os_, ow)


def _xattn_body(h_ref, hb_ref, kv_ref, wq_ref, wo_ref, g_ref, b_ref, o32_ref, o16_ref, *, n_heads, scale, alpha, chunk):
    w = n_heads * HEAD_DIM
    q = jnp.dot(hb_ref[0], wq_ref[...], preferred_element_type=F32).astype(BF16)
    outs = []
    for hd in range(n_heads):
        sl = slice(hd * HEAD_DIM, (hd + 1) * HEAD_DIM)
        k = kv_ref[0, :, sl]
        v = kv_ref[0, :, w + hd * HEAD_DIM:w + (hd + 1) * HEAD_DIM]
        s = lax.dot_general(q[:, sl], k, _NT, preferred_element_type=F32) * scale
        p = jnp.exp(s - jnp.max(s, axis=-1, keepdims=True))
        p = p / jnp.sum(p, axis=-1, keepdims=True)
        outs.append(jnp.dot(p.astype(BF16), v, preferred_element_type=F32).astype(BF16))
    o = jnp.concatenate(outs, axis=1)
    for r0 in range(0, o.shape[0], chunk):
        rows = slice(r0, r0 + chunk)
        mix = jnp.dot(o[rows], wo_ref[...], preferred_element_type=F32)
        out = _layer_norm(alpha * h_ref[0, rows, :] + mix, g_ref[...], b_ref[...])
        o32_ref[0, rows, :] = out
        o16_ref[0, rows, :] = out.astype(BF16)


def mem_attention_block(h, hb, kv, w_q, w_o, g, b, n_heads, alpha, tq=256):
    B, S, D = h.shape
    n_mem = kv.shape[1]
    w = n_heads * HEAD_DIM
    tq = _pick(S, tq)
    row = pl.BlockSpec((1, tq, D), lambda bb, i: (bb, i, 0))
    vec = pl.BlockSpec((1, D), lambda bb, i: (0, 0))
    return pl.pallas_call(
        functools.partial(_xattn_body, n_heads=n_heads, scale=HEAD_DIM ** -0.5, alpha=alpha, chunk=min(XA_ROW_CHUNK, tq)),
        grid=(B, S // tq),
        in_specs=[row, row,
                  pl.BlockSpec((1, n_mem, 2 * w), lambda bb, i: (bb, 0, 0)),
                  pl.BlockSpec((D, w), lambda bb, i: (0, 0)),
                  pl.BlockSpec((w, D), lambda bb, i: (0, 0)), vec, vec],
        out_specs=[row, row],
        out_shape=[jax.ShapeDtypeStruct((B, S, D), F32), jax.ShapeDtypeStruct((B, S, D), BF16)],
        compiler_params=_cparams("parallel", "parallel"),
        name="mem_attention_block",
    )(h, hb, kv, w_q, w_o, g.reshape(1, D).astype(F32), b.reshape(1, D).astype(F32))


def _ffn_up_body(x_ref, wa_ref, wu_ref, cw_ref, o_ref, wab_ref, wub_ref, tail_ref, *, tiles_per_seq):
    i = pl.program_id(1)

    @pl.when(i == 0)
    def _():
        wab_ref[...] = wa_ref[0].astype(BF16)
        wub_ref[...] = wu_ref[0].astype(BF16)

    x = x_ref[...]
    a = jnp.dot(x, wab_ref[...], preferred_element_type=F32)
    tm = a.shape[0]
    first = (i % tiles_per_seq) == 0
    prev = jnp.where(first, 0.0, tail_ref[...])
    tail_ref[...] = a[tm - 8:]
    rowi = lax.broadcasted_iota(jnp.int32, a.shape, 0)
    a1 = jnp.where(rowi >= 1, pltpu.roll(a, 1, 0), prev[7:8])
    a2 = jnp.where(rowi >= 2, pltpu.roll(a, 2, 0), jnp.where(rowi == 1, prev[7:8], prev[6:7]))
    cw = cw_ref[...]
    gate = _gelu(cw[2:3] * a + cw[1:2] * a1 + cw[0:1] * a2)
    u = jnp.dot(x, wub_ref[...], preferred_element_type=F32)
    o_ref[...] = (gate * u).astype(o_ref.dtype)


def ffn_up_glu(x, w_up, layer, conv_w, seq_len, tm=1024, tn=256):
    M, K = x.shape
    Fd = w_up.shape[2] // 2
    tm, tn = _pick(seq_len, tm), _pick(Fd, tn)
    nj = Fd // tn
    return pl.pallas_call(
        functools.partial(_ffn_up_body, tiles_per_seq=seq_len // tm),
        grid=(nj, M // tm),
        in_specs=[pl.BlockSpec((tm, K), lambda j, i: (i, 0)),
                  pl.BlockSpec((1, K, tn), lambda j, i: (layer, 0, j)),
                  pl.BlockSpec((1, K, tn), lambda j, i: (layer, 0, j + nj)),
                  pl.BlockSpec((CONV_W, tn), lambda j, i: (0, j))],
        out_specs=pl.BlockSpec((tm, tn), lambda j, i: (i, j)),
        out_shape=jax.ShapeDtypeStruct((M, Fd), BF16),
        scratch_shapes=[pltpu.VMEM((K, tn), BF16), pltpu.VMEM((K, tn), BF16), pltpu.VMEM((8, tn), F32)],
        compiler_params=_cparams("parallel", "arbitrary"),
        name="ffn_up_glu",
    )(x, w_up, w_up, conv_w.astype(F32))


def _rope_tables(S):
    half = HEAD_DIM // 2
    inv_freq = ROPE_THETA ** (-jnp.arange(half, dtype=F32) / half)
    ang = jnp.arange(S, dtype=F32)[:, None] * inv_freq[None, :]
    cos, sin = jnp.cos(ang), jnp.sin(ang)
    return jnp.concatenate([cos, cos], axis=-1), jnp.concatenate([-sin, sin], axis=-1)


def _hgrn_sb_mixer(hb, B, S, w_in_all, lb_raw, norm_w, w_out, e):
    width = w_out.shape[0]
    a_heads = width // (2 * HEAD_DIM)
    b_heads = a_heads
    proj = matmul_f32w(hb, w_in_all, e).reshape(B, S, -1)
    o_a = hgrn2(proj, lb_raw, norm_w, a_heads, e)
    o_b = stick_breaking(proj, 4 * a_heads, 4 * a_heads + b_heads, 4 * a_heads + 2 * b_heads, b_heads)
    return matmul_pair(o_a.reshape(B * S, -1), o_b.reshape(B * S, -1), w_out.astype(BF16))


def _nsa_mixer(hb, B, S, w_in_all, o, cmp_pos, cmp_w1, cmp_w2, w_out, cos, sin):
    G = NSA_KV_HEADS
    q_w = w_out.shape[0]
    n_heads = q_w // HEAD_DIM
    rep = n_heads // G
    kv_w = G * HEAD_DIM
    main_w = q_w + 6 * kv_w
    proj = matmul_f32w(hb, w_in_all, o, n_cols=main_w).reshape(B, S, main_w)
    gl = matmul_f32w(hb, w_in_all, o, col0=main_w, n_cols=LANES, tn=LANES)

    log2_scale = HEAD_DIM ** -0.5 * LOG2E
    ks_rot = rope(proj, q_w + 2 * kv_w, G, cos, sin)
    kw_rot = rope(proj, q_w + 4 * kv_w, G, cos, sin)

    n16 = S // CMP_STRIDE
    kvc_in = proj[:, :, q_w:q_w + 2 * kv_w].reshape(B, S, 2, G, HEAD_DIM)
    x16 = kvc_in.transpose(2, 0, 3, 1, 4).reshape(2, B, G, n16, CMP_STRIDE * HEAD_DIM)
    pe = jnp.broadcast_to(cmp_pos.reshape(2, 1, CMP_LEN * HEAD_DIM), (2, 8, CMP_LEN * HEAD_DIM)).astype(BF16)
    w1 = cmp_w1.reshape(2, CMP_LEN * HEAD_DIM, HEAD_DIM).astype(BF16)
    kvc = compress(x16, pe, w1, cmp_w2.astype(BF16), log2_scale)

    o_c, imp = cmp_attention(proj, kvc[0], kvc[1], rep)
    sel = select_blocks(imp)
    col = lambda off: (q_w + off * kv_w) // HEAD_DIM
    o_s = gqa_attention(proj, cos, sin, log2_scale, ks_rot, proj, col(3), rep, "sel", sel=sel, tq=256, tk=512)
    o_w = gqa_attention(proj, cos, sin, log2_scale, kw_rot, proj, col(5), rep, "win", tq=256, tk=256)
    o = nsa_gate(gl, o_c.reshape(B * S, q_w), o_s.reshape(B * S, q_w), o_w.reshape(B * S, q_w), n_heads)
    return matmul(o, w_out.astype(BF16))


def kernel(x, mem, ab_w_in, hgrn_lb, hgrn_norm_w, ab_w_out, nsa_w_in, nsa_cmp_pos, nsa_cmp_w1,
           nsa_cmp_w2, nsa_w_out, xa_w_q, xa_w_kv, xa_w_o, ffn_w_up, ffn_conv, ffn_w_down, ln_g, ln_b):
    B, S, D = x.shape
    depth = ln_g.shape[0]
    alpha = (2 * depth) ** 0.25
    n_mem = mem.shape[1]
    cos, sin = _rope_tables(S)
    h = x.reshape(B * S, D).astype(F32)
    hb = h.astype(BF16)
    memb = mem.reshape(B * n_mem, D).astype(BF16)
    for layer in range(depth):
        if layer % 2 == 0:
            e = layer // 2
            mix = _hgrn_sb_mixer(hb, B, S, ab_w_in, hgrn_lb, hgrn_norm_w[e], ab_w_out[e], e)
        else:
            o = layer // 2
            mix = _nsa_mixer(hb, B, S, nsa_w_in, o, nsa_cmp_pos[o], nsa_cmp_w1[o], nsa_cmp_w2[o],
                             nsa_w_out[o], cos, sin)
        h, hb = add_layer_norm(h, mix, ln_g[layer, 0], ln_b[layer, 0], alpha)

        xkv = matmul(memb, xa_w_kv[layer].astype(BF16)).reshape(B, n_mem, -1)
        h3, hb3 = mem_attention_block(h.reshape(B, S, D), hb.reshape(B, S, D), xkv, xa_w_q[layer].astype(BF16),
                                      xa_w_o[layer].astype(BF16), ln_g[layer, 1], ln_b[layer, 1], XA_HEADS, alpha)
        h, hb = h3.reshape(B * S, D), hb3.reshape(B * S, D)

        gated = ffn_up_glu(hb, ffn_w_up, layer, ffn_conv[layer], S)
        h, hb = add_layer_norm(h, matmul(gated, ffn_w_down[layer].astype(BF16), tm=512, tk=gated.shape[1]),
                               ln_g[layer, 2], ln_b[layer, 2], alpha)
    return h.reshape(B, S, D).astype(x.dtype)
```
